```python
import math
import jax, jax.numpy as jnp
from jax import lax
import numpy as np

D_MODEL = 1024
BATCH = 4
SEQ = 8192
DEPTH = 1

CONV_WIDTH = D_MODEL // 2
CONV_KERNEL = 31
HY_WIDTH = D_MODEL // 2
HY_ORDER = 2
HY_SHORT = 3
HY_EMB = 33
HY_FFN = 64
HY_DECAY_TARGET = 1e-2
HY_FAST_DECAY = 0.3
HY_SLOW_DECAY = 1.5
N_MEM = 256
XA_HEADS = 4
XA_HEAD_DIM = D_MODEL // XA_HEADS
N_EXPERTS = 16
EC_CAPACITY = 2
D_FF_EXPERT = 2 * D_MODEL
COL_GLU = 2 * CONV_WIDTH
COL_HY = 3 * HY_WIDTH
COL_GATE = 2 * D_MODEL
IN_COLS = COL_GLU + COL_HY + COL_GATE
LN_EPS = 1e-5
DN_ALPHA = (2.0 * DEPTH) ** 0.25
DN_BETA = (8.0 * DEPTH) ** -0.25

kernel_name = 'hybrid_conformer_hyena_ec_moe_encoder'


def layer_norm(x, g, b):
    xf = x.astype(jnp.float32)
    mu = jnp.mean(xf, axis=-1, keepdims=True)
    var = jnp.mean(jnp.square(xf - mu), axis=-1, keepdims=True)
    y = (xf - mu) * lax.rsqrt(var + LN_EPS)
    return (y * g.astype(jnp.float32) + b.astype(jnp.float32)).astype(x.dtype)


def depthwise_conv(u, w, b):
    k = w.shape[0]
    pad = (k - 1) // 2
    y = lax.conv_general_dilated(
        u, w[:, None, :].astype(u.dtype), window_strides=(1,), padding=[(pad, k - 1 - pad)],
        dimension_numbers=('NWC', 'WIO', 'NWC'), feature_group_count=u.shape[-1])
    return y + b.astype(u.dtype)


def hyena_positional_features(length):
    n = jnp.arange(length, dtype=jnp.float32)
    t = n / max(length - 1, 1)
    bands = (HY_EMB - 1) // 2
    f = jnp.linspace(1e-4, bands - 1, bands, dtype=jnp.float32)
    ang = (2.0 * math.pi * n / length)[:, None] * f[None, :]
    feat = jnp.concatenate([t[:, None], jnp.cos(ang), -jnp.sin(ang)], axis=-1)
    return feat, t


def hyena_kernels(feat, t, w1, b1, f1, w2, b2, f2, w3):
    f32 = jnp.float32
    h = jnp.sin(f1.astype(f32) * (feat @ w1.astype(f32) + b1.astype(f32)))
    h = jnp.sin(f2.astype(f32) * (h @ w2.astype(f32) + b2.astype(f32)))
    h = h @ w3.astype(f32)
    length = feat.shape[0]
    h = h.reshape(length, HY_ORDER, 2, HY_WIDTH)
    max_decay = math.log(HY_DECAY_TARGET) / HY_FAST_DECAY
    min_decay = math.log(HY_DECAY_TARGET) / HY_SLOW_DECAY
    deltas = jnp.abs(jnp.linspace(min_decay, max_decay, HY_WIDTH, dtype=f32))
    decay = jnp.exp(-t[:, None] * deltas[None, :])
    h = h * decay[:, None, None, :]
    fwd = h[:, :, 0]
    bwd = h[:, :, 1]
    k = jnp.concatenate([fwd, jnp.zeros((1, HY_ORDER, HY_WIDTH), f32), bwd[1:][::-1]], axis=0)
    k = k * lax.rsqrt(jnp.sum(jnp.square(k), axis=0, keepdims=True) + 1e-6)
    return k


def bidirectional_fftconv(u, k):
    n = k.shape[0]
    length = u.shape[1]
    uf = jnp.fft.rfft(u, n=n, axis=1)
    kf = jnp.fft.rfft(k, n=n, axis=0)
    return jnp.fft.irfft(uf * kf[None], n=n, axis=1)[:, :length]


def parallel_mixer(h, w_in, b_gate, conf_dw_w, conf_dw_b, conf_ln_g, conf_ln_b, conf_w_out,
                   hy_short_w, hy_short_b, hy_kern, hy_skip, hy_w_out, w_mix_out):
    proj = h @ w_in
    glu = proj[..., :COL_GLU]
    hy = proj[..., COL_GLU:COL_GLU + COL_HY]
    gate_logits = proj[..., COL_GLU + COL_HY:]
    a, b = jnp.split(glu, 2, axis=-1)
    u = a * jax.nn.sigmoid(b)
    u = depthwise_conv(u, conf_dw_w, conf_dw_b)
    u = jax.nn.silu(layer_norm(u, conf_ln_g, conf_ln_b))
    y_a = u @ conf_w_out
    hy = depthwise_conv(hy, hy_short_w, hy_short_b)
    x1, x2, v = jnp.split(hy, 3, axis=-1)
    z = v.astype(jnp.float32)
    for n, g in enumerate((x1, x2)):
        z = g.astype(jnp.float32) * (bidirectional_fftconv(z, hy_kern[:, n])
                                     + hy_skip[n].astype(jnp.float32) * z)
    y_b = z.astype(h.dtype) @ hy_w_out
    g_a, g_b = jnp.split(jax.nn.sigmoid(gate_logits + b_gate), 2, axis=-1)
    return (g_a * y_a + g_b * y_b) @ w_mix_out


def memory_cross_attention(x, mem, wq, wk, wv, wo):
    bsz, length, d = x.shape
    m = mem.shape[1]
    q = (x @ wq).reshape(bsz, length, XA_HEADS, XA_HEAD_DIM)
    k = (mem @ wk).reshape(bsz, m, XA_HEADS, XA_HEAD_DIM)
    v = (mem @ wv).reshape(bsz, m, XA_HEADS, XA_HEAD_DIM)
    s = jnp.einsum('blhk,bmhk->bhlm', q, k).astype(jnp.float32) * (XA_HEAD_DIM ** -0.5)
    p = jax.nn.softmax(s, axis=-1).astype(v.dtype)
    o = jnp.einsum('bhlm,bmhk->blhk', p, v).reshape(bsz, length, d)
    return o @ wo


def expert_choice_moe(x, w_router, w_gate, w_up, w_down):
    bsz, length, d = x.shape
    cap = max(1, EC_CAPACITY * length // N_EXPERTS)
    logits = jnp.einsum('btd,de->bte', x, w_router).astype(jnp.float32)
    aff = jax.nn.softmax(logits, axis=-1)
    gate, idx = lax.top_k(jnp.swapaxes(aff, 1, 2), cap)
    xg = jax.vmap(lambda xb, ib: xb[ib])(x, idx)
    hg = jnp.einsum('becd,edf->becf', xg, w_gate)
    hu = jnp.einsum('becd,edf->becf', xg, w_up)
    y = jnp.einsum('becf,efd->becd', jax.nn.silu(hg) * hu, w_down)
    y = y * gate[..., None].astype(y.dtype)
    return jax.vmap(lambda ib, yb: jnp.zeros((length, d), yb.dtype)
                    .at[ib.reshape(-1)].add(yb.reshape(-1, d)))(idx, y)


def setup_inputs(seed: int = 0) -> dict:
    key = jax.random.key(seed)
    keys = iter(jax.random.split(key, 48))
    f32 = jnp.float32

    def nrm(shape, scale):
        return jax.random.normal(next(keys), shape, f32) * scale

    def gain(shape):
        return 1.0 + nrm(shape, 0.02)

    def bias(shape):
        return nrm(shape, 0.02)

    ly = (DEPTH,)
    d = D_MODEL
    return {
        'x': nrm((BATCH, SEQ, d), 1.0),
        'mem': nrm((BATCH, N_MEM, d), 1.0),
        'ln_in_g': gain((d,)),
        'ln_in_b': bias((d,)),
        'w_in': nrm(ly + (d, IN_COLS), d ** -0.5),
        'b_gate': bias(ly + (COL_GATE,)),
        'conf_dw_w': nrm(ly + (CONV_KERNEL, CONV_WIDTH), CONV_KERNEL ** -0.5),
        'conf_dw_b': bias(ly + (CONV_WIDTH,)),
        'conf_ln_g': gain(ly + (CONV_WIDTH,)),
        'conf_ln_b': bias(ly + (CONV_WIDTH,)),
        'conf_w_out': nrm(ly + (CONV_WIDTH, d), CONV_WIDTH ** -0.5),
        'hy_short_w': nrm(ly + (HY_SHORT, COL_HY), HY_SHORT ** -0.5),
        'hy_short_b': bias(ly + (COL_HY,)),
        'hy_ffn_w1': nrm(ly + (HY_EMB, HY_FFN), HY_EMB ** -0.5),
        'hy_ffn_b1': bias(ly + (HY_FFN,)),
        'hy_freq1': gain(ly + (HY_FFN,)),
        'hy_ffn_w2': nrm(ly + (HY_FFN, HY_FFN), HY_FFN ** -0.5),
        'hy_ffn_b2': bias(ly + (HY_FFN,)),
        'hy_freq2': gain(ly + (HY_FFN,)),
        'hy_ffn_w3': nrm(ly + (HY_FFN, HY_ORDER * 2 * HY_WIDTH), HY_FFN ** -0.5),
        'hy_skip': nrm(ly + (HY_ORDER, HY_WIDTH), 0.5),
        'hy_w_out': nrm(ly + (HY_WIDTH, d), HY_WIDTH ** -0.5),
        'w_mix_out': nrm(ly + (d, d), d ** -0.5 * DN_BETA),
        'ln_mix_g': gain(ly + (d,)),
        'ln_mix_b': bias(ly + (d,)),
        'xa_wq': nrm(ly + (d, d), d ** -0.5),
        'xa_wk': nrm(ly + (d, d), d ** -0.5),
        'xa_wv': nrm(ly + (d, d), d ** -0.5 * DN_BETA),
        'xa_wo': nrm(ly + (d, d), d ** -0.5 * DN_BETA),
        'ln_xa_g': gain(ly + (d,)),
        'ln_xa_b': bias(ly + (d,)),
        'moe_w_router': nrm(ly + (d, N_EXPERTS), d ** -0.5),
        'moe_w_gate': nrm(ly + (N_EXPERTS, d, D_FF_EXPERT), d ** -0.5),
        'moe_w_up': nrm(ly + (N_EXPERTS, d, D_FF_EXPERT), d ** -0.5),
        'moe_w_down': nrm(ly + (N_EXPERTS, D_FF_EXPERT, d), D_FF_EXPERT ** -0.5 * DN_BETA),
        'ln_moe_g': gain(ly + (d,)),
        'ln_moe_b': bias(ly + (d,)),
    }


def reference(x, mem, ln_in_g, ln_in_b, w_in, b_gate, conf_dw_w, conf_dw_b, conf_ln_g, conf_ln_b,
              conf_w_out, hy_short_w, hy_short_b, hy_ffn_w1, hy_ffn_b1, hy_freq1, hy_ffn_w2,
              hy_ffn_b2, hy_freq2, hy_ffn_w3, hy_skip, hy_w_out, w_mix_out, ln_mix_g, ln_mix_b,
              xa_wq, xa_wk, xa_wv, xa_wo, ln_xa_g, ln_xa_b, moe_w_router, moe_w_gate, moe_w_up,
              moe_w_down, ln_moe_g, ln_moe_b):
    x = layer_norm(x, ln_in_g, ln_in_b)
    feat, t = hyena_positional_features(x.shape[1])
    for i in range(DEPTH):
        kern = hyena_kernels(feat, t, hy_ffn_w1[i], hy_ffn_b1[i], hy_freq1[i], hy_ffn_w2[i],
                             hy_ffn_b2[i], hy_freq2[i], hy_ffn_w3[i])
        mixed = parallel_mixer(x, w_in[i], b_gate[i], conf_dw_w[i], conf_dw_b[i], conf_ln_g[i],
                               conf_ln_b[i], conf_w_out[i], hy_short_w[i], hy_short_b[i], kern,
                               hy_skip[i], hy_w_out[i], w_mix_out[i])
        x = layer_norm(DN_ALPHA * x + mixed, ln_mix_g[i], ln_mix_b[i])
        xa = memory_cross_attention(x, mem, xa_wq[i], xa_wk[i], xa_wv[i], xa_wo[i])
        x = layer_norm(DN_ALPHA * x + xa, ln_xa_g[i], ln_xa_b[i])
        moe = expert_choice_moe(x, moe_w_router[i], moe_w_gate[i], moe_w_up[i], moe_w_down[i])
        x = layer_norm(DN_ALPHA * x + moe, ln_moe_g[i], ln_moe_b[i])
    return x
```

```python
import functools
import math

import numpy as np
import jax
import jax.numpy as jnp
from jax import lax
from jax.experimental import pallas as pl
from jax.experimental.pallas import tpu as pltpu

F32 = jnp.float32
BF16 = jnp.bfloat16
I32 = jnp.int32

LANES = 128
SUBLANES = 8
VMEM_LIMIT_BYTES = 56 * 1024 * 1024

LN_EPS = 1e-5
XA_HEADS = 4
N_EXPERT_CAPACITY = 2
HY_DECAY_TARGET = 1e-2
HY_FAST_DECAY = 0.3
HY_SLOW_DECAY = 1.5
HY_BANDS = 16


def _cparams(*sem):
    return pltpu.CompilerParams(dimension_semantics=sem, vmem_limit_bytes=VMEM_LIMIT_BYTES)


def _ln(x, g, b):
    mu = jnp.mean(x, axis=-1, keepdims=True)
    xc = x - mu
    var = jnp.mean(xc * xc, axis=-1, keepdims=True)
    return xc * lax.rsqrt(var + LN_EPS) * g + b


def _sigmoid(x):
    return 1.0 / (1.0 + jnp.exp(-x))


def _dot(a, b):
    return jnp.dot(a, b, preferred_element_type=F32)


def _dot_nt(a, b):
    return lax.dot_general(a, b, (((1,), (1,)), ((), ())), preferred_element_type=F32)


def _split_bf16(x):
    hi = x.astype(BF16)
    lo = (x - hi.astype(F32)).astype(BF16)
    return hi, lo


def _dot3(a, b_hi, b_lo):
    a_hi, a_lo = _split_bf16(a)
    return _dot(a_hi, b_hi) + (_dot(a_hi, b_lo) + _dot(a_lo, b_hi))


def _dot3_lhs(a_hi, a_lo, b):
    b_hi, b_lo = _split_bf16(b)
    return _dot(a_hi, b_hi) + (_dot(a_hi, b_lo) + _dot(a_lo, b_hi))


def _inproj_kernel(x_ref, g_ref, b_ref, wglu_ref, whyt_ref, wgate_ref, bgate_ref,
                   h_ref, u_ref, hyt_ref, gate_ref, *, apply_ln):
    h = _ln(x_ref[...], g_ref[...], b_ref[...]) if apply_ln else x_ref[...]
    h_ref[...] = h
    hb = h.astype(BF16)
    glu = _dot(hb, wglu_ref[...])
    cw = glu.shape[1] // 2
    u_ref[...] = glu[:, :cw] * _sigmoid(glu[:, cw:])
    hyt_ref[...] = _dot_nt(whyt_ref[...], hb)
    gl = _dot(hb, wgate_ref[...]) + bgate_ref[...]
    gate_ref[...] = _sigmoid(gl).astype(BF16)


def _inproj(x, g, b, wglu, whyt, wgate, bgate, *, tm, apply_ln):
    bsz, length, d = x.shape
    n_glu, n_hy, n_gate = wglu.shape[1], whyt.shape[0], wgate.shape[1]
    const = lambda i, j: (0, 0)
    return pl.pallas_call(
        functools.partial(_inproj_kernel, apply_ln=apply_ln),
        grid=(bsz, length // tm),
        in_specs=[
            pl.BlockSpec((None, tm, d), lambda i, j: (i, j, 0)),
            pl.BlockSpec((1, d), const), pl.BlockSpec((1, d), const),
            pl.BlockSpec((d, n_glu), const),
            pl.BlockSpec((n_hy, d), const),
            pl.BlockSpec((d, n_gate), const),
            pl.BlockSpec((1, n_gate), const),
        ],
        out_specs=[
            pl.BlockSpec((None, tm, d), lambda i, j: (i, j, 0)),
            pl.BlockSpec((None, tm, n_glu // 2), lambda i, j: (i, j, 0)),
            pl.BlockSpec((None, n_hy, tm), lambda i, j: (i, 0, j)),
            pl.BlockSpec((None, tm, n_gate), lambda i, j: (i, j, 0)),
        ],
        out_shape=[
            jax.ShapeDtypeStruct((bsz, length, d), F32),
            jax.ShapeDtypeStruct((bsz, length, n_glu // 2), F32),
            jax.ShapeDtypeStruct((bsz, n_hy, length), F32),
            jax.ShapeDtypeStruct((bsz, length, n_gate), BF16),
        ],
        compiler_params=_cparams("parallel", "parallel"),
        name="inproj",
    )(x, g, b, wglu, whyt, wgate, bgate)


def _conf_kernel(prev_ref, cur_ref, next_ref, w_ref, cb_ref, g_ref, b_ref, o_ref, ext_ref,
                 *, tl, halo, ksize, rows):
    j = pl.program_id(1)
    nj = pl.num_programs(1)
    pad = (ksize - 1) // 2
    ext_ref[0:halo, :] = jnp.where(j > 0, prev_ref[...], 0.0)
    ext_ref[halo:halo + tl, :] = cur_ref[...]
    ext_ref[halo + tl:halo + tl + halo, :] = jnp.where(j < nj - 1, next_ref[...], 0.0)
    w = w_ref[...]
    cb, g, b = cb_ref[...], g_ref[...], b_ref[...]
    for r0 in range(0, tl, rows):
        acc = jnp.zeros((rows, w.shape[1]), F32)
        for k in range(ksize):
            s = halo - pad + k + r0
            acc = acc + w[k:k + 1, :] * ext_ref[s:s + rows, :]
        y = _ln(acc + cb, g, b)
        o_ref[r0:r0 + rows, :] = (y * _sigmoid(y)).astype(o_ref.dtype)


def _conf(u, w, cb, g, b, *, tl, halo=16, rows=64):
    bsz, length, c = u.shape
    ksize = w.shape[0]
    assert (ksize - 1) // 2 <= halo and tl % halo == 0 and tl % rows == 0
    nh = tl // halo
    last = length // halo - 1
    const = lambda i, j: (0, 0)
    return pl.pallas_call(
        functools.partial(_conf_kernel, tl=tl, halo=halo, ksize=ksize, rows=rows),
        grid=(bsz, length // tl),
        in_specs=[
            pl.BlockSpec((None, halo, c), lambda i, j: (i, jnp.maximum(j * nh - 1, 0), 0)),
            pl.BlockSpec((None, tl, c), lambda i, j: (i, j, 0)),
            pl.BlockSpec((None, halo, c), lambda i, j: (i, jnp.minimum((j + 1) * nh, last), 0)),
            pl.BlockSpec((ksize, c), const),
            pl.BlockSpec((1, c), const), pl.BlockSpec((1, c), const), pl.BlockSpec((1, c), const),
        ],
        out_specs=pl.BlockSpec((None, tl, c), lambda i, j: (i, j, 0)),
        out_shape=jax.ShapeDtypeStruct((bsz, length, c), BF16),
        scratch_shapes=[pltpu.VMEM((tl + 2 * halo, c), F32)],
        compiler_params=_cparams("parallel", "parallel"),
        name="conf",
    )(u, u, u, w, cb, g, b)


def _filt_ffn_kernel(feat_ref, w1h_ref, w1l_ref, b1_ref, f1_ref, w2h_ref, w2l_ref, b2_ref, f2_ref, o_ref):
    a = _dot3_lhs(w1h_ref[...], w1l_ref[...], feat_ref[...]) + b1_ref[...]
    h = jnp.sin(f1_ref[...] * a)
    a2 = _dot3_lhs(w2h_ref[...], w2l_ref[...], h) + b2_ref[...]
    o_ref[...] = jnp.sin(f2_ref[...] * a2)


def _filt_ffn(feat_t, w1h, w1l, b1, f1, w2h, w2l, b2, f2, *, tn):
    kp, n = feat_t.shape
    m = w1h.shape[0]
    const = lambda i: (0, 0)
    return pl.pallas_call(
        _filt_ffn_kernel,
        grid=(n // tn,),
        in_specs=[pl.BlockSpec((kp, tn), lambda i: (0, i)),
                  pl.BlockSpec((m, kp), const), pl.BlockSpec((m, kp), const),
                  pl.BlockSpec((m, 1), const), pl.BlockSpec((m, 1), const),
                  pl.BlockSpec((m, m), const), pl.BlockSpec((m, m), const),
                  pl.BlockSpec((m, 1), const), pl.BlockSpec((m, 1), const)],
        out_specs=pl.BlockSpec((m, tn), lambda i: (0, i)),
        out_shape=jax.ShapeDtypeStruct((m, n), F32),
        compiler_params=_cparams("parallel"),
        name="filt_ffn",
    )(feat_t, w1h, w1l, b1, f1, w2h, w2l, b2, f2)


def _filt_kernel(h2_ref, w3fh_ref, w3fl_ref, w3bh_ref, w3bl_ref, delta_ref, text_ref, valid_ref, o_ref):
    n = h2_ref.shape[1]
    half = n // 2
    decay = jnp.exp(-text_ref[...] * delta_ref[...]) * valid_ref[...]
    kf = _dot3_lhs(w3fh_ref[...], w3fl_ref[...], h2_ref[:, 0:half]) * decay[:, 0:half]
    kb = _dot3_lhs(w3bh_ref[...], w3bl_ref[...], h2_ref[:, half:n]) * decay[:, half:n]
    ss = jnp.sum(kf * kf, axis=1, keepdims=True) + jnp.sum(kb * kb, axis=1, keepdims=True)
    scale = lax.rsqrt(ss + 1e-6)
    o_ref[:, 0:half] = kf * scale
    o_ref[:, half:n] = kb * scale


def _filt(h2t, w3fh, w3fl, w3bh, w3bl, delta, text, valid, *, g):
    m, n = h2t.shape
    rows = w3fh.shape[0]
    const = lambda i: (0, 0)
    return pl.pallas_call(
        _filt_kernel,
        grid=(rows // g,),
        in_specs=[pl.BlockSpec((m, n), const),
                  pl.BlockSpec((g, m), lambda i: (i, 0)), pl.BlockSpec((g, m), lambda i: (i, 0)),
                  pl.BlockSpec((g, m), lambda i: (i, 0)), pl.BlockSpec((g, m), lambda i: (i, 0)),
                  pl.BlockSpec((g, 1), lambda i: (i, 0)),
                  pl.BlockSpec((1, n), const), pl.BlockSpec((1, n), const)],
        out_specs=pl.BlockSpec((g, n), lambda i: (i, 0)),
        out_shape=jax.ShapeDtypeStruct((rows, n), F32),
        compiler_params=_cparams("parallel"),
        name="filt",
    )(h2t, w3fh, w3fl, w3bh, w3bl, delta, text, valid)


def _dft_tables(r1, n1_used):
    n = r1 * LANES
    a1 = 2.0 * np.pi * np.outer(np.arange(n1_used), np.arange(r1)) / r1
    a2 = 2.0 * np.pi * np.outer(np.arange(LANES), np.arange(LANES)) / LANES
    at = 2.0 * np.pi * np.outer(np.arange(LANES), np.arange(r1)) / n
    return n, a1, a2, at


def _cplx_rhs(ang, sign):
    c, s = np.cos(ang), sign * np.sin(ang)
    return np.block([[c, s], [-s, c]])


def _fwd_steps(x_t, w1, twr, twi, w2):
    g, n2, k1w = x_t.shape
    r1 = twr.shape[1]
    a = _dot(x_t.reshape(g * n2, k1w).astype(BF16), w1).reshape(g, n2, 2 * r1)
    ar, ai = a[:, :, :r1], a[:, :, r1:]
    br = ar * twr - ai * twi
    bi = ar * twi + ai * twr
    bt = jnp.concatenate([jnp.swapaxes(br, 1, 2), jnp.swapaxes(bi, 1, 2)], axis=-1)
    c = _dot(bt.reshape(g * r1, 2 * n2).astype(BF16), w2)
    return c.reshape(g, r1, 2 * n2)


def _inv_steps(d, w3, twr_t, twi_t, w4):
    g, r1, w = d.shape
    n2 = w // 2
    e = _dot(d.reshape(g * r1, w).astype(BF16), w3).reshape(g, r1, w)
    er, ei = e[:, :, :n2], e[:, :, n2:]
    fr = er * twr_t + ei * twi_t
    fi = ei * twr_t - er * twi_t
    ft = jnp.concatenate([jnp.swapaxes(fr, 1, 2), jnp.swapaxes(fi, 1, 2)], axis=-1)
    y = _dot(ft.reshape(g * n2, 2 * r1).astype(BF16), w4)
    return y.reshape(g, n2, w4.shape[1])


def _kfft_kernel(k_ref, w1_ref, twr_ref, twi_ref, w2_ref, o_ref):
    x_t = jnp.swapaxes(k_ref[...], 1, 2)
    o_ref[...] = _fwd_steps(x_t, w1_ref[...], twr_ref[...], twi_ref[...], w2_ref[...])


def _kfft(k3, w1, twr, twi, w2, *, g):
    rows, r1, lanes = k3.shape
    const = lambda i: (0, 0)
    return pl.pallas_call(
        _kfft_kernel,
        grid=(rows // g,),
        in_specs=[pl.BlockSpec((g, r1, lanes), lambda i: (i, 0, 0)),
                  pl.BlockSpec(w1.shape, const), pl.BlockSpec(twr.shape, const),
                  pl.BlockSpec(twi.shape, const), pl.BlockSpec(w2.shape, const)],
        out_specs=pl.BlockSpec((g, r1, 2 * lanes), lambda i: (i, 0, 0)),
        out_shape=jax.ShapeDtypeStruct((rows, r1, 2 * lanes), F32),
        compiler_params=_cparams("parallel"),
        name="kfft",
    )(k3, w1, twr, twi, w2)


def _short_conv(a, w, b):
    p, g, r, l = a.shape
    a3 = a.reshape(p * g, r, l)
    lane = lax.broadcasted_iota(I32, a3.shape, 2)
    row = lax.broadcasted_iota(I32, a3.shape, 1)
    pl_ = pltpu.roll(a3, 1, 2)
    ql = pltpu.roll(pl_, 1, 1)
    prev = jnp.where(lane == 0, jnp.where(row == 0, 0.0, ql), pl_)
    pr = pltpu.roll(a3, l - 1, 2)
    qr = pltpu.roll(pr, r - 1, 1)
    nxt = jnp.where(lane == l - 1, jnp.where(row == r - 1, 0.0, qr), pr)
    prev, nxt = prev.reshape(a.shape), nxt.reshape(a.shape)
    return w[0][None] * prev + w[1][None] * a + w[2][None] * nxt + b[None]


def _hyena_kernel(x1_ref, x2_ref, v_ref, sw1_ref, sb1_ref, sw2_ref, sb2_ref, swv_ref, sbv_ref,
                  kf0_ref, kf1_ref, skip_ref, w1_ref, twr_ref, twi_ref, w2_ref, w3_ref,
                  twrt_ref, twit_ref, w4_ref, z_ref):
    x1 = _short_conv(x1_ref[...], sw1_ref[...], sb1_ref[...])
    x2 = _short_conv(x2_ref[...], sw2_ref[...], sb2_ref[...])
    z = _short_conv(v_ref[...], swv_ref[...], sbv_ref[...])
    rh = z.shape[2]
    for gate, kf_ref, o in ((x1, kf0_ref, 0), (x2, kf1_ref, 1)):
        s = jnp.concatenate([z[0], z[1]], axis=1)
        c = _fwd_steps(jnp.swapaxes(s, 1, 2), w1_ref[...], twr_ref[...], twi_ref[...], w2_ref[...])
        kf = kf_ref[...]
        n2 = c.shape[2] // 2
        cr, ci, kr, ki = c[:, :, :n2], c[:, :, n2:], kf[:, :, :n2], kf[:, :, n2:]
        d = jnp.concatenate([cr * kr - ci * ki, cr * ki + ci * kr], axis=-1)
        y = _inv_steps(d, w3_ref[...], twrt_ref[...], twit_ref[...], w4_ref[...])
        y = jnp.swapaxes(y, 1, 2)
        conv = jnp.stack([y[:, :rh], y[:, rh:]], axis=0)
        z = gate * (conv + skip_ref[o][None] * z)
    z_ref[...] = z


def _hyena(hy4, sw, sb, kf, skip, tabs, *, g):
    bsz, c3, rh, lanes = hy4.shape
    c = c3 // 3
    r1 = 2 * rh
    ncb = c // g
    w1, twr, twi, w2, w3, twrt, twit, w4 = tabs
    const2 = lambda i, p: (0, 0)
    data = lambda off: pl.BlockSpec((2, g, rh, lanes), lambda i, p, off=off: (p, off + i, 0, 0))
    wspec = lambda off: pl.BlockSpec((3, g, 1, lanes), lambda i, p, off=off: (0, off + i, 0, 0))
    bspec = lambda off: pl.BlockSpec((g, 1, lanes), lambda i, p, off=off: (off + i, 0, 0))
    kspec = lambda off: pl.BlockSpec((g, r1, 2 * lanes), lambda i, p, off=off: (off + i, 0, 0))
    return pl.pallas_call(
        _hyena_kernel,
        grid=(ncb, bsz // 2),
        in_specs=[data(0), data(ncb), data(2 * ncb),
                  wspec(0), bspec(0), wspec(ncb), bspec(ncb), wspec(2 * ncb), bspec(2 * ncb),
                  kspec(0), kspec(ncb),
                  pl.BlockSpec((2, g, 1, lanes), lambda i, p: (0, i, 0, 0)),
                  pl.BlockSpec(w1.shape, const2), pl.BlockSpec(twr.shape, const2),
                  pl.BlockSpec(twi.shape, const2), pl.BlockSpec(w2.shape, const2),
                  pl.BlockSpec(w3.shape, const2), pl.BlockSpec(twrt.shape, const2),
                  pl.BlockSpec(twit.shape, const2), pl.BlockSpec(w4.shape, const2)],
        out_specs=pl.BlockSpec((2, g, rh, lanes), lambda i, p: (p, i, 0, 0)),
        out_shape=jax.ShapeDtypeStruct((bsz, c, rh, lanes), F32),
        compiler_params=_cparams("parallel", "parallel"),
        name="hyena",
    )(hy4, hy4, hy4, sw, sb, sw, sb, sw, sb, kf, kf, skip, w1, twr, twi, w2, w3, twrt, twit, w4)


def _mix_kernel(ua_ref, zt_ref, gate_ref, h_ref, wa_ref, wb_ref, wm_ref, g_ref, b_ref, o_ref, *, alpha):
    ya = _dot(ua_ref[...], wa_ref[...])
    zt = jnp.transpose(zt_ref[...]).astype(BF16)
    yb = _dot(zt, wb_ref[...])
    gt = gate_ref[...].astype(F32)
    d = ya.shape[1]
    m = gt[:, :d] * ya + gt[:, d:] * yb
    mixed = _dot(m.astype(BF16), wm_ref[...])
    o_ref[...] = _ln(alpha * h_ref[...] + mixed, g_ref[...], b_ref[...])


def _mix(ua, zt, gates, h, wa, wb, wm, g, b, *, tm, alpha):
    bsz, length, d = h.shape
    c = ua.shape[2]
    const = lambda i, j: (0, 0)
    return pl.pallas_call(
        functools.partial(_mix_kernel, alpha=alpha),
        grid=(bsz, length // tm),
        in_specs=[pl.BlockSpec((None, tm, c), lambda i, j: (i, j, 0)),
                  pl.BlockSpec((None, c, tm), lambda i, j: (i, 0, j)),
                  pl.BlockSpec((None, tm, 2 * d), lambda i, j: (i, j, 0)),
                  pl.BlockSpec((None, tm, d), lambda i, j: (i, j, 0)),
                  pl.BlockSpec((c, d), const), pl.BlockSpec((c, d), const), pl.BlockSpec((d, d), const),
                  pl.BlockSpec((1, d), const), pl.BlockSpec((1, d), const)],
        out_specs=pl.BlockSpec((None, tm, d), lambda i, j: (i, j, 0)),
        out_shape=jax.ShapeDtypeStruct((bsz, length, d), F32),
        compiler_params=_cparams("parallel", "parallel"),
        name="mix",
    )(ua, zt, gates, h, wa, wb, wm, g, b)


def _kv_kernel(mem_ref, wk_ref, wv_ref, k_ref, v_ref):
    mb = mem_ref[...].astype(BF16)
    k_ref[...] = _dot(mb, wk_ref[...]).astype(BF16)
    v_ref[...] = _dot(mb, wv_ref[...]).astype(BF16)


def _kv(mem, wk, wv):
    bsz, m, d = mem.shape
    const = lambda i: (0, 0)
    blk = pl.BlockSpec((None, m, d), lambda i: (i, 0, 0))
    return pl.pallas_call(
        _kv_kernel,
        grid=(bsz,),
        in_specs=[blk, pl.BlockSpec((d, d), const), pl.BlockSpec((d, d), const)],
        out_specs=[blk, blk],
        out_shape=[jax.ShapeDtypeStruct((bsz, m, d), BF16)] * 2,
        compiler_params=_cparams("parallel"),
        name="kv",
    )(mem, wk, wv)


def _xattn_kernel(x_ref, k_ref, v_ref, wq_ref, wo_ref, g_ref, b_ref, wrh_ref, wrl_ref,
                  x2_ref, x2b_ref, aff_ref, *, alpha, heads):
    x = x_ref[...]
    d = x.shape[1]
    dh = d // heads
    q = (_dot(x.astype(BF16), wq_ref[...]) * (dh ** -0.5)).astype(BF16)
    outs = []
    for hd in range(heads):
        sl = slice(hd * dh, (hd + 1) * dh)
        s = _dot_nt(q[:, sl], k_ref[:, sl])
        s = s - jnp.max(s, axis=-1, keepdims=True)
        p = jnp.exp(s)
        p = p / jnp.sum(p, axis=-1, keepdims=True)
        outs.append(_dot(p.astype(BF16), v_ref[:, sl]))
    o = jnp.concatenate(outs, axis=-1)
    xa = _dot(o.astype(BF16), wo_ref[...])
    x2 = _ln(alpha * x + xa, g_ref[...], b_ref[...])
    x2_ref[...] = x2
    x2h, x2l = _split_bf16(x2)
    x2b_ref[...] = x2h
    wrh, wrl = wrh_ref[...], wrl_ref[...]
    logits = _dot_nt(wrh, x2h) + (_dot_nt(wrh, x2l) + _dot_nt(wrl, x2h))
    logits = logits - jnp.max(logits, axis=0, keepdims=True)
    ex = jnp.exp(logits)
    aff_ref[...] = ex / jnp.sum(ex, axis=0, keepdims=True)


def _xattn(x, k, v, wq, wo, g, b, wrh, wrl, *, tm, alpha, heads):
    bsz, length, d = x.shape
    m = k.shape[1]
    e = wrh.shape[0]
    const = lambda i, j: (0, 0)
    tok = pl.BlockSpec((None, tm, d), lambda i, j: (i, j, 0))
    mem = pl.BlockSpec((None, m, d), lambda i, j: (i, 0, 0))
    return pl.pallas_call(
        functools.partial(_xattn_kernel, alpha=alpha, heads=heads),
        grid=(bsz, length // tm),
        in_specs=[tok, mem, mem, pl.BlockSpec((d, d), const), pl.BlockSpec((d, d), const),
                  pl.BlockSpec((1, d), const), pl.BlockSpec((1, d), const),
                  pl.BlockSpec((e, d), const), pl.BlockSpec((e, d), const)],
        out_specs=[tok, tok, pl.BlockSpec((None, e, tm), lambda i, j: (i, 0, j))],
        out_shape=[jax.ShapeDtypeStruct((bsz, length, d), F32),
                   jax.ShapeDtypeStruct((bsz, length, d), BF16),
                   jax.ShapeDtypeStruct((bsz, e, length), F32)],
        compiler_params=_cparams("parallel", "parallel"),
        name="xattn",
    )(x, k, v, wq, wo, g, b, wrh, wrl)


def _excl_cumsum_lanes(mask_f32, tri, write):
    e, t = mask_f32.shape
    carry = jnp.zeros((e, 1), F32)
    for c in range(t // LANES):
        m = mask_f32[:, c * LANES:(c + 1) * LANES]
        inc = _dot(m.astype(BF16), tri)
        write(c, inc - m + carry)
        carry = carry + inc[:, LANES - 1:LANES]


def _select_kernel(aff_ref, tri_ref, slot_ref, gate_ref, rank_ref, cnt_ref, *, cap):
    a = aff_ref[...]
    e = a.shape[0]
    v = jnp.zeros((e, 1), I32)
    for bit in range(30, -1, -1):
        cand = v | (1 << bit)
        cnt = jnp.sum(jnp.where(a >= pltpu.bitcast(cand, F32), 1.0, 0.0), axis=1, keepdims=True)
        v = jnp.where(cnt >= cap, cand, v)
    thr = pltpu.bitcast(v, F32)
    gt = a > thr
    eq = a == thr
    need = cap - jnp.sum(jnp.where(gt, 1.0, 0.0), axis=1, keepdims=True)
    tri = tri_ref[...]

    def write_eq(c, val):
        cnt_ref[:, c * LANES:(c + 1) * LANES] = val
    _excl_cumsum_lanes(jnp.where(eq, 1.0, 0.0), tri, write_eq)
    sel = gt | (eq & (cnt_ref[...] < need))
    gate_ref[...] = jnp.where(sel, a, 0.0)

    def write_rank(c, val):
        cnt_ref[:, c * LANES:(c + 1) * LANES] = val
    _excl_cumsum_lanes(jnp.where(sel, 1.0, 0.0), tri, write_rank)
    rank = cnt_ref[...].astype(I32)
    rank_ref[...] = rank
    slot_ref[...] = jnp.where(sel, rank, -1)


def _select(aff, tri, *, cap):
    bsz, e, t = aff.shape
    blk = pl.BlockSpec((None, e, t), lambda i: (i, 0, 0))
    return pl.pallas_call(
        functools.partial(_select_kernel, cap=cap),
        grid=(bsz,),
        in_specs=[blk, pl.BlockSpec((LANES, LANES), lambda i: (0, 0))],
        out_specs=[blk, blk, blk],
        out_shape=[jax.ShapeDtypeStruct((bsz, e, t), I32), jax.ShapeDtypeStruct((bsz, e, t), F32),
                   jax.ShapeDtypeStruct((bsz, e, t), I32)],
        scratch_shapes=[pltpu.VMEM((e, t), F32)],
        compiler_params=_cparams("parallel"),
        name="select",
    )(aff, tri)


def _window_start(r0, cap, win, align):
    return pl.multiple_of(jnp.minimum((r0 // align) * align, cap - win), align)


def _dispatch_kernel(jlo_ref, nch_ref, x_ref, slot_ref, gh_ref, gl_ref, o_ref, gs_ref, acc_ref, gacc_ref,
                     *, st, kc, tt):
    b, e = pl.program_id(0), pl.program_id(1)
    t_total = x_ref.shape[0]
    n_e = gh_ref.shape[1]
    for s in range(o_ref.shape[0] // st):
        j0 = jlo_ref[b, e, s]

        def chunk(c, s=s, j0=j0):
            want = (j0 + c * (kc // tt)) * tt
            t0 = pl.multiple_of(jnp.minimum(want, t_total - kc), tt)
            tok = t0 + lax.broadcasted_iota(I32, (1, kc), 1)
            rel = jnp.where(tok >= want, slot_ref[:, pl.ds(t0, kc)] - s * st, -1)
            onehot = jnp.where(lax.broadcasted_iota(I32, (st, kc), 0) == rel, 1.0, 0.0).astype(BF16)
            xs = _dot(onehot, x_ref[pl.ds(t0, kc), :])
            gs = _dot(onehot, gh_ref[pl.ds(t0, kc), :]) + _dot(onehot, gl_ref[pl.ds(t0, kc), :])
            return xs, gs

        xs, gs = chunk(0)
        acc_ref[...] = xs
        gacc_ref[...] = gs

        def body(c, carry):
            xs, gs = chunk(c)
            acc_ref[...] += xs
            gacc_ref[...] += gs
            return carry
        lax.fori_loop(1, nch_ref[b, e, s], body, 0)
        o_ref[s * st:(s + 1) * st, :] = acc_ref[...].astype(o_ref.dtype)
        mine = lax.broadcasted_iota(I32, (st, n_e), 1) == e
        gs_ref[s * st:(s + 1) * st, :] = jnp.sum(jnp.where(mine, gacc_ref[...], 0.0), axis=1, keepdims=True)


def _dispatch(jlo, nch, xb, slot4, gate_hi, gate_lo, *, cap, st, kc, tt):
    bsz, t, d = xb.shape
    e = slot4.shape[1]
    assert cap % st == 0 and kc % tt == 0 and t % tt == 0 and kc <= t
    tokmaj = pl.BlockSpec((None, t, e), lambda i, j, a, c: (i, 0, 0))
    return pl.pallas_call(
        functools.partial(_dispatch_kernel, st=st, kc=kc, tt=tt),
        grid_spec=pltpu.PrefetchScalarGridSpec(
            num_scalar_prefetch=2,
            grid=(bsz, e),
            in_specs=[pl.BlockSpec((None, t, d), lambda i, j, a, c: (i, 0, 0)),
                      pl.BlockSpec((None, None, 1, t), lambda i, j, a, c: (i, j, 0, 0)),
                      tokmaj, tokmaj],
            out_specs=[pl.BlockSpec((None, None, cap, d), lambda i, j, a, c: (i, j, 0, 0)),
                       pl.BlockSpec((None, None, cap, 1), lambda i, j, a, c: (i, j, 0, 0))],
            scratch_shapes=[pltpu.VMEM((st, d), F32), pltpu.VMEM((st, e), F32)]),
        out_shape=[jax.ShapeDtypeStruct((bsz, e, cap, d), BF16),
                   jax.ShapeDtypeStruct((bsz, e, cap, 1), F32)],
        compiler_params=_cparams("parallel", "parallel"),
        name="dispatch",
    )(jlo, nch, xb, slot4, gate_hi, gate_lo)


def _expert_kernel(x_ref, gs_ref, wg_ref, wu_ref, wd_ref, o_ref, acc_ref, *, fc):
    x = x_ref[...]
    f = wg_ref.shape[1]
    for i, f0 in enumerate(range(0, f, fc)):
        hg = _dot(x, wg_ref[:, f0:f0 + fc])
        hu = _dot(x, wu_ref[:, f0:f0 + fc])
        act = (hg * _sigmoid(hg) * hu).astype(BF16)
        part = _dot(act, wd_ref[f0:f0 + fc, :])
        if i == 0:
            acc_ref[...] = part
        else:
            acc_ref[...] += part
    o_ref[...] = (acc_ref[...] * gs_ref[...]).astype(o_ref.dtype)


def _expert(xg, gs, wg, wu, wd, *, fc):
    bsz, e, cap, d = xg.shape
    f = wg.shape[2]
    tok = pl.BlockSpec((None, None, cap, d), lambda i, j: (j, i, 0, 0))
    return pl.pallas_call(
        functools.partial(_expert_kernel, fc=fc),
        grid=(e, bsz),
        in_specs=[tok,
                  pl.BlockSpec((None, None, cap, 1), lambda i, j: (j, i, 0, 0)),
                  pl.BlockSpec((None, d, f), lambda i, j: (i, 0, 0)),
                  pl.BlockSpec((None, d, f), lambda i, j: (i, 0, 0)),
                  pl.BlockSpec((None, f, d), lambda i, j: (i, 0, 0))],
        out_specs=tok,
        out_shape=jax.ShapeDtypeStruct((bsz, e, cap, d), BF16),
        scratch_shapes=[pltpu.VMEM((cap, d), F32)],
        compiler_params=_cparams("parallel", "parallel"),
        name="expert",
    )(xg, gs, wg, wu, wd)


def _combine_kernel(r0_ref, x_ref, slot_ref, g_ref, b_ref, y_ref, o_ref, *, tt, win, align, alpha):
    b, j = pl.program_id(0), pl.program_id(1)
    n_e, cap = y_ref.shape[0], y_ref.shape[1]
    acc = alpha * x_ref[...]
    slots = slot_ref[...]
    for e in range(n_e):
        start = _window_start(r0_ref[b, e, j], cap, win, align)
        rel = slots[:, e:e + 1] - start
        onehot = jnp.where(lax.broadcasted_iota(I32, (tt, win), 1) == rel, 1.0, 0.0).astype(BF16)
        acc = acc + _dot(onehot, y_ref[e, pl.ds(start, win), :])
    o_ref[...] = _ln(acc, g_ref[...], b_ref[...])


def _combine(r0, x, slot_t, g, b, y, *, tt, win, align, alpha):
    bsz, t, d = x.shape
    e, cap = y.shape[1], y.shape[2]
    assert win >= tt + align and win <= cap and (cap - win) % align == 0
    const = lambda i, j, r: (0, 0)
    return pl.pallas_call(
        functools.partial(_combine_kernel, tt=tt, win=win, align=align, alpha=alpha),
        grid_spec=pltpu.PrefetchScalarGridSpec(
            num_scalar_prefetch=1,
            grid=(bsz, t // tt),
            in_specs=[pl.BlockSpec((None, tt, d), lambda i, j, r: (i, j, 0)),
                      pl.BlockSpec((None, tt, e), lambda i, j, r: (i, j, 0)),
                      pl.BlockSpec((1, d), const), pl.BlockSpec((1, d), const),
                      pl.BlockSpec((None, e, cap, d), lambda i, j, r: (i, 0, 0, 0),
                                   pipeline_mode=pl.Buffered(1))],
            out_specs=pl.BlockSpec((None, tt, d), lambda i, j, r: (i, j, 0))),
        out_shape=jax.ShapeDtypeStruct((bsz, t, d), F32),
        compiler_params=_cparams("parallel", "parallel"),
        name="combine",
    )(r0, x, slot_t, g, b, y)


def _hyena_constants(length):
    n = 2 * length
    m = np.arange(n)
    pos = np.where(m < length, m, n - m).astype(np.float64)
    pos[length] = 0.0
    t = pos / max(length - 1, 1)
    f = np.linspace(1e-4, HY_BANDS - 1, HY_BANDS)
    ang = (2.0 * np.pi * pos / length)[:, None] * f[None, :]
    feat = np.concatenate([t[:, None], np.cos(ang), -np.sin(ang)], axis=-1)
    feat_t = np.zeros((LANES, n), np.float32)
    feat_t[:feat.shape[1]] = feat.T
    valid = np.ones((1, n), np.float32)
    valid[0, length] = 0.0
    return feat_t, t[None, :].astype(np.float32), valid


def _decay_rates(width):
    max_decay = math.log(HY_DECAY_TARGET) / HY_FAST_DECAY
    min_decay = math.log(HY_DECAY_TARGET) / HY_SLOW_DECAY
    return np.abs(np.linspace(min_decay, max_decay, width, dtype=np.float32)).astype(np.float32)


def _dft_constants(r1):
    n, a1h, a2, at = _dft_tables(r1, r1 // 2)
    _, a1f, _, _ = _dft_tables(r1, r1)
    bf = lambda x: jnp.asarray(x, F32).astype(BF16)
    w1_data = bf(_cplx_rhs(a1h, -1.0))
    w1_filt = bf(np.concatenate([np.cos(a1f), -np.sin(a1f)], axis=1))
    w2 = bf(_cplx_rhs(a2, -1.0))
    w3 = bf(_cplx_rhs(a2.T, 1.0))
    w4 = bf(_cplx_rhs(a1h.T, 1.0) / n)
    twr, twi = np.cos(at), -np.sin(at)
    f32 = lambda x: jnp.asarray(x, F32)
    return dict(w1_data=w1_data, w1_filt=w1_filt, w2=w2, w3=w3, w4=w4,
                twr=f32(twr), twi=f32(twi), twr_t=f32(twr.T), twi_t=f32(twi.T))


def kernel(x, mem, ln_in_g, ln_in_b, w_in, b_gate, conf_dw_w, conf_dw_b, conf_ln_g, conf_ln_b, conf_w_out, hy_short_w, hy_short_b, hy_ffn_w1, hy_ffn_b1, hy_freq1, hy_ffn_w2, hy_ffn_b2, hy_freq2, hy_ffn_w3, hy_skip, hy_w_out, w_mix_out, ln_mix_g, ln_mix_b, xa_wq, xa_wk, xa_wv, xa_wo, ln_xa_g, ln_xa_b, moe_w_router, moe_w_gate, moe_w_up, moe_w_down, ln_moe_g, ln_moe_b):
    bsz, length, d = x.shape
    depth = w_in.shape[0]
    alpha = (2.0 * depth) ** 0.25
    cw = conf_dw_w.shape[2]
    hw = hy_skip.shape[2]
    n_glu, n_hy = 2 * cw, 3 * hw
    n_exp = moe_w_router.shape[2]
    cap = max(1, N_EXPERT_CAPACITY * length // n_exp)
    rh = length // LANES
    r1 = 2 * rh
    assert bsz % 2 == 0 and length % LANES == 0

    row = lambda v: v.reshape(1, -1).astype(F32)
    col = lambda v: v.reshape(-1, 1).astype(F32)

    feat_t, t_ext, valid = _hyena_constants(length)
    dft = _dft_constants(r1)
    rates = _decay_rates(hw)
    tri = jnp.asarray(np.triu(np.ones((LANES, LANES), np.float32))).astype(BF16)

    n_ord = hy_skip.shape[1]
    tm = min(256, length)
    lane_b = lambda v: jnp.broadcast_to(v[..., None, None], v.shape + (1, LANES)).astype(F32)
    dft_data = (dft["w1_data"], dft["twr"], dft["twi"], dft["w2"], dft["w3"], dft["twr_t"], dft["twi_t"],
                dft["w4"])

    h = x
    for i in range(depth):
        w1h, w1l = _split_bf16(jnp.pad(hy_ffn_w1[i].T, ((0, 0), (0, LANES - hy_ffn_w1.shape[1]))))
        w2h, w2l = _split_bf16(hy_ffn_w2[i].T)
        h2t = _filt_ffn(jnp.asarray(feat_t), w1h, w1l, col(hy_ffn_b1[i]), col(hy_freq1[i]),
                        w2h, w2l, col(hy_ffn_b2[i]), col(hy_freq2[i]), tn=min(2048, 2 * length))
        w3 = hy_ffn_w3[i].reshape(-1, n_ord, 2, hw)
        w3f = jnp.transpose(w3[:, :, 0], (1, 2, 0)).reshape(n_ord * hw, -1)
        w3b = jnp.transpose(w3[:, :, 1], (1, 2, 0)).reshape(n_ord * hw, -1)
        w3fh, w3fl = _split_bf16(w3f)
        w3bh, w3bl = _split_bf16(w3b)
        delta = jnp.asarray(np.tile(rates, n_ord).reshape(-1, 1))
        kern = _filt(h2t, w3fh, w3fl, w3bh, w3bl, delta, jnp.asarray(t_ext), jnp.asarray(valid),
                     g=min(128, hw))
        kf = _kfft(kern.reshape(n_ord * hw, r1, LANES), dft["w1_filt"], dft["twr"], dft["twi"], dft["w2"],
                   g=min(16, hw))

        wi = w_in[i]
        wglu = wi[:, :n_glu].astype(BF16)
        whyt = wi[:, n_glu:n_glu + n_hy].T.astype(BF16)
        wgate = wi[:, n_glu + n_hy:].astype(BF16)
        hn, u, hyt, gates = _inproj(h, row(ln_in_g), row(ln_in_b), wglu, whyt, wgate, row(b_gate[i]),
                                    tm=tm, apply_ln=(i == 0))

        ua = _conf(u, conf_dw_w[i], row(conf_dw_b[i]), row(conf_ln_g[i]), row(conf_ln_b[i]),
                   tl=min(512, length))

        z = _hyena(hyt.reshape(bsz, n_hy, rh, LANES), lane_b(hy_short_w[i]), lane_b(hy_short_b[i]), kf,
                   lane_b(hy_skip[i]), dft_data, g=min(16, hw))

        x1 = _mix(ua, z.reshape(bsz, hw, length), gates, hn, conf_w_out[i].astype(BF16),
                  hy_w_out[i].astype(BF16), w_mix_out[i].astype(BF16), row(ln_mix_g[i]), row(ln_mix_b[i]),
                  tm=tm, alpha=alpha)

        k, v = _kv(mem, xa_wk[i].astype(BF16), xa_wv[i].astype(BF16))
        wrh, wrl = _split_bf16(moe_w_router[i].T)
        x2, x2b, aff = _xattn(x1, k, v, xa_wq[i].astype(BF16), xa_wo[i].astype(BF16), row(ln_xa_g[i]),
                              row(ln_xa_b[i]), wrh, wrl, tm=tm, alpha=alpha, heads=XA_HEADS)

        slot, gate, rank = _select(aff, tri, cap=cap)
        tt, st, align = LANES, LANES, 16
        kc = min(16 * tt, length)
        r0 = rank[:, :, ::tt]
        cum = jnp.concatenate([r0, jnp.full((bsz, n_exp, 1), cap, I32)], axis=-1)
        s_lo = (jnp.arange(cap // st, dtype=I32) * st)[None, None, :, None]
        jlo = jnp.sum((cum[:, :, None, 1:] <= s_lo).astype(I32), axis=-1)
        jhi = jnp.sum((cum[:, :, None, :-1] < s_lo + st).astype(I32), axis=-1) - 1
        nch = (jhi - jlo + kc // tt) // (kc // tt)
        gate_hi, gate_lo = _split_bf16(jnp.swapaxes(gate, 1, 2))
        xg, gslot = _dispatch(jlo, nch, x2b, slot.reshape(bsz, n_exp, 1, length), gate_hi, gate_lo,
                              cap=cap, st=st, kc=kc, tt=tt)
        y = _expert(xg, gslot, moe_w_gate[i].astype(BF16), moe_w_up[i].astype(BF16),
                    moe_w_down[i].astype(BF16), fc=min(512, moe_w_gate.shape[3]))
        h = _combine(r0, x2, jnp.swapaxes(slot, 1, 2), row(ln_moe_g[i]), row(ln_moe_b[i]), y,
                     tt=tt, win=2 * tt, align=align, alpha=alpha)
    return h
```

```python
import functools
import math

import numpy as np
import jax
import jax.numpy as jnp
from jax import lax
from jax.experimental import pallas as pl
from jax.experimental.pallas import tpu as pltpu

F32 = jnp.float32
BF16 = jnp.bfloat16
I32 = jnp.int32

LANES = 128
SUBLANES = 8
VMEM_LIMIT_BYTES = 56 * 1024 * 1024
SUB_ROWS = 256

LN_EPS = 1e-5
XA_HEADS = 4
N_EXPERT_CAPACITY = 2
HY_DECAY_TARGET = 1e-2
HY_FAST_DECAY = 0.3
HY_SLOW_DECAY = 1.5
HY_BANDS = 16


def _cparams(*sem):
    return pltpu.CompilerParams(dimension_semantics=sem, vmem_limit_bytes=VMEM_LIMIT_BYTES)


def _ln(x, g, b):
    mu = jnp.mean(x, axis=-1, keepdims=True)
    xc = x - mu
    var = jnp.mean(xc * xc, axis=-1, keepdims=True)
    return xc * lax.rsqrt(var + LN_EPS) * g + b


def _sigmoid(x):
    return 1.0 / (1.0 + jnp.exp(-x))


def _dot(a, b):
    return jnp.dot(a, b, preferred_element_type=F32)


def _dot_nt(a, b):
    return lax.dot_general(a, b, (((1,), (1,)), ((), ())), preferred_element_type=F32)


def _split_bf16(x):
    hi = x.astype(BF16)
    lo = (x - hi.astype(F32)).astype(BF16)
    return hi, lo


def _dot3(a, b_hi, b_lo):
    a_hi, a_lo = _split_bf16(a)
    return _dot(a_hi, b_hi) + (_dot(a_hi, b_lo) + _dot(a_lo, b_hi))


def _dot3_lhs(a_hi, a_lo, b):
    b_hi, b_lo = _split_bf16(b)
    return _dot(a_hi, b_hi) + (_dot(a_hi, b_lo) + _dot(a_lo, b_hi))


def _inproj_kernel(x_ref, g_ref, b_ref, wglu_ref, whyt_ref, wgate_ref, bgate_ref,
                   h_ref, u_ref, hyt_ref, gate_ref, *, apply_ln, sub):
    for r in range(0, x_ref.shape[0], sub):
        rs = slice(r, r + sub)
        h = _ln(x_ref[rs, :], g_ref[...], b_ref[...]) if apply_ln else x_ref[rs, :]
        h_ref[rs, :] = h
        hb = h.astype(BF16)
        glu = _dot(hb, wglu_ref[...])
        cw = glu.shape[1] // 2
        u_ref[rs, :] = glu[:, :cw] * _sigmoid(glu[:, cw:])
        hyt_ref[:, rs] = _dot_nt(whyt_ref[...], hb)
        gl = _dot(hb, wgate_ref[...]) + bgate_ref[...]
        gate_ref[rs, :] = _sigmoid(gl).astype(BF16)


def _inproj(x, g, b, wglu, whyt, wgate, bgate, *, tm, apply_ln):
    bsz, length, d = x.shape
    n_glu, n_hy, n_gate = wglu.shape[1], whyt.shape[0], wgate.shape[1]
    const = lambda i, j: (0, 0)
    return pl.pallas_call(
        functools.partial(_inproj_kernel, apply_ln=apply_ln, sub=min(SUB_ROWS, tm)),
        grid=(bsz, length // tm),
        in_specs=[
            pl.BlockSpec((None, tm, d), lambda i, j: (i, j, 0)),
            pl.BlockSpec((1, d), const), pl.BlockSpec((1, d), const),
            pl.BlockSpec((d, n_glu), const),
            pl.BlockSpec((n_hy, d), const),
            pl.BlockSpec((d, n_gate), const),
            pl.BlockSpec((1, n_gate), const),
        ],
        out_specs=[
            pl.BlockSpec((None, tm, d), lambda i, j: (i, j, 0)),
            pl.BlockSpec((None, tm, n_glu // 2), lambda i, j: (i, j, 0)),
            pl.BlockSpec((None, n_hy, tm), lambda i, j: (i, 0, j)),
            pl.BlockSpec((None, tm, n_gate), lambda i, j: (i, j, 0)),
        ],
        out_shape=[
            jax.ShapeDtypeStruct((bsz, length, d), F32),
            jax.ShapeDtypeStruct((bsz, length, n_glu // 2), F32),
            jax.ShapeDtypeStruct((bsz, n_hy, length), F32),
            jax.ShapeDtypeStruct((bsz, length, n_gate), BF16),
        ],
        compiler_params=_cparams("parallel", "parallel"),
        name="inproj",
    )(x, g, b, wglu, whyt, wgate, bgate)


def _conf_kernel(prev_ref, cur_ref, next_ref, w_ref, cb_ref, g_ref, b_ref, o_ref, ext_ref, sh_ref,
                 *, tl, halo, ksize, rows):
    j = pl.program_id(1)
    nj = pl.num_programs(1)
    pad = (ksize - 1) // 2
    ext_ref[0:halo, :] = jnp.where(j > 0, prev_ref[...], 0.0)
    ext_ref[halo:halo + tl, :] = cur_ref[...]
    ext_ref[halo + tl:halo + tl + halo, :] = jnp.where(j < nj - 1, next_ref[...], 0.0)
    n_sh = sh_ref.shape[1]
    for r in range(1, SUBLANES):
        sh_ref[r - 1, :, :] = ext_ref[r:r + n_sh, :]
    w = w_ref[...]
    cb, g, b = cb_ref[...], g_ref[...], b_ref[...]
    for r0 in range(0, tl, rows):
        acc = jnp.zeros((rows, w.shape[1]), F32)
        for k in range(ksize):
            q, r = divmod(halo - pad + k + r0, SUBLANES)
            q *= SUBLANES
            src = ext_ref[q:q + rows, :] if r == 0 else sh_ref[r - 1, q:q + rows, :]
            acc = acc + w[k:k + 1, :] * src
        y = _ln(acc + cb, g, b)
        o_ref[r0:r0 + rows, :] = (y * _sigmoid(y)).astype(o_ref.dtype)


def _conf(u, w, cb, g, b, *, tl, halo=16, rows=64):
    bsz, length, c = u.shape
    ksize = w.shape[0]
    assert (ksize - 1) // 2 <= halo and tl % halo == 0 and tl % rows == 0
    nh = tl // halo
    last = length // halo - 1
    const = lambda i, j: (0, 0)
    return pl.pallas_call(
        functools.partial(_conf_kernel, tl=tl, halo=halo, ksize=ksize, rows=rows),
        grid=(bsz, length // tl),
        in_specs=[
            pl.BlockSpec((None, halo, c), lambda i, j: (i, jnp.maximum(j * nh - 1, 0), 0)),
            pl.BlockSpec((None, tl, c), lambda i, j: (i, j, 0)),
            pl.BlockSpec((None, halo, c), lambda i, j: (i, jnp.minimum((j + 1) * nh, last), 0)),
            pl.BlockSpec((ksize, c), const),
            pl.BlockSpec((1, c), const), pl.BlockSpec((1, c), const), pl.BlockSpec((1, c), const),
        ],
        out_specs=pl.BlockSpec((None, tl, c), lambda i, j: (i, j, 0)),
        out_shape=jax.ShapeDtypeStruct((bsz, length, c), BF16),
        scratch_shapes=[pltpu.VMEM((tl + 2 * halo, c), F32),
                        pltpu.VMEM((SUBLANES - 1, tl + 2 * halo - SUBLANES, c), F32)],
        compiler_params=_cparams("parallel", "parallel"),
        name="conf",
    )(u, u, u, w, cb, g, b)


def _filt_ffn_kernel(feat_ref, w1h_ref, w1l_ref, b1_ref, f1_ref, w2h_ref, w2l_ref, b2_ref, f2_ref, o_ref):
    a = _dot3_lhs(w1h_ref[...], w1l_ref[...], feat_ref[...]) + b1_ref[...]
    h = jnp.sin(f1_ref[...] * a)
    a2 = _dot3_lhs(w2h_ref[...], w2l_ref[...], h) + b2_ref[...]
    o_ref[...] = jnp.sin(f2_ref[...] * a2)


def _filt_ffn(feat_t, w1h, w1l, b1, f1, w2h, w2l, b2, f2, *, tn):
    kp, n = feat_t.shape
    m = w1h.shape[0]
    const = lambda i: (0, 0)
    return pl.pallas_call(
        _filt_ffn_kernel,
        grid=(n // tn,),
        in_specs=[pl.BlockSpec((kp, tn), lambda i: (0, i)),
                  pl.BlockSpec((m, kp), const), pl.BlockSpec((m, kp), const),
                  pl.BlockSpec((m, 1), const), pl.BlockSpec((m, 1), const),
                  pl.BlockSpec((m, m), const), pl.BlockSpec((m, m), const),
                  pl.BlockSpec((m, 1), const), pl.BlockSpec((m, 1), const)],
        out_specs=pl.BlockSpec((m, tn), lambda i: (0, i)),
        out_shape=jax.ShapeDtypeStruct((m, n), F32),
        compiler_params=_cparams("parallel"),
        name="filt_ffn",
    )(feat_t, w1h, w1l, b1, f1, w2h, w2l, b2, f2)


def _filt_kernel(h2_ref, w3fh_ref, w3fl_ref, w3bh_ref, w3bl_ref, delta_ref, text_ref, valid_ref, o_ref):
    n = h2_ref.shape[1]
    half = n // 2
    decay = jnp.exp(-text_ref[...] * delta_ref[...]) * valid_ref[...]
    kf = _dot3_lhs(w3fh_ref[...], w3fl_ref[...], h2_ref[:, 0:half]) * decay[:, 0:half]
    kb = _dot3_lhs(w3bh_ref[...], w3bl_ref[...], h2_ref[:, half:n]) * decay[:, half:n]
    ss = jnp.sum(kf * kf, axis=1, keepdims=True) + jnp.sum(kb * kb, axis=1, keepdims=True)
    scale = lax.rsqrt(ss + 1e-6)
    o_ref[:, 0:half] = kf * scale
    o_ref[:, half:n] = kb * scale


def _filt(h2t, w3fh, w3fl, w3bh, w3bl, delta, text, valid, *, g):
    m, n = h2t.shape
    rows = w3fh.shape[0]
    const = lambda i: (0, 0)
    return pl.pallas_call(
        _filt_kernel,
        grid=(rows // g,),
        in_specs=[pl.BlockSpec((m, n), const),
                  pl.BlockSpec((g, m), lambda i: (i, 0)), pl.BlockSpec((g, m), lambda i: (i, 0)),
                  pl.BlockSpec((g, m), lambda i: (i, 0)), pl.BlockSpec((g, m), lambda i: (i, 0)),
                  pl.BlockSpec((g, 1), lambda i: (i, 0)),
                  pl.BlockSpec((1, n), const), pl.BlockSpec((1, n), const)],
        out_specs=pl.BlockSpec((g, n), lambda i: (i, 0)),
        out_shape=jax.ShapeDtypeStruct((rows, n), F32),
        compiler_params=_cparams("parallel"),
        name="filt",
    )(h2t, w3fh, w3fl, w3bh, w3bl, delta, text, valid)


def _dft_tables(r1, n1_used):
    n = r1 * LANES
    a1 = 2.0 * np.pi * np.outer(np.arange(n1_used), np.arange(r1)) / r1
    a2 = 2.0 * np.pi * np.outer(np.arange(LANES), np.arange(LANES)) / LANES
    at = 2.0 * np.pi * np.outer(np.arange(LANES), np.arange(r1)) / n
    return n, a1, a2, at


def _cplx_rhs(ang, sign):
    c, s = np.cos(ang), sign * np.sin(ang)
    return np.block([[c, s], [-s, c]])


def _fwd_steps(x_t, w1, twr, twi, w2):
    g, n2, k1w = x_t.shape
    r1 = twr.shape[1]
    a = _dot(x_t.reshape(g * n2, k1w).astype(BF16), w1).reshape(g, n2, 2 * r1)
    ar, ai = a[:, :, :r1], a[:, :, r1:]
    br = ar * twr - ai * twi
    bi = ar * twi + ai * twr
    bt = jnp.concatenate([jnp.swapaxes(br, 1, 2), jnp.swapaxes(bi, 1, 2)], axis=-1)
    c = _dot(bt.reshape(g * r1, 2 * n2).astype(BF16), w2)
    return c.reshape(g, r1, 2 * n2)


def _inv_steps(d, w3, twr_t, twi_t, w4):
    g, r1, w = d.shape
    n2 = w // 2
    e = _dot(d.reshape(g * r1, w).astype(BF16), w3).reshape(g, r1, w)
    er, ei = e[:, :, :n2], e[:, :, n2:]
    fr = er * twr_t + ei * twi_t
    fi = ei * twr_t - er * twi_t
    ft = jnp.concatenate([jnp.swapaxes(fr, 1, 2), jnp.swapaxes(fi, 1, 2)], axis=-1)
    y = _dot(ft.reshape(g * n2, 2 * r1).astype(BF16), w4)
    return y.reshape(g, n2, w4.shape[1])


def _kfft_kernel(k_ref, w1_ref, twr_ref, twi_ref, w2_ref, o_ref):
    x_t = jnp.swapaxes(k_ref[...], 1, 2)
    o_ref[...] = _fwd_steps(x_t, w1_ref[...], twr_ref[...], twi_ref[...], w2_ref[...])


def _kfft(k3, w1, twr, twi, w2, *, g):
    rows, r1, lanes = k3.shape
    const = lambda i: (0, 0)
    return pl.pallas_call(
        _kfft_kernel,
        grid=(rows // g,),
        in_specs=[pl.BlockSpec((g, r1, lanes), lambda i: (i, 0, 0)),
                  pl.BlockSpec(w1.shape, const), pl.BlockSpec(twr.shape, const),
                  pl.BlockSpec(twi.shape, const), pl.BlockSpec(w2.shape, const)],
        out_specs=pl.BlockSpec((g, r1, 2 * lanes), lambda i: (i, 0, 0)),
        out_shape=jax.ShapeDtypeStruct((rows, r1, 2 * lanes), F32),
        compiler_params=_cparams("parallel"),
        name="kfft",
    )(k3, w1, twr, twi, w2)


def _short_conv(a, w, b):
    p, g, r, l = a.shape
    a3 = a.reshape(p * g, r, l)
    lane = lax.broadcasted_iota(I32, a3.shape, 2)
    row = lax.broadcasted_iota(I32, a3.shape, 1)
    pl_ = pltpu.roll(a3, 1, 2)
    ql = pltpu.roll(pl_, 1, 1)
    prev = jnp.where(lane == 0, jnp.where(row == 0, 0.0, ql), pl_)
    pr = pltpu.roll(a3, l - 1, 2)
    qr = pltpu.roll(pr, r - 1, 1)
    nxt = jnp.where(lane == l - 1, jnp.where(row == r - 1, 0.0, qr), pr)
    prev, nxt = prev.reshape(a.shape), nxt.reshape(a.shape)
    return w[0][None] * prev + w[1][None] * a + w[2][None] * nxt + b[None]


def _hyena_kernel(x1_ref, x2_ref, v_ref, sw1_ref, sb1_ref, sw2_ref, sb2_ref, swv_ref, sbv_ref,
                  kf0_ref, kf1_ref, skip_ref, w1_ref, twr_ref, twi_ref, w2_ref, w3_ref,
                  twrt_ref, twit_ref, w4_ref, z_ref):
    x1 = _short_conv(x1_ref[...], sw1_ref[...], sb1_ref[...])
    x2 = _short_conv(x2_ref[...], sw2_ref[...], sb2_ref[...])
    z = _short_conv(v_ref[...], swv_ref[...], sbv_ref[...])
    rh = z.shape[2]
    for gate, kf_ref, o in ((x1, kf0_ref, 0), (x2, kf1_ref, 1)):
        s = jnp.concatenate([z[0], z[1]], axis=1)
        c = _fwd_steps(jnp.swapaxes(s, 1, 2), w1_ref[...], twr_ref[...], twi_ref[...], w2_ref[...])
        kf = kf_ref[...]
        n2 = c.shape[2] // 2
        cr, ci, kr, ki = c[:, :, :n2], c[:, :, n2:], kf[:, :, :n2], kf[:, :, n2:]
        d = jnp.concatenate([cr * kr - ci * ki, cr * ki + ci * kr], axis=-1)
        y = _inv_steps(d, w3_ref[...], twrt_ref[...], twit_ref[...], w4_ref[...])
        y = jnp.swapaxes(y, 1, 2)
        conv = jnp.stack([y[:, :rh], y[:, rh:]], axis=0)
        z = gate * (conv + skip_ref[o][None] * z)
    z_ref[...] = z


def _hyena(hy4, sw, sb, kf, skip, tabs, *, g):
    bsz, c3, rh, lanes = hy4.shape
    c = c3 // 3
    r1 = 2 * rh
    ncb = c // g
    w1, twr, twi, w2, w3, twrt, twit, w4 = tabs
    const2 = lambda i, p: (0, 0)
    data = lambda off: pl.BlockSpec((2, g, rh, lanes), lambda i, p, off=off: (p, off + i, 0, 0))
    wspec = lambda off: pl.BlockSpec((3, g, 1, lanes), lambda i, p, off=off: (0, off + i, 0, 0))
    bspec = lambda off: pl.BlockSpec((g, 1, lanes), lambda i, p, off=off: (off + i, 0, 0))
    kspec = lambda off: pl.BlockSpec((g, r1, 2 * lanes), lambda i, p, off=off: (off + i, 0, 0))
    return pl.pallas_call(
        _hyena_kernel,
        grid=(ncb, bsz // 2),
        in_specs=[data(0), data(ncb), data(2 * ncb),
                  wspec(0), bspec(0), wspec(ncb), bspec(ncb), wspec(2 * ncb), bspec(2 * ncb),
                  kspec(0), kspec(ncb),
                  pl.BlockSpec((2, g, 1, lanes), lambda i, p: (0, i, 0, 0)),
                  pl.BlockSpec(w1.shape, const2), pl.BlockSpec(twr.shape, const2),
                  pl.BlockSpec(twi.shape, const2), pl.BlockSpec(w2.shape, const2),
                  pl.BlockSpec(w3.shape, const2), pl.BlockSpec(twrt.shape, const2),
                  pl.BlockSpec(twit.shape, const2), pl.BlockSpec(w4.shape, const2)],
        out_specs=pl.BlockSpec((2, g, rh, lanes), lambda i, p: (p, i, 0, 0)),
        out_shape=jax.ShapeDtypeStruct((bsz, c, rh, lanes), F32),
        compiler_params=_cparams("parallel", "parallel"),
        name="hyena",
    )(hy4, hy4, hy4, sw, sb, sw, sb, sw, sb, kf, kf, skip, w1, twr, twi, w2, w3, twrt, twit, w4)


def _mix_kernel(ua_ref, zt_ref, gate_ref, h_ref, wa_ref, wb_ref, wm_ref, g_ref, b_ref, o_ref, *, alpha, sub):
    for r in range(0, h_ref.shape[0], sub):
        rs = slice(r, r + sub)
        ya = _dot(ua_ref[rs, :], wa_ref[...])
        zt = jnp.transpose(zt_ref[:, rs]).astype(BF16)
        yb = _dot(zt, wb_ref[...])
        gt = gate_ref[rs, :].astype(F32)
        d = ya.shape[1]
        m = gt[:, :d] * ya + gt[:, d:] * yb
        mixed = _dot(m.astype(BF16), wm_ref[...])
        o_ref[rs, :] = _ln(alpha * h_ref[rs, :] + mixed, g_ref[...], b_ref[...])


def _mix(ua, zt, gates, h, wa, wb, wm, g, b, *, tm, alpha):
    bsz, length, d = h.shape
    c = ua.shape[2]
    const = lambda i, j: (0, 0)
    return pl.pallas_call(
        functools.partial(_mix_kernel, alpha=alpha, sub=min(SUB_ROWS, tm)),
        grid=(bsz, length // tm),
        in_specs=[pl.BlockSpec((None, tm, c), lambda i, j: (i, j, 0)),
                  pl.BlockSpec((None, c, tm), lambda i, j: (i, 0, j)),
                  pl.BlockSpec((None, tm, 2 * d), lambda i, j: (i, j, 0)),
                  pl.BlockSpec((None, tm, d), lambda i, j: (i, j, 0)),
                  pl.BlockSpec((c, d), const), pl.BlockSpec((c, d), const), pl.BlockSpec((d, d), const),
                  pl.BlockSpec((1, d), const), pl.BlockSpec((1, d), const)],
        out_specs=pl.BlockSpec((None, tm, d), lambda i, j: (i, j, 0)),
        out_shape=jax.ShapeDtypeStruct((bsz, length, d), F32),
        compiler_params=_cparams("parallel", "parallel"),
        name="mix",
    )(ua, zt, gates, h, wa, wb, wm, g, b)


def _kv_kernel(mem_ref, wk_ref, wv_ref, k_ref, v_ref):
    mb = mem_ref[...].astype(BF16)
    k_ref[...] = _dot(mb, wk_ref[...]).astype(BF16)
    v_ref[...] = _dot(mb, wv_ref[...]).astype(BF16)


def _kv(mem, wk, wv):
    bsz, m, d = mem.shape
    const = lambda i: (0, 0)
    blk = pl.BlockSpec((None, m, d), lambda i: (i, 0, 0))
    return pl.pallas_call(
        _kv_kernel,
        grid=(bsz,),
        in_specs=[blk, pl.BlockSpec((d, d), const), pl.BlockSpec((d, d), const)],
        out_specs=[blk, blk],
        out_shape=[jax.ShapeDtypeStruct((bsz, m, d), BF16)] * 2,
        compiler_params=_cparams("parallel"),
        name="kv",
    )(mem, wk, wv)


def _xattn_kernel(x_ref, k_ref, v_ref, wq_ref, wo_ref, g_ref, b_ref, wrh_ref, wrl_ref,
                  x2_ref, x2b_ref, aff_ref, *, alpha, heads, sub):
    d = x_ref.shape[1]
    dh = d // heads
    wrh, wrl = wrh_ref[...], wrl_ref[...]
    for r in range(0, x_ref.shape[0], sub):
        rs = slice(r, r + sub)
        x = x_ref[rs, :]
        q = (_dot(x.astype(BF16), wq_ref[...]) * (dh ** -0.5)).astype(BF16)
        outs = []
        for hd in range(heads):
            sl = slice(hd * dh, (hd + 1) * dh)
            s = _dot_nt(q[:, sl], k_ref[:, sl])
            s = s - jnp.max(s, axis=-1, keepdims=True)
            p = jnp.exp(s)
            p = p / jnp.sum(p, axis=-1, keepdims=True)
            outs.append(_dot(p.astype(BF16), v_ref[:, sl]))
        o = jnp.concatenate(outs, axis=-1)
        xa = _dot(o.astype(BF16), wo_ref[...])
        x2 = _ln(alpha * x + xa, g_ref[...], b_ref[...])
        x2_ref[rs, :] = x2
        x2h, x2l = _split_bf16(x2)
        x2b_ref[rs, :] = x2h
        logits = _dot_nt(wrh, x2h) + (_dot_nt(wrh, x2l) + _dot_nt(wrl, x2h))
        logits = logits - jnp.max(logits, axis=0, keepdims=True)
        ex = jnp.exp(logits)
        aff_ref[:, rs] = ex / jnp.sum(ex, axis=0, keepdims=True)


def _xattn(x, k, v, wq, wo, g, b, wrh, wrl, *, tm, alpha, heads):
    bsz, length, d = x.shape
    m = k.shape[1]
    e = wrh.shape[0]
    const = lambda i, j: (0, 0)
    tok = pl.BlockSpec((None, tm, d), lambda i, j: (i, j, 0))
    mem = pl.BlockSpec((None, m, d), lambda i, j: (i, 0, 0))
    return pl.pallas_call(
        functools.partial(_xattn_kernel, alpha=alpha, heads=heads, sub=min(SUB_ROWS, tm)),
        grid=(bsz, length // tm),
        in_specs=[tok, mem, mem, pl.BlockSpec((d, d), const), pl.BlockSpec((d, d), const),
                  pl.BlockSpec((1, d), const), pl.BlockSpec((1, d), const),
                  pl.BlockSpec((e, d), const), pl.BlockSpec((e, d), const)],
        out_specs=[tok, tok, pl.BlockSpec((None, e, tm), lambda i, j: (i, 0, j))],
        out_shape=[jax.ShapeDtypeStruct((bsz, length, d), F32),
                   jax.ShapeDtypeStruct((bsz, length, d), BF16),
                   jax.ShapeDtypeStruct((bsz, e, length), F32)],
        compiler_params=_cparams("parallel", "parallel"),
        name="xattn",
    )(x, k, v, wq, wo, g, b, wrh, wrl)


def _excl_cumsum_lanes(mask_f32, tri, write):
    e, t = mask_f32.shape
    carry = jnp.zeros((e, 1), F32)
    for c in range(t // LANES):
        m = mask_f32[:, c * LANES:(c + 1) * LANES]
        inc = _dot(m.astype(BF16), tri)
        write(c, inc - m + carry)
        carry = carry + inc[:, LANES - 1:LANES]


def _select_kernel(aff_ref, tri_ref, slot_ref, gate_ref, rank_ref, cnt_ref, *, cap):
    a = aff_ref[...]
    e = a.shape[0]
    v = jnp.zeros((e, 1), I32)
    for bit in range(30, -1, -1):
        cand = v | (1 << bit)
        cnt = jnp.sum(jnp.where(a >= pltpu.bitcast(cand, F32), 1.0, 0.0), axis=1, keepdims=True)
        v = jnp.where(cnt >= cap, cand, v)
    thr = pltpu.bitcast(v, F32)
    gt = a > thr
    eq = a == thr
    need = cap - jnp.sum(jnp.where(gt, 1.0, 0.0), axis=1, keepdims=True)
    tri = tri_ref[...]

    def write_eq(c, val):
        cnt_ref[:, c * LANES:(c + 1) * LANES] = val
    _excl_cumsum_lanes(jnp.where(eq, 1.0, 0.0), tri, write_eq)
    sel = gt | (eq & (cnt_ref[...] < need))
    gate_ref[...] = jnp.where(sel, a, 0.0)

    def write_rank(c, val):
        cnt_ref[:, c * LANES:(c + 1) * LANES] = val
    _excl_cumsum_lanes(jnp.where(sel, 1.0, 0.0), tri, write_rank)
    rank = cnt_ref[...].astype(I32)
    rank_ref[...] = rank
    slot_ref[...] = jnp.where(sel, rank, -1)


def _select(aff, tri, *, cap):
    bsz, e, t = aff.shape
    blk = pl.BlockSpec((None, e, t), lambda i: (i, 0, 0))
    return pl.pallas_call(
        functools.partial(_select_kernel, cap=cap),
        grid=(bsz,),
        in_specs=[blk, pl.BlockSpec((LANES, LANES), lambda i: (0, 0))],
        out_specs=[blk, blk, blk],
        out_shape=[jax.ShapeDtypeStruct((bsz, e, t), I32), jax.ShapeDtypeStruct((bsz, e, t), F32),
                   jax.ShapeDtypeStruct((bsz, e, t), I32)],
        scratch_shapes=[pltpu.VMEM((e, t), F32)],
        compiler_params=_cparams("parallel"),
        name="select",
    )(aff, tri)


def _window_start(r0, cap, win, align):
    return pl.multiple_of(jnp.minimum((r0 // align) * align, cap - win), align)


def _dispatch_kernel(jlo_ref, nch_ref, x_ref, slot_ref, ghl_ref, o_ref, gs_ref, acc_ref, gacc_ref,
                     *, st, kc, tt):
    b, e = pl.program_id(0), pl.program_id(1)
    t_total = x_ref.shape[0]
    n_e = ghl_ref.shape[1] // 2
    for s in range(o_ref.shape[0] // st):
        j0 = jlo_ref[b, e, s]

        def chunk(c, s=s, j0=j0):
            want = (j0 + c * (kc // tt)) * tt
            t0 = pl.multiple_of(jnp.minimum(want, t_total - kc), tt)
            tok = t0 + lax.broadcasted_iota(I32, (1, kc), 1)
            rel = jnp.where(tok >= want, slot_ref[:, pl.ds(t0, kc)] - s * st, -1)
            onehot = jnp.where(lax.broadcasted_iota(I32, (st, kc), 0) == rel, 1.0, 0.0).astype(BF16)
            xs = _dot(onehot, x_ref[pl.ds(t0, kc), :])
            gs = _dot(onehot, ghl_ref[pl.ds(t0, kc), :])
            return xs, gs

        xs, gs = chunk(0)
        acc_ref[...] = xs
        gacc_ref[...] = gs

        def body(c, carry):
            xs, gs = chunk(c)
            acc_ref[...] += xs
            gacc_ref[...] += gs
            return carry
        lax.fori_loop(1, nch_ref[b, e, s], body, 0)
        o_ref[s * st:(s + 1) * st, :] = acc_ref[...].astype(o_ref.dtype)
        lane = lax.broadcasted_iota(I32, (st, 2 * n_e), 1)
        mine = (lane == e) | (lane == e + n_e)
        gs_ref[s * st:(s + 1) * st, :] = jnp.sum(jnp.where(mine, gacc_ref[...], 0.0), axis=1, keepdims=True)


def _dispatch(jlo, nch, xb, slot4, gate_hl, *, cap, st, kc, tt):
    bsz, t, d = xb.shape
    e = slot4.shape[1]
    assert cap % st == 0 and kc % tt == 0 and t % tt == 0 and kc <= t
    return pl.pallas_call(
        functools.partial(_dispatch_kernel, st=st, kc=kc, tt=tt),
        grid_spec=pltpu.PrefetchScalarGridSpec(
            num_scalar_prefetch=2,
            grid=(bsz, e),
            in_specs=[pl.BlockSpec((None, t, d), lambda i, j, a, c: (i, 0, 0)),
                      pl.BlockSpec((None, None, 1, t), lambda i, j, a, c: (i, j, 0, 0)),
                      pl.BlockSpec((None, t, 2 * e), lambda i, j, a, c: (i, 0, 0))],
            out_specs=[pl.BlockSpec((None, None, cap, d), lambda i, j, a, c: (i, j, 0, 0)),
                       pl.BlockSpec((None, None, cap, 1), lambda i, j, a, c: (i, j, 0, 0))],
            scratch_shapes=[pltpu.VMEM((st, d), F32), pltpu.VMEM((st, 2 * e), F32)]),
        out_shape=[jax.ShapeDtypeStruct((bsz, e, cap, d), BF16),
                   jax.ShapeDtypeStruct((bsz, e, cap, 1), F32)],
        compiler_params=_cparams("parallel", "parallel"),
        name="dispatch",
    )(jlo, nch, xb, slot4, gate_hl)


def _expert_kernel(x_ref, gs_ref, wg_ref, wu_ref, wd_ref, o_ref, acc_ref, *, fc):
    x = x_ref[...]
    f = wg_ref.shape[1]
    for i, f0 in enumerate(range(0, f, fc)):
        hg = _dot(x, wg_ref[:, f0:f0 + fc])
        hu = _dot(x, wu_ref[:, f0:f0 + fc])
        act = (hg * _sigmoid(hg) * hu).astype(BF16)
        part = _dot(act, wd_ref[f0:f0 + fc, :])
        if i == 0:
            acc_ref[...] = part
        else:
            acc_ref[...] += part
    o_ref[...] = (acc_ref[...] * gs_ref[...]).astype(o_ref.dtype)


def _expert(xg, gs, wg, wu, wd, *, fc):
    bsz, e, cap, d = xg.shape
    f = wg.shape[2]
    tok = pl.BlockSpec((None, None, cap, d), lambda i, j: (j, i, 0, 0))
    return pl.pallas_call(
        functools.partial(_expert_kernel, fc=fc),
        grid=(e, bsz),
        in_specs=[tok,
                  pl.BlockSpec((None, None, cap, 1), lambda i, j: (j, i, 0, 0)),
                  pl.BlockSpec((None, d, f), lambda i, j: (i, 0, 0)),
                  pl.BlockSpec((None, d, f), lambda i, j: (i, 0, 0)),
                  pl.BlockSpec((None, f, d), lambda i, j: (i, 0, 0))],
        out_specs=tok,
        out_shape=jax.ShapeDtypeStruct((bsz, e, cap, d), BF16),
        scratch_shapes=[pltpu.VMEM((cap, d), F32)],
        compiler_params=_cparams("parallel", "parallel"),
        name="expert",
    )(xg, gs, wg, wu, wd)


def _combine_kernel(r0_ref, x_ref, slot_ref, g_ref, b_ref, y_ref, o_ref, *, tt, win, align, alpha):
    b, j = pl.program_id(0), pl.program_id(1)
    n_e, cap = y_ref.shape[0], y_ref.shape[1]
    acc = alpha * x_ref[...]
    slots = slot_ref[...]
    for e in range(n_e):
        start = _window_start(r0_ref[b, e, j], cap, win, align)
        rel = slots[:, e:e + 1] - start
        onehot = jnp.where(lax.broadcasted_iota(I32, (tt, win), 1) == rel, 1.0, 0.0).astype(BF16)
        acc = acc + _dot(onehot, y_ref[e, pl.ds(start, win), :])
    o_ref[...] = _ln(acc, g_ref[...], b_ref[...])


def _combine(r0, x, slot_t, g, b, y, *, tt, win, align, alpha):
    bsz, t, d = x.shape
    e, cap = y.shape[1], y.shape[2]
    assert win >= tt + align and win <= cap and (cap - win) % align == 0
    const = lambda i, j, r: (0, 0)
    return pl.pallas_call(
        functools.partial(_combine_kernel, tt=tt, win=win, align=align, alpha=alpha),
        grid_spec=pltpu.PrefetchScalarGridSpec(
            num_scalar_prefetch=1,
            grid=(bsz, t // tt),
            in_specs=[pl.BlockSpec((None, tt, d), lambda i, j, r: (i, j, 0)),
                      pl.BlockSpec((None, tt, e), lambda i, j, r: (i, j, 0)),
                      pl.BlockSpec((1, d), const), pl.BlockSpec((1, d), const),
                      pl.BlockSpec((None, e, cap, d), lambda i, j, r: (i, 0, 0, 0),
                                   pipeline_mode=pl.Buffered(1))],
            out_specs=pl.BlockSpec((None, tt, d), lambda i, j, r: (i, j, 0))),
        out_shape=jax.ShapeDtypeStruct((bsz, t, d), F32),
        compiler_params=_cparams("parallel", "parallel"),
        name="combine",
    )(r0, x, slot_t, g, b, y)


def _hyena_constants(length):
    n = 2 * length
    m = np.arange(n)
    pos = np.where(m < length, m, n - m).astype(np.float64)
    pos[length] = 0.0
    t = pos / max(length - 1, 1)
    f = np.linspace(1e-4, HY_BANDS - 1, HY_BANDS)
    ang = (2.0 * np.pi * pos / length)[:, None] * f[None, :]
    feat = np.concatenate([t[:, None], np.cos(ang), -np.sin(ang)], axis=-1)
    feat_t = np.zeros((LANES, n), np.float32)
    feat_t[:feat.shape[1]] = feat.T
    valid = np.ones((1, n), np.float32)
    valid[0, length] = 0.0
    return feat_t, t[None, :].astype(np.float32), valid


def _decay_rates(width):
    max_decay = math.log(HY_DECAY_TARGET) / HY_FAST_DECAY
    min_decay = math.log(HY_DECAY_TARGET) / HY_SLOW_DECAY
    return np.abs(np.linspace(min_decay, max_decay, width, dtype=np.float32)).astype(np.float32)


def _dft_constants(r1):
    n, a1h, a2, at = _dft_tables(r1, r1 // 2)
    _, a1f, _, _ = _dft_tables(r1, r1)
    bf = lambda x: jnp.asarray(x, F32).astype(BF16)
    w1_data = bf(_cplx_rhs(a1h, -1.0))
    w1_filt = bf(np.concatenate([np.cos(a1f), -np.sin(a1f)], axis=1))
    w2 = bf(_cplx_rhs(a2, -1.0))
    w3 = bf(_cplx_rhs(a2.T, 1.0))
    w4 = bf(_cplx_rhs(a1h.T, 1.0) / n)
    twr, twi = np.cos(at), -np.sin(at)
    f32 = lambda x: jnp.asarray(x, F32)
    return dict(w1_data=w1_data, w1_filt=w1_filt, w2=w2, w3=w3, w4=w4,
                twr=f32(twr), twi=f32(twi), twr_t=f32(twr.T), twi_t=f32(twi.T))


def kernel(x, mem, ln_in_g, ln_in_b, w_in, b_gate, conf_dw_w, conf_dw_b, conf_ln_g, conf_ln_b, conf_w_out, hy_short_w, hy_short_b, hy_ffn_w1, hy_ffn_b1, hy_freq1, hy_ffn_w2, hy_ffn_b2, hy_freq2, hy_ffn_w3, hy_skip, hy_w_out, w_mix_out, ln_mix_g, ln_mix_b, xa_wq, xa_wk, xa_wv, xa_wo, ln_xa_g, ln_xa_b, moe_w_router, moe_w_gate, moe_w_up, moe_w_down, ln_moe_g, ln_moe_b):
    bsz, length, d = x.shape
    depth = w_in.shape[0]
    alpha = (2.0 * depth) ** 0.25
    cw = conf_dw_w.shape[2]
    hw = hy_skip.shape[2]
    n_glu, n_hy = 2 * cw, 3 * hw
    n_exp = moe_w_router.shape[2]
    cap = max(1, N_EXPERT_CAPACITY * length // n_exp)
    rh = length // LANES
    r1 = 2 * rh
    assert bsz % 2 == 0 and length % LANES == 0

    row = lambda v: v.reshape(1, -1).astype(F32)
    col = lambda v: v.reshape(-1, 1).astype(F32)

    feat_t, t_ext, valid = _hyena_constants(length)
    dft = _dft_constants(r1)
    rates = _decay_rates(hw)
    tri = jnp.asarray(np.triu(np.ones((LANES, LANES), np.float32))).astype(BF16)

    n_ord = hy_skip.shape[1]
    tm = min(512, length)
    lane_b = lambda v: jnp.broadcast_to(v[..., None, None], v.shape + (1, LANES)).astype(F32)
    dft_data = (dft["w1_data"], dft["twr"], dft["twi"], dft["w2"], dft["w3"], dft["twr_t"], dft["twi_t"],
                dft["w4"])

    h = x
    for i in range(depth):
        w1h, w1l = _split_bf16(jnp.pad(hy_ffn_w1[i].T, ((0, 0), (0, LANES - hy_ffn_w1.shape[1]))))
        w2h, w2l = _split_bf16(hy_ffn_w2[i].T)
        h2t = _filt_ffn(jnp.asarray(feat_t), w1h, w1l, col(hy_ffn_b1[i]), col(hy_freq1[i]),
                        w2h, w2l, col(hy_ffn_b2[i]), col(hy_freq2[i]), tn=min(2048, 2 * length))
        w3 = hy_ffn_w3[i].reshape(-1, n_ord, 2, hw)
        w3f = jnp.transpose(w3[:, :, 0], (1, 2, 0)).reshape(n_ord * hw, -1)
        w3b = jnp.transpose(w3[:, :, 1], (1, 2, 0)).reshape(n_ord * hw, -1)
        w3fh, w3fl = _split_bf16(w3f)
        w3bh, w3bl = _split_bf16(w3b)
        delta = jnp.asarray(np.tile(rates, n_ord).reshape(-1, 1))
        kern = _filt(h2t, w3fh, w3fl, w3bh, w3bl, delta, jnp.asarray(t_ext), jnp.asarray(valid),
                     g=min(128, hw))
        kf = _kfft(kern.reshape(n_ord * hw, r1, LANES), dft["w1_filt"], dft["twr"], dft["twi"], dft["w2"],
                   g=min(16, hw))

        wi = w_in[i]
        wglu = wi[:, :n_glu].astype(BF16)
        whyt = wi[:, n_glu:n_glu + n_hy].T.astype(BF16)
        wgate = wi[:, n_glu + n_hy:].astype(BF16)
        hn, u, hyt, gates = _inproj(h, row(ln_in_g), row(ln_in_b), wglu, whyt, wgate, row(b_gate[i]),
                                    tm=tm, apply_ln=(i == 0))

        ua = _conf(u, conf_dw_w[i], row(conf_dw_b[i]), row(conf_ln_g[i]), row(conf_ln_b[i]),
                   tl=min(512, length))

        z = _hyena(hyt.reshape(bsz, n_hy, rh, LANES), lane_b(hy_short_w[i]), lane_b(hy_short_b[i]), kf,
                   lane_b(hy_skip[i]), dft_data, g=min(16, hw))

        x1 = _mix(ua, z.reshape(bsz, hw, length), gates, hn, conf_w_out[i].astype(BF16),
                  hy_w_out[i].astype(BF16), w_mix_out[i].astype(BF16), row(ln_mix_g[i]), row(ln_mix_b[i]),
                  tm=tm, alpha=alpha)

        k, v = _kv(mem, xa_wk[i].astype(BF16), xa_wv[i].astype(BF16))
        wrh, wrl = _split_bf16(moe_w_router[i].T)
        x2, x2b, aff = _xattn(x1, k, v, xa_wq[i].astype(BF16), xa_wo[i].astype(BF16), row(ln_xa_g[i]),
                              row(ln_xa_b[i]), wrh, wrl, tm=tm, alpha=alpha, heads=XA_HEADS)

        slot, gate, rank = _select(aff, tri, cap=cap)
        tt, st, align = LANES, LANES, 16
        kc = min(12 * tt, length)
        r0 = rank[:, :, ::tt]
        cum = jnp.concatenate([r0, jnp.full((bsz, n_exp, 1), cap, I32)], axis=-1)
        s_lo = (jnp.arange(cap // st, dtype=I32) * st)[None, None, :, None]
        jlo = jnp.sum((cum[:, :, None, 1:] <= s_lo).astype(I32), axis=-1)
        jhi = jnp.sum((cum[:, :, None, :-1] < s_lo + st).astype(I32), axis=-1) - 1
        nch = (jhi - jlo + kc // tt) // (kc // tt)
        gate_hl = jnp.concatenate(_split_bf16(jnp.swapaxes(gate, 1, 2)), axis=-1)
        xg, gslot = _dispatch(jlo, nch, x2b, slot.reshape(bsz, n_exp, 1, length), gate_hl,
                              cap=cap, st=st, kc=kc, tt=tt)
        y = _expert(xg, gslot, moe_w_gate[i].astype(BF16), moe_w_up[i].astype(BF16),
                    moe_w_down[i].astype(BF16), fc=min(512, moe_w_gate.shape[3]))
        h = _combine(r0, x2, jnp.swapaxes(slot, 1, 2), row(ln_moe_g[i]), row(ln_moe_b[i]), y,
                     tt=tt, win=tt + align, align=align, alpha=alpha)
    return h
```

```python
import functools
import math

import numpy as np
import jax
import jax.numpy as jnp
from jax import lax
from jax.experimental import pallas as pl
from jax.experimental.pallas import tpu as pltpu

F32 = jnp.float32
BF16 = jnp.bfloat16
I32 = jnp.int32

LANES = 128
SUBLANES = 8
VMEM_LIMIT_BYTES = 56 * 1024 * 1024
SUB_ROWS = 256

LN_EPS = 1e-5
XA_HEADS = 4
N_EXPERT_CAPACITY = 2
HY_DECAY_TARGET = 1e-2
HY_FAST_DECAY = 0.3
HY_SLOW_DECAY = 1.5
HY_BANDS = 16


def _cparams(*sem):
    return pltpu.CompilerParams(dimension_semantics=sem, vmem_limit_bytes=VMEM_LIMIT_BYTES)


def _ln(x, g, b):
    mu = jnp.mean(x, axis=-1, keepdims=True)
    xc = x - mu
    var = jnp.mean(xc * xc, axis=-1, keepdims=True)
    return xc * lax.rsqrt(var + LN_EPS) * g + b


def _sigmoid(x):
    return 1.0 / (1.0 + jnp.exp(-x))


def _dot(a, b):
    return jnp.dot(a, b, preferred_element_type=F32)


def _dot_nt(a, b):
    return lax.dot_general(a, b, (((1,), (1,)), ((), ())), preferred_element_type=F32)


def _split_bf16(x):
    hi = x.astype(BF16)
    lo = (x - hi.astype(F32)).astype(BF16)
    return hi, lo


def _dot3(a, b_hi, b_lo):
    a_hi, a_lo = _split_bf16(a)
    return _dot(a_hi, b_hi) + (_dot(a_hi, b_lo) + _dot(a_lo, b_hi))


def _dot3_lhs(a_hi, a_lo, b):
    b_hi, b_lo = _split_bf16(b)
    return _dot(a_hi, b_hi) + (_dot(a_hi, b_lo) + _dot(a_lo, b_hi))


def _inproj_kernel(x_ref, g_ref, b_ref, wglu_ref, whyt_ref, wgate_ref, bgate_ref,
                   h_ref, u_ref, hyt_ref, gate_ref, *, apply_ln, sub):
    for r in range(0, x_ref.shape[0], sub):
        rs = slice(r, r + sub)
        h = _ln(x_ref[rs, :], g_ref[...], b_ref[...]) if apply_ln else x_ref[rs, :]
        h_ref[rs, :] = h
        hb = h.astype(BF16)
        glu = _dot(hb, wglu_ref[...])
        cw = glu.shape[1] // 2
        u_ref[rs, :] = glu[:, :cw] * _sigmoid(glu[:, cw:])
        hyt_ref[:, rs] = _dot_nt(whyt_ref[...], hb)
        gl = _dot(hb, wgate_ref[...]) + bgate_ref[...]
        gate_ref[rs, :] = _sigmoid(gl).astype(BF16)


def _inproj(x, g, b, wglu, whyt, wgate, bgate, *, tm, apply_ln):
    bsz, length, d = x.shape
    n_glu, n_hy, n_gate = wglu.shape[1], whyt.shape[0], wgate.shape[1]
    const = lambda i, j: (0, 0)
    return pl.pallas_call(
        functools.partial(_inproj_kernel, apply_ln=apply_ln, sub=min(SUB_ROWS, tm)),
        grid=(bsz, length // tm),
        in_specs=[
            pl.BlockSpec((None, tm, d), lambda i, j: (i, j, 0)),
            pl.BlockSpec((1, d), const), pl.BlockSpec((1, d), const),
            pl.BlockSpec((d, n_glu), const),
            pl.BlockSpec((n_hy, d), const),
            pl.BlockSpec((d, n_gate), const),
            pl.BlockSpec((1, n_gate), const),
        ],
        out_specs=[
            pl.BlockSpec((None, tm, d), lambda i, j: (i, j, 0)),
            pl.BlockSpec((None, tm, n_glu // 2), lambda i, j: (i, j, 0)),
            pl.BlockSpec((None, n_hy, tm), lambda i, j: (i, 0, j)),
            pl.BlockSpec((None, tm, n_gate), lambda i, j: (i, j, 0)),
        ],
        out_shape=[
            jax.ShapeDtypeStruct((bsz, length, d), F32),
            jax.ShapeDtypeStruct((bsz, length, n_glu // 2), F32),
            jax.ShapeDtypeStruct((bsz, n_hy, length), F32),
            jax.ShapeDtypeStruct((bsz, length, n_gate), BF16),
        ],
        compiler_params=_cparams("parallel", "parallel"),
        name="inproj",
    )(x, g, b, wglu, whyt, wgate, bgate)


def _conf_kernel(prev_ref, cur_ref, next_ref, w_ref, cb_ref, g_ref, b_ref, o_ref, ext_ref, sh_ref,
                 *, tl, halo, ksize, rows):
    j = pl.program_id(1)
    nj = pl.num_programs(1)
    pad = (ksize - 1) // 2
    ext_ref[0:halo, :] = jnp.where(j > 0, prev_ref[...], 0.0)
    ext_ref[halo:halo + tl, :] = cur_ref[...]
    ext_ref[halo + tl:halo + tl + halo, :] = jnp.where(j < nj - 1, next_ref[...], 0.0)
    n_sh = sh_ref.shape[1]
    for r in range(1, SUBLANES):
        sh_ref[r - 1, :, :] = ext_ref[r:r + n_sh, :]
    w = w_ref[...]
    cb, g, b = cb_ref[...], g_ref[...], b_ref[...]
    for r0 in range(0, tl, rows):
        acc = jnp.zeros((rows, w.shape[1]), F32)
        for k in range(ksize):
            q, r = divmod(halo - pad + k + r0, SUBLANES)
            q *= SUBLANES
            src = ext_ref[q:q + rows, :] if r == 0 else sh_ref[r - 1, q:q + rows, :]
            acc = acc + w[k:k + 1, :] * src
        y = _ln(acc + cb, g, b)
        o_ref[r0:r0 + rows, :] = (y * _sigmoid(y)).astype(o_ref.dtype)


def _conf(u, w, cb, g, b, *, tl, halo=16, rows=64):
    bsz, length, c = u.shape
    ksize = w.shape[0]
    assert (ksize - 1) // 2 <= halo and tl % halo == 0 and tl % rows == 0
    nh = tl // halo
    last = length // halo - 1
    const = lambda i, j: (0, 0)
    return pl.pallas_call(
        functools.partial(_conf_kernel, tl=tl, halo=halo, ksize=ksize, rows=rows),
        grid=(bsz, length // tl),
        in_specs=[
            pl.BlockSpec((None, halo, c), lambda i, j: (i, jnp.maximum(j * nh - 1, 0), 0)),
            pl.BlockSpec((None, tl, c), lambda i, j: (i, j, 0)),
            pl.BlockSpec((None, halo, c), lambda i, j: (i, jnp.minimum((j + 1) * nh, last), 0)),
            pl.BlockSpec((ksize, c), const),
            pl.BlockSpec((1, c), const), pl.BlockSpec((1, c), const), pl.BlockSpec((1, c), const),
        ],
        out_specs=pl.BlockSpec((None, tl, c), lambda i, j: (i, j, 0)),
        out_shape=jax.ShapeDtypeStruct((bsz, length, c), BF16),
        scratch_shapes=[pltpu.VMEM((tl + 2 * halo, c), F32),
                        pltpu.VMEM((SUBLANES - 1, tl + 2 * halo - SUBLANES, c), F32)],
        compiler_params=_cparams("parallel", "parallel"),
        name="conf",
    )(u, u, u, w, cb, g, b)


def _filt_ffn_kernel(feat_ref, w1h_ref, w1l_ref, b1_ref, f1_ref, w2h_ref, w2l_ref, b2_ref, f2_ref, o_ref):
    a = _dot3_lhs(w1h_ref[...], w1l_ref[...], feat_ref[...]) + b1_ref[...]
    h = jnp.sin(f1_ref[...] * a)
    a2 = _dot3_lhs(w2h_ref[...], w2l_ref[...], h) + b2_ref[...]
    o_ref[...] = jnp.sin(f2_ref[...] * a2)


def _filt_ffn(feat_t, w1h, w1l, b1, f1, w2h, w2l, b2, f2, *, tn):
    kp, n = feat_t.shape
    m = w1h.shape[0]
    const = lambda i: (0, 0)
    return pl.pallas_call(
        _filt_ffn_kernel,
        grid=(n // tn,),
        in_specs=[pl.BlockSpec((kp, tn), lambda i: (0, i)),
                  pl.BlockSpec((m, kp), const), pl.BlockSpec((m, kp), const),
                  pl.BlockSpec((m, 1), const), pl.BlockSpec((m, 1), const),
                  pl.BlockSpec((m, m), const), pl.BlockSpec((m, m), const),
                  pl.BlockSpec((m, 1), const), pl.BlockSpec((m, 1), const)],
        out_specs=pl.BlockSpec((m, tn), lambda i: (0, i)),
        out_shape=jax.ShapeDtypeStruct((m, n), F32),
        compiler_params=_cparams("parallel"),
        name="filt_ffn",
    )(feat_t, w1h, w1l, b1, f1, w2h, w2l, b2, f2)


def _filt_kernel(h2_ref, w3fh_ref, w3fl_ref, w3bh_ref, w3bl_ref, delta_ref, text_ref, valid_ref, o_ref):
    n = h2_ref.shape[1]
    half = n // 2
    decay = jnp.exp(-text_ref[...] * delta_ref[...]) * valid_ref[...]
    kf = _dot3_lhs(w3fh_ref[...], w3fl_ref[...], h2_ref[:, 0:half]) * decay[:, 0:half]
    kb = _dot3_lhs(w3bh_ref[...], w3bl_ref[...], h2_ref[:, half:n]) * decay[:, half:n]
    ss = jnp.sum(kf * kf, axis=1, keepdims=True) + jnp.sum(kb * kb, axis=1, keepdims=True)
    scale = lax.rsqrt(ss + 1e-6)
    o_ref[:, 0:half] = kf * scale
    o_ref[:, half:n] = kb * scale


def _filt(h2t, w3fh, w3fl, w3bh, w3bl, delta, text, valid, *, g):
    m, n = h2t.shape
    rows = w3fh.shape[0]
    const = lambda i: (0, 0)
    return pl.pallas_call(
        _filt_kernel,
        grid=(rows // g,),
        in_specs=[pl.BlockSpec((m, n), const),
                  pl.BlockSpec((g, m), lambda i: (i, 0)), pl.BlockSpec((g, m), lambda i: (i, 0)),
                  pl.BlockSpec((g, m), lambda i: (i, 0)), pl.BlockSpec((g, m), lambda i: (i, 0)),
                  pl.BlockSpec((g, 1), lambda i: (i, 0)),
                  pl.BlockSpec((1, n), const), pl.BlockSpec((1, n), const)],
        out_specs=pl.BlockSpec((g, n), lambda i: (i, 0)),
        out_shape=jax.ShapeDtypeStruct((rows, n), F32),
        compiler_params=_cparams("parallel"),
        name="filt",
    )(h2t, w3fh, w3fl, w3bh, w3bl, delta, text, valid)


def _dft_tables(r1, n1_used):
    n = r1 * LANES
    a1 = 2.0 * np.pi * np.outer(np.arange(n1_used), np.arange(r1)) / r1
    a2 = 2.0 * np.pi * np.outer(np.arange(LANES), np.arange(LANES)) / LANES
    at = 2.0 * np.pi * np.outer(np.arange(LANES), np.arange(r1)) / n
    return n, a1, a2, at


def _cplx_rhs(ang, sign):
    c, s = np.cos(ang), sign * np.sin(ang)
    return np.block([[c, s], [-s, c]])


def _fwd_steps(x_t, w1, twr, twi, w2):
    g, n2, k1w = x_t.shape
    r1 = twr.shape[1]
    a = _dot(x_t.reshape(g * n2, k1w).astype(BF16), w1).reshape(g, n2, 2 * r1)
    ar, ai = a[:, :, :r1], a[:, :, r1:]
    br = ar * twr - ai * twi
    bi = ar * twi + ai * twr
    bt = jnp.concatenate([jnp.swapaxes(br, 1, 2), jnp.swapaxes(bi, 1, 2)], axis=-1)
    c = _dot(bt.reshape(g * r1, 2 * n2).astype(BF16), w2)
    return c.reshape(g, r1, 2 * n2)


def _inv_steps(d, w3, twr_t, twi_t, w4):
    g, r1, w = d.shape
    n2 = w // 2
    e = _dot(d.reshape(g * r1, w).astype(BF16), w3).reshape(g, r1, w)
    er, ei = e[:, :, :n2], e[:, :, n2:]
    fr = er * twr_t + ei * twi_t
    fi = ei * twr_t - er * twi_t
    ft = jnp.concatenate([jnp.swapaxes(fr, 1, 2), jnp.swapaxes(fi, 1, 2)], axis=-1)
    y = _dot(ft.reshape(g * n2, 2 * r1).astype(BF16), w4)
    return y.reshape(g, n2, w4.shape[1])


def _kfft_kernel(k_ref, w1_ref, twr_ref, twi_ref, w2_ref, o_ref):
    x_t = jnp.swapaxes(k_ref[...], 1, 2)
    o_ref[...] = _fwd_steps(x_t, w1_ref[...], twr_ref[...], twi_ref[...], w2_ref[...])


def _kfft(k3, w1, twr, twi, w2, *, g):
    rows, r1, lanes = k3.shape
    const = lambda i: (0, 0)
    return pl.pallas_call(
        _kfft_kernel,
        grid=(rows // g,),
        in_specs=[pl.BlockSpec((g, r1, lanes), lambda i: (i, 0, 0)),
                  pl.BlockSpec(w1.shape, const), pl.BlockSpec(twr.shape, const),
                  pl.BlockSpec(twi.shape, const), pl.BlockSpec(w2.shape, const)],
        out_specs=pl.BlockSpec((g, r1, 2 * lanes), lambda i: (i, 0, 0)),
        out_shape=jax.ShapeDtypeStruct((rows, r1, 2 * lanes), F32),
        compiler_params=_cparams("parallel"),
        name="kfft",
    )(k3, w1, twr, twi, w2)


def _short_conv(a, w, b):
    p, g, r, l = a.shape
    a3 = a.reshape(p * g, r, l)
    lane = lax.broadcasted_iota(I32, a3.shape, 2)
    row = lax.broadcasted_iota(I32, a3.shape, 1)
    pl_ = pltpu.roll(a3, 1, 2)
    ql = pltpu.roll(pl_, 1, 1)
    prev = jnp.where(lane == 0, jnp.where(row == 0, 0.0, ql), pl_)
    pr = pltpu.roll(a3, l - 1, 2)
    qr = pltpu.roll(pr, r - 1, 1)
    nxt = jnp.where(lane == l - 1, jnp.where(row == r - 1, 0.0, qr), pr)
    prev, nxt = prev.reshape(a.shape), nxt.reshape(a.shape)
    return w[0][None] * prev + w[1][None] * a + w[2][None] * nxt + b[None]


def _hyena_kernel(x1_ref, x2_ref, v_ref, sw1_ref, sb1_ref, sw2_ref, sb2_ref, swv_ref, sbv_ref,
                  kf0_ref, kf1_ref, skip_ref, w1_ref, twr_ref, twi_ref, w2_ref, w3_ref,
                  twrt_ref, twit_ref, w4_ref, z_ref):
    x1 = _short_conv(x1_ref[...], sw1_ref[...], sb1_ref[...])
    x2 = _short_conv(x2_ref[...], sw2_ref[...], sb2_ref[...])
    z = _short_conv(v_ref[...], swv_ref[...], sbv_ref[...])
    rh = z.shape[2]
    for gate, kf_ref, o in ((x1, kf0_ref, 0), (x2, kf1_ref, 1)):
        s = jnp.concatenate([z[0], z[1]], axis=1)
        c = _fwd_steps(jnp.swapaxes(s, 1, 2), w1_ref[...], twr_ref[...], twi_ref[...], w2_ref[...])
        kf = kf_ref[...]
        n2 = c.shape[2] // 2
        cr, ci, kr, ki = c[:, :, :n2], c[:, :, n2:], kf[:, :, :n2], kf[:, :, n2:]
        d = jnp.concatenate([cr * kr - ci * ki, cr * ki + ci * kr], axis=-1)
        y = _inv_steps(d, w3_ref[...], twrt_ref[...], twit_ref[...], w4_ref[...])
        y = jnp.swapaxes(y, 1, 2)
        conv = jnp.stack([y[:, :rh], y[:, rh:]], axis=0)
        z = gate * (conv + skip_ref[o][None] * z)
    z_ref[...] = z


def _hyena(hy4, sw, sb, kf, skip, tabs, *, g):
    bsz, c3, rh, lanes = hy4.shape
    c = c3 // 3
    r1 = 2 * rh
    ncb = c // g
    w1, twr, twi, w2, w3, twrt, twit, w4 = tabs
    const2 = lambda i, p: (0, 0)
    data = lambda off: pl.BlockSpec((2, g, rh, lanes), lambda i, p, off=off: (p, off + i, 0, 0))
    wspec = lambda off: pl.BlockSpec((3, g, 1, lanes), lambda i, p, off=off: (0, off + i, 0, 0))
    bspec = lambda off: pl.BlockSpec((g, 1, lanes), lambda i, p, off=off: (off + i, 0, 0))
    kspec = lambda off: pl.BlockSpec((g, r1, 2 * lanes), lambda i, p, off=off: (off + i, 0, 0))
    return pl.pallas_call(
        _hyena_kernel,
        grid=(ncb, bsz // 2),
        in_specs=[data(0), data(ncb), data(2 * ncb),
                  wspec(0), bspec(0), wspec(ncb), bspec(ncb), wspec(2 * ncb), bspec(2 * ncb),
                  kspec(0), kspec(ncb),
                  pl.BlockSpec((2, g, 1, lanes), lambda i, p: (0, i, 0, 0)),
                  pl.BlockSpec(w1.shape, const2), pl.BlockSpec(twr.shape, const2),
                  pl.BlockSpec(twi.shape, const2), pl.BlockSpec(w2.shape, const2),
                  pl.BlockSpec(w3.shape, const2), pl.BlockSpec(twrt.shape, const2),
                  pl.BlockSpec(twit.shape, const2), pl.BlockSpec(w4.shape, const2)],
        out_specs=pl.BlockSpec((2, g, rh, lanes), lambda i, p: (p, i, 0, 0)),
        out_shape=jax.ShapeDtypeStruct((bsz, c, rh, lanes), F32),
        compiler_params=_cparams("parallel", "parallel"),
        name="hyena",
    )(hy4, hy4, hy4, sw, sb, sw, sb, sw, sb, kf, kf, skip, w1, twr, twi, w2, w3, twrt, twit, w4)


def _mix_kernel(ua_ref, zt_ref, gate_ref, h_ref, wa_ref, wb_ref, wm_ref, g_ref, b_ref, o_ref, *, alpha, sub):
    for r in range(0, h_ref.shape[0], sub):
        rs = slice(r, r + sub)
        ya = _dot(ua_ref[rs, :], wa_ref[...])
        zt = jnp.transpose(zt_ref[:, rs]).astype(BF16)
        yb = _dot(zt, wb_ref[...])
        gt = gate_ref[rs, :].astype(F32)
        d = ya.shape[1]
        m = gt[:, :d] * ya + gt[:, d:] * yb
        mixed = _dot(m.astype(BF16), wm_ref[...])
        o_ref[rs, :] = _ln(alpha * h_ref[rs, :] + mixed, g_ref[...], b_ref[...])


def _mix(ua, zt, gates, h, wa, wb, wm, g, b, *, tm, sub, alpha):
    bsz, length, d = h.shape
    c = ua.shape[2]
    const = lambda i, j: (0, 0)
    return pl.pallas_call(
        functools.partial(_mix_kernel, alpha=alpha, sub=min(sub, tm)),
        grid=(bsz, length // tm),
        in_specs=[pl.BlockSpec((None, tm, c), lambda i, j: (i, j, 0)),
                  pl.BlockSpec((None, c, tm), lambda i, j: (i, 0, j)),
                  pl.BlockSpec((None, tm, 2 * d), lambda i, j: (i, j, 0)),
                  pl.BlockSpec((None, tm, d), lambda i, j: (i, j, 0)),
                  pl.BlockSpec((c, d), const), pl.BlockSpec((c, d), const), pl.BlockSpec((d, d), const),
                  pl.BlockSpec((1, d), const), pl.BlockSpec((1, d), const)],
        out_specs=pl.BlockSpec((None, tm, d), lambda i, j: (i, j, 0)),
        out_shape=jax.ShapeDtypeStruct((bsz, length, d), F32),
        compiler_params=_cparams("parallel", "parallel"),
        name="mix",
    )(ua, zt, gates, h, wa, wb, wm, g, b)


def _kv_kernel(mem_ref, wk_ref, wv_ref, k_ref, v_ref):
    mb = mem_ref[...].astype(BF16)
    k_ref[...] = _dot(mb, wk_ref[...]).astype(BF16)
    v_ref[...] = _dot(mb, wv_ref[...]).astype(BF16)


def _kv(mem, wk, wv):
    bsz, m, d = mem.shape
    const = lambda i: (0, 0)
    blk = pl.BlockSpec((None, m, d), lambda i: (i, 0, 0))
    return pl.pallas_call(
        _kv_kernel,
        grid=(bsz,),
        in_specs=[blk, pl.BlockSpec((d, d), const), pl.BlockSpec((d, d), const)],
        out_specs=[blk, blk],
        out_shape=[jax.ShapeDtypeStruct((bsz, m, d), BF16)] * 2,
        compiler_params=_cparams("parallel"),
        name="kv",
    )(mem, wk, wv)


def _xattn_kernel(x_ref, k_ref, v_ref, wq_ref, wo_ref, g_ref, b_ref, wrh_ref, wrl_ref,
                  x2_ref, x2b_ref, aff_ref, *, alpha, heads, sub):
    d = x_ref.shape[1]
    dh = d // heads
    wrh, wrl = wrh_ref[...], wrl_ref[...]
    for r in range(0, x_ref.shape[0], sub):
        rs = slice(r, r + sub)
        x = x_ref[rs, :]
        q = (_dot(x.astype(BF16), wq_ref[...]) * (dh ** -0.5)).astype(BF16)
        outs = []
        for hd in range(heads):
            sl = slice(hd * dh, (hd + 1) * dh)
            s = _dot_nt(q[:, sl], k_ref[:, sl])
            s = s - jnp.max(s, axis=-1, keepdims=True)
            p = jnp.exp(s)
            p = p / jnp.sum(p, axis=-1, keepdims=True)
            outs.append(_dot(p.astype(BF16), v_ref[:, sl]))
        o = jnp.concatenate(outs, axis=-1)
        xa = _dot(o.astype(BF16), wo_ref[...])
        x2 = _ln(alpha * x + xa, g_ref[...], b_ref[...])
        x2_ref[rs, :] = x2
        x2h, x2l = _split_bf16(x2)
        x2b_ref[rs, :] = x2h
        logits = _dot_nt(wrh, x2h) + (_dot_nt(wrh, x2l) + _dot_nt(wrl, x2h))
        logits = logits - jnp.max(logits, axis=0, keepdims=True)
        ex = jnp.exp(logits)
        aff_ref[:, rs] = ex / jnp.sum(ex, axis=0, keepdims=True)


def _xattn(x, k, v, wq, wo, g, b, wrh, wrl, *, tm, sub, alpha, heads):
    bsz, length, d = x.shape
    m = k.shape[1]
    e = wrh.shape[0]
    const = lambda i, j: (0, 0)
    tok = pl.BlockSpec((None, tm, d), lambda i, j: (i, j, 0))
    mem = pl.BlockSpec((None, m, d), lambda i, j: (i, 0, 0))
    return pl.pallas_call(
        functools.partial(_xattn_kernel, alpha=alpha, heads=heads, sub=min(sub, tm)),
        grid=(bsz, length // tm),
        in_specs=[tok, mem, mem, pl.BlockSpec((d, d), const), pl.BlockSpec((d, d), const),
                  pl.BlockSpec((1, d), const), pl.BlockSpec((1, d), const),
                  pl.BlockSpec((e, d), const), pl.BlockSpec((e, d), const)],
        out_specs=[tok, tok, pl.BlockSpec((None, e, tm), lambda i, j: (i, 0, j))],
        out_shape=[jax.ShapeDtypeStruct((bsz, length, d), F32),
                   jax.ShapeDtypeStruct((bsz, length, d), BF16),
                   jax.ShapeDtypeStruct((bsz, e, length), F32)],
        compiler_params=_cparams("parallel", "parallel"),
        name="xattn",
    )(x, k, v, wq, wo, g, b, wrh, wrl)


def _excl_cumsum_lanes(mask_f32, tri, write):
    e, t = mask_f32.shape
    carry = jnp.zeros((e, 1), F32)
    for c in range(t // LANES):
        m = mask_f32[:, c * LANES:(c + 1) * LANES]
        inc = _dot(m.astype(BF16), tri)
        write(c, inc - m + carry)
        carry = carry + inc[:, LANES - 1:LANES]


def _select_kernel(aff_ref, tri_ref, slot_ref, gate_ref, rank_ref, cnt_ref, *, cap):
    a = aff_ref[...]
    e = a.shape[0]
    v = jnp.zeros((e, 1), I32)
    for bit in range(30, -1, -1):
        cand = v | (1 << bit)
        cnt = jnp.sum(jnp.where(a >= pltpu.bitcast(cand, F32), 1.0, 0.0), axis=1, keepdims=True)
        v = jnp.where(cnt >= cap, cand, v)
    thr = pltpu.bitcast(v, F32)
    gt = a > thr
    eq = a == thr
    need = cap - jnp.sum(jnp.where(gt, 1.0, 0.0), axis=1, keepdims=True)
    tri = tri_ref[...]

    def write_eq(c, val):
        cnt_ref[:, c * LANES:(c + 1) * LANES] = val
    _excl_cumsum_lanes(jnp.where(eq, 1.0, 0.0), tri, write_eq)
    sel = gt | (eq & (cnt_ref[...] < need))
    gate_ref[...] = jnp.where(sel, a, 0.0)

    def write_rank(c, val):
        cnt_ref[:, c * LANES:(c + 1) * LANES] = val
    _excl_cumsum_lanes(jnp.where(sel, 1.0, 0.0), tri, write_rank)
    rank = cnt_ref[...].astype(I32)
    rank_ref[...] = rank
    slot_ref[...] = jnp.where(sel, rank, -1)


def _select(aff, tri, *, cap):
    bsz, e, t = aff.shape
    blk = pl.BlockSpec((None, e, t), lambda i: (i, 0, 0))
    return pl.pallas_call(
        functools.partial(_select_kernel, cap=cap),
        grid=(bsz,),
        in_specs=[blk, pl.BlockSpec((LANES, LANES), lambda i: (0, 0))],
        out_specs=[blk, blk, blk],
        out_shape=[jax.ShapeDtypeStruct((bsz, e, t), I32), jax.ShapeDtypeStruct((bsz, e, t), F32),
                   jax.ShapeDtypeStruct((bsz, e, t), I32)],
        scratch_shapes=[pltpu.VMEM((e, t), F32)],
        compiler_params=_cparams("parallel"),
        name="select",
    )(aff, tri)


def _window_start(r0, cap, win, align):
    return pl.multiple_of(jnp.minimum((r0 // align) * align, cap - win), align)


def _dispatch_kernel(jlo_ref, nch_ref, x_ref, slot_ref, ghl_ref, o_ref, gs_ref, acc_ref, gacc_ref,
                     *, st, kc, tt):
    b, e = pl.program_id(0), pl.program_id(1)
    t_total = x_ref.shape[0]
    n_e = ghl_ref.shape[1] // 2
    for s in range(o_ref.shape[0] // st):
        j0 = jlo_ref[b, e, s]

        def chunk(c, s=s, j0=j0):
            want = (j0 + c * (kc // tt)) * tt
            t0 = pl.multiple_of(jnp.minimum(want, t_total - kc), tt)
            tok = t0 + lax.broadcasted_iota(I32, (1, kc), 1)
            rel = jnp.where(tok >= want, slot_ref[:, pl.ds(t0, kc)] - s * st, -1)
            onehot = jnp.where(lax.broadcasted_iota(I32, (st, kc), 0) == rel, 1.0, 0.0).astype(BF16)
            xs = _dot(onehot, x_ref[pl.ds(t0, kc), :])
            gs = _dot(onehot, ghl_ref[pl.ds(t0, kc), :])
            return xs, gs

        xs, gs = chunk(0)
        acc_ref[...] = xs
        gacc_ref[...] = gs

        def body(c, carry):
            xs, gs = chunk(c)
            acc_ref[...] += xs
            gacc_ref[...] += gs
            return carry
        lax.fori_loop(1, nch_ref[b, e, s], body, 0)
        o_ref[s * st:(s + 1) * st, :] = acc_ref[...].astype(o_ref.dtype)
        lane = lax.broadcasted_iota(I32, (st, 2 * n_e), 1)
        mine = (lane == e) | (lane == e + n_e)
        gs_ref[s * st:(s + 1) * st, :] = jnp.sum(jnp.where(mine, gacc_ref[...], 0.0), axis=1, keepdims=True)


def _dispatch(jlo, nch, xb, slot4, gate_hl, *, cap, st, kc, tt):
    bsz, t, d = xb.shape
    e = slot4.shape[1]
    assert cap % st == 0 and kc % tt == 0 and t % tt == 0 and kc <= t
    return pl.pallas_call(
        functools.partial(_dispatch_kernel, st=st, kc=kc, tt=tt),
        grid_spec=pltpu.PrefetchScalarGridSpec(
            num_scalar_prefetch=2,
            grid=(bsz, e),
            in_specs=[pl.BlockSpec((None, t, d), lambda i, j, a, c: (i, 0, 0)),
                      pl.BlockSpec((None, None, 1, t), lambda i, j, a, c: (i, j, 0, 0)),
                      pl.BlockSpec((None, t, 2 * e), lambda i, j, a, c: (i, 0, 0))],
            out_specs=[pl.BlockSpec((None, None, cap, d), lambda i, j, a, c: (i, j, 0, 0)),
                       pl.BlockSpec((None, None, cap, 1), lambda i, j, a, c: (i, j, 0, 0))],
            scratch_shapes=[pltpu.VMEM((st, d), F32), pltpu.VMEM((st, 2 * e), F32)]),
        out_shape=[jax.ShapeDtypeStruct((bsz, e, cap, d), BF16),
                   jax.ShapeDtypeStruct((bsz, e, cap, 1), F32)],
        compiler_params=_cparams("parallel", "parallel"),
        name="dispatch",
    )(jlo, nch, xb, slot4, gate_hl)


def _expert_kernel(x_ref, gs_ref, wg_ref, wu_ref, wd_ref, o_ref, acc_ref):
    f = pl.program_id(2)
    wg = wg_ref[...].astype(BF16)
    wu = wu_ref[...].astype(BF16)
    wd = wd_ref[...].astype(BF16)
    @pl.when(f == 0)
    def _():
        acc_ref[...] = jnp.zeros_like(acc_ref)

    for bi in range(x_ref.shape[0]):
        x = x_ref[bi]
        hg = _dot(x, wg)
        hu = _dot(x, wu)
        act = (hg * _sigmoid(hg) * hu).astype(BF16)
        acc_ref[bi] += _dot(act, wd)

    @pl.when(f == pl.num_programs(2) - 1)
    def _():
        o_ref[...] = (acc_ref[...] * gs_ref[...]).astype(o_ref.dtype)


def _expert(xg, gs, wg, wu, wd, *, fc, nb):
    bsz, e, cap, d = xg.shape
    f = wg.shape[2]
    assert bsz % nb == 0 and f % fc == 0
    tok = lambda last: pl.BlockSpec((nb, None, cap, last), lambda i, j, k: (j, i, 0, 0))
    return pl.pallas_call(
        _expert_kernel,
        grid=(e, bsz // nb, f // fc),
        in_specs=[tok(d), tok(1),
                  pl.BlockSpec((None, d, fc), lambda i, j, k: (i, 0, k)),
                  pl.BlockSpec((None, d, fc), lambda i, j, k: (i, 0, k)),
                  pl.BlockSpec((None, fc, d), lambda i, j, k: (i, k, 0))],
        out_specs=tok(d),
        out_shape=jax.ShapeDtypeStruct((bsz, e, cap, d), BF16),
        scratch_shapes=[pltpu.VMEM((nb, cap, d), F32)],
        compiler_params=_cparams("parallel", "parallel", "arbitrary"),
        name="expert",
    )(xg, gs, wg, wu, wd)


def _combine_kernel(r0_ref, x_ref, slot_ref, g_ref, b_ref, y_ref, o_ref, *, tt, win, align, alpha):
    b, j = pl.program_id(0), pl.program_id(1)
    n_e, cap = y_ref.shape[0], y_ref.shape[1]
    acc = alpha * x_ref[...]
    slots = slot_ref[...]
    for e in range(n_e):
        start = _window_start(r0_ref[b, e, j], cap, win, align)
        rel = slots[:, e:e + 1] - start
        onehot = jnp.where(lax.broadcasted_iota(I32, (tt, win), 1) == rel, 1.0, 0.0).astype(BF16)
        acc = acc + _dot(onehot, y_ref[e, pl.ds(start, win), :])
    o_ref[...] = _ln(acc, g_ref[...], b_ref[...])


def _combine(r0, x, slot_t, g, b, y, *, tt, win, align, alpha):
    bsz, t, d = x.shape
    e, cap = y.shape[1], y.shape[2]
    assert win >= tt + align and win <= cap and (cap - win) % align == 0
    const = lambda i, j, r: (0, 0)
    return pl.pallas_call(
        functools.partial(_combine_kernel, tt=tt, win=win, align=align, alpha=alpha),
        grid_spec=pltpu.PrefetchScalarGridSpec(
            num_scalar_prefetch=1,
            grid=(bsz, t // tt),
            in_specs=[pl.BlockSpec((None, tt, d), lambda i, j, r: (i, j, 0)),
                      pl.BlockSpec((None, tt, e), lambda i, j, r: (i, j, 0)),
                      pl.BlockSpec((1, d), const), pl.BlockSpec((1, d), const),
                      pl.BlockSpec((None, e, cap, d), lambda i, j, r: (i, 0, 0, 0),
                                   pipeline_mode=pl.Buffered(1))],
            out_specs=pl.BlockSpec((None, tt, d), lambda i, j, r: (i, j, 0))),
        out_shape=jax.ShapeDtypeStruct((bsz, t, d), F32),
        compiler_params=_cparams("parallel", "parallel"),
        name="combine",
    )(r0, x, slot_t, g, b, y)


def _hyena_constants(length):
    n = 2 * length
    m = np.arange(n)
    pos = np.where(m < length, m, n - m).astype(np.float64)
    pos[length] = 0.0
    t = pos / max(length - 1, 1)
    f = np.linspace(1e-4, HY_BANDS - 1, HY_BANDS)
    ang = (2.0 * np.pi * pos / length)[:, None] * f[None, :]
    feat = np.concatenate([t[:, None], np.cos(ang), -np.sin(ang)], axis=-1)
    feat_t = np.zeros((LANES, n), np.float32)
    feat_t[:feat.shape[1]] = feat.T
    valid = np.ones((1, n), np.float32)
    valid[0, length] = 0.0
    return feat_t, t[None, :].astype(np.float32), valid


def _decay_rates(width):
    max_decay = math.log(HY_DECAY_TARGET) / HY_FAST_DECAY
    min_decay = math.log(HY_DECAY_TARGET) / HY_SLOW_DECAY
    return np.abs(np.linspace(min_decay, max_decay, width, dtype=np.float32)).astype(np.float32)


def _dft_constants(r1):
    n, a1h, a2, at = _dft_tables(r1, r1 // 2)
    _, a1f, _, _ = _dft_tables(r1, r1)
    bf = lambda x: jnp.asarray(x, F32).astype(BF16)
    w1_data = bf(_cplx_rhs(a1h, -1.0))
    w1_filt = bf(np.concatenate([np.cos(a1f), -np.sin(a1f)], axis=1))
    w2 = bf(_cplx_rhs(a2, -1.0))
    w3 = bf(_cplx_rhs(a2.T, 1.0))
    w4 = bf(_cplx_rhs(a1h.T, 1.0) / n)
    twr, twi = np.cos(at), -np.sin(at)
    f32 = lambda x: jnp.asarray(x, F32)
    return dict(w1_data=w1_data, w1_filt=w1_filt, w2=w2, w3=w3, w4=w4,
                twr=f32(twr), twi=f32(twi), twr_t=f32(twr.T), twi_t=f32(twi.T))


def kernel(x, mem, ln_in_g, ln_in_b, w_in, b_gate, conf_dw_w, conf_dw_b, conf_ln_g, conf_ln_b, conf_w_out, hy_short_w, hy_short_b, hy_ffn_w1, hy_ffn_b1, hy_freq1, hy_ffn_w2, hy_ffn_b2, hy_freq2, hy_ffn_w3, hy_skip, hy_w_out, w_mix_out, ln_mix_g, ln_mix_b, xa_wq, xa_wk, xa_wv, xa_wo, ln_xa_g, ln_xa_b, moe_w_router, moe_w_gate, moe_w_up, moe_w_down, ln_moe_g, ln_moe_b):
    bsz, length, d = x.shape
    depth = w_in.shape[0]
    alpha = (2.0 * depth) ** 0.25
    cw = conf_dw_w.shape[2]
    hw = hy_skip.shape[2]
    n_glu, n_hy = 2 * cw, 3 * hw
    n_exp = moe_w_router.shape[2]
    cap = max(1, N_EXPERT_CAPACITY * length // n_exp)
    rh = length // LANES
    r1 = 2 * rh
    assert bsz % 2 == 0 and length % LANES == 0

    row = lambda v: v.reshape(1, -1).astype(F32)
    col = lambda v: v.reshape(-1, 1).astype(F32)

    feat_t, t_ext, valid = _hyena_constants(length)
    dft = _dft_constants(r1)
    rates = _decay_rates(hw)
    tri = jnp.asarray(np.triu(np.ones((LANES, LANES), np.float32))).astype(BF16)

    n_ord = hy_skip.shape[1]
    tm = min(512, length)
    lane_b = lambda v: jnp.broadcast_to(v[..., None, None], v.shape + (1, LANES)).astype(F32)
    dft_data = (dft["w1_data"], dft["twr"], dft["twi"], dft["w2"], dft["w3"], dft["twr_t"], dft["twi_t"],
                dft["w4"])

    h = x
    for i in range(depth):
        w1h, w1l = _split_bf16(jnp.pad(hy_ffn_w1[i].T, ((0, 0), (0, LANES - hy_ffn_w1.shape[1]))))
        w2h, w2l = _split_bf16(hy_ffn_w2[i].T)
        h2t = _filt_ffn(jnp.asarray(feat_t), w1h, w1l, col(hy_ffn_b1[i]), col(hy_freq1[i]),
                        w2h, w2l, col(hy_ffn_b2[i]), col(hy_freq2[i]), tn=min(2048, 2 * length))
        w3 = hy_ffn_w3[i].reshape(-1, n_ord, 2, hw)
        w3f = jnp.transpose(w3[:, :, 0], (1, 2, 0)).reshape(n_ord * hw, -1)
        w3b = jnp.transpose(w3[:, :, 1], (1, 2, 0)).reshape(n_ord * hw, -1)
        w3fh, w3fl = _split_bf16(w3f)
        w3bh, w3bl = _split_bf16(w3b)
        delta = jnp.asarray(np.tile(rates, n_ord).reshape(-1, 1))
        kern = _filt(h2t, w3fh, w3fl, w3bh, w3bl, delta, jnp.asarray(t_ext), jnp.asarray(valid),
                     g=min(128, hw))
        kf = _kfft(kern.reshape(n_ord * hw, r1, LANES), dft["w1_filt"], dft["twr"], dft["twi"], dft["w2"],
                   g=min(16, hw))

        wi = w_in[i]
        wglu = wi[:, :n_glu].astype(BF16)
        whyt = wi[:, n_glu:n_glu + n_hy].T.astype(BF16)
        wgate = wi[:, n_glu + n_hy:].astype(BF16)
        hn, u, hyt, gates = _inproj(h, row(ln_in_g), row(ln_in_b), wglu, whyt, wgate, row(b_gate[i]),
                                    tm=tm, apply_ln=(i == 0))

        ua = _conf(u, conf_dw_w[i], row(conf_dw_b[i]), row(conf_ln_g[i]), row(conf_ln_b[i]),
                   tl=min(512, length))

        z = _hyena(hyt.reshape(bsz, n_hy, rh, LANES), lane_b(hy_short_w[i]), lane_b(hy_short_b[i]), kf,
                   lane_b(hy_skip[i]), dft_data, g=min(16, hw))

        x1 = _mix(ua, z.reshape(bsz, hw, length), gates, hn, conf_w_out[i].astype(BF16),
                  hy_w_out[i].astype(BF16), w_mix_out[i].astype(BF16), row(ln_mix_g[i]), row(ln_mix_b[i]),
                  tm=min(1024, length), sub=512, alpha=alpha)

        k, v = _kv(mem, xa_wk[i].astype(BF16), xa_wv[i].astype(BF16))
        wrh, wrl = _split_bf16(moe_w_router[i].T)
        x2, x2b, aff = _xattn(x1, k, v, xa_wq[i].astype(BF16), xa_wo[i].astype(BF16), row(ln_xa_g[i]),
                              row(ln_xa_b[i]), wrh, wrl, tm=min(1024, length), sub=512, alpha=alpha,
                              heads=XA_HEADS)

        slot, gate, rank = _select(aff, tri, cap=cap)
        tt, st, align = LANES, LANES, 16
        kc = min(12 * tt, length)
        r0 = rank[:, :, ::tt]
        cum = jnp.concatenate([r0, jnp.full((bsz, n_exp, 1), cap, I32)], axis=-1)
        s_lo = (jnp.arange(cap // st, dtype=I32) * st)[None, None, :, None]
        jlo = jnp.sum((cum[:, :, None, 1:] <= s_lo).astype(I32), axis=-1)
        jhi = jnp.sum((cum[:, :, None, :-1] < s_lo + st).astype(I32), axis=-1) - 1
        nch = (jhi - jlo + kc // tt) // (kc // tt)
        gate_hl = jnp.concatenate(_split_bf16(jnp.swapaxes(gate, 1, 2)), axis=-1)
        xg, gslot = _dispatch(jlo, nch, x2b, slot.reshape(bsz, n_exp, 1, length), gate_hl,
                              cap=cap, st=st, kc=kc, tt=tt)
        y = _expert(xg, gslot, moe_w_gate[i], moe_w_up[i], moe_w_down[i],
                    fc=min(512, moe_w_gate.shape[3]), nb=2)
        h = _combine(r0, x2, jnp.swapaxes(slot, 1, 2), row(ln_moe_g[i]), row(ln_moe_b[i]), y,
                     tt=tt, win=tt + align, align=align, alpha=alpha)
    return h
```

```python
import functools
import math

import numpy as np
import jax
import jax.numpy as jnp
from jax import lax
from jax.experimental import pallas as pl
from jax.experimental.pallas import tpu as pltpu

F32 = jnp.float32
BF16 = jnp.bfloat16
I32 = jnp.int32

LANES = 128
SUBLANES = 8
VMEM_LIMIT_BYTES = 56 * 1024 * 1024
SUB_ROWS = 256

LN_EPS = 1e-5
XA_HEADS = 4
N_EXPERT_CAPACITY = 2
HY_DECAY_TARGET = 1e-2
HY_FAST_DECAY = 0.3
HY_SLOW_DECAY = 1.5
HY_BANDS = 16


def _cparams(*sem):
    return pltpu.CompilerParams(dimension_semantics=sem, vmem_limit_bytes=VMEM_LIMIT_BYTES)


def _ln(x, g, b):
    mu = jnp.mean(x, axis=-1, keepdims=True)
    xc = x - mu
    var = jnp.mean(xc * xc, axis=-1, keepdims=True)
    return xc * lax.rsqrt(var + LN_EPS) * g + b


def _sigmoid(x):
    return 1.0 / (1.0 + jnp.exp(-x))


def _dot(a, b):
    return jnp.dot(a, b, preferred_element_type=F32)


def _dot_nt(a, b):
    return lax.dot_general(a, b, (((1,), (1,)), ((), ())), preferred_element_type=F32)


def _split_bf16(x):
    hi = x.astype(BF16)
    lo = (x - hi.astype(F32)).astype(BF16)
    return hi, lo


def _dot3(a, b_hi, b_lo):
    a_hi, a_lo = _split_bf16(a)
    return _dot(a_hi, b_hi) + (_dot(a_hi, b_lo) + _dot(a_lo, b_hi))


def _dot3_lhs(a_hi, a_lo, b):
    b_hi, b_lo = _split_bf16(b)
    return _dot(a_hi, b_hi) + (_dot(a_hi, b_lo) + _dot(a_lo, b_hi))


def _inproj_kernel(x_ref, g_ref, b_ref, wglu_ref, whyt_ref, wgate_ref, bgate_ref,
                   h_ref, u_ref, hyt_ref, gate_ref, *, apply_ln, sub):
    for r in range(0, x_ref.shape[0], sub):
        rs = slice(r, r + sub)
        h = _ln(x_ref[rs, :], g_ref[...], b_ref[...]) if apply_ln else x_ref[rs, :]
        h_ref[rs, :] = h
        hb = h.astype(BF16)
        glu = _dot(hb, wglu_ref[...])
        cw = glu.shape[1] // 2
        u_ref[rs, :] = glu[:, :cw] * _sigmoid(glu[:, cw:])
        hyt_ref[:, rs] = _dot_nt(whyt_ref[...], hb).astype(hyt_ref.dtype)
        gl = _dot(hb, wgate_ref[...]) + bgate_ref[...]
        gate_ref[rs, :] = _sigmoid(gl).astype(BF16)


def _inproj(x, g, b, wglu, whyt, wgate, bgate, *, tm, apply_ln):
    bsz, length, d = x.shape
    n_glu, n_hy, n_gate = wglu.shape[1], whyt.shape[0], wgate.shape[1]
    const = lambda i, j: (0, 0)
    return pl.pallas_call(
        functools.partial(_inproj_kernel, apply_ln=apply_ln, sub=min(SUB_ROWS, tm)),
        grid=(bsz, length // tm),
        in_specs=[
            pl.BlockSpec((None, tm, d), lambda i, j: (i, j, 0)),
            pl.BlockSpec((1, d), const), pl.BlockSpec((1, d), const),
            pl.BlockSpec((d, n_glu), const),
            pl.BlockSpec((n_hy, d), const),
            pl.BlockSpec((d, n_gate), const),
            pl.BlockSpec((1, n_gate), const),
        ],
        out_specs=[
            pl.BlockSpec((None, tm, d), lambda i, j: (i, j, 0)),
            pl.BlockSpec((None, tm, n_glu // 2), lambda i, j: (i, j, 0)),
            pl.BlockSpec((None, n_hy, tm), lambda i, j: (i, 0, j)),
            pl.BlockSpec((None, tm, n_gate), lambda i, j: (i, j, 0)),
        ],
        out_shape=[
            jax.ShapeDtypeStruct((bsz, length, d), F32),
            jax.ShapeDtypeStruct((bsz, length, n_glu // 2), F32),
            jax.ShapeDtypeStruct((bsz, n_hy, length), BF16),
            jax.ShapeDtypeStruct((bsz, length, n_gate), BF16),
        ],
        compiler_params=_cparams("parallel", "parallel"),
        name="inproj",
    )(x, g, b, wglu, whyt, wgate, bgate)


def _conf_kernel(prev_ref, cur_ref, next_ref, w_ref, cb_ref, g_ref, b_ref, o_ref, ext_ref, sh_ref,
                 *, tl, halo, ksize, rows):
    j = pl.program_id(1)
    nj = pl.num_programs(1)
    pad = (ksize - 1) // 2
    ext_ref[0:halo, :] = jnp.where(j > 0, prev_ref[...], 0.0)
    ext_ref[halo:halo + tl, :] = cur_ref[...]
    ext_ref[halo + tl:halo + tl + halo, :] = jnp.where(j < nj - 1, next_ref[...], 0.0)
    n_sh = sh_ref.shape[1]
    for r in range(1, SUBLANES):
        sh_ref[r - 1, :, :] = ext_ref[r:r + n_sh, :]
    w = w_ref[...]
    cb, g, b = cb_ref[...], g_ref[...], b_ref[...]
    for r0 in range(0, tl, rows):
        acc = jnp.zeros((rows, w.shape[1]), F32)
        for k in range(ksize):
            q, r = divmod(halo - pad + k + r0, SUBLANES)
            q *= SUBLANES
            src = ext_ref[q:q + rows, :] if r == 0 else sh_ref[r - 1, q:q + rows, :]
            acc = acc + w[k:k + 1, :] * src
        y = _ln(acc + cb, g, b)
        o_ref[r0:r0 + rows, :] = (y * _sigmoid(y)).astype(o_ref.dtype)


def _conf(u, w, cb, g, b, *, tl, halo=16, rows=64):
    bsz, length, c = u.shape
    ksize = w.shape[0]
    assert (ksize - 1) // 2 <= halo and tl % halo == 0 and tl % rows == 0
    nh = tl // halo
    last = length // halo - 1
    const = lambda i, j: (0, 0)
    return pl.pallas_call(
        functools.partial(_conf_kernel, tl=tl, halo=halo, ksize=ksize, rows=rows),
        grid=(bsz, length // tl),
        in_specs=[
            pl.BlockSpec((None, halo, c), lambda i, j: (i, jnp.maximum(j * nh - 1, 0), 0)),
            pl.BlockSpec((None, tl, c), lambda i, j: (i, j, 0)),
            pl.BlockSpec((None, halo, c), lambda i, j: (i, jnp.minimum((j + 1) * nh, last), 0)),
            pl.BlockSpec((ksize, c), const),
            pl.BlockSpec((1, c), const), pl.BlockSpec((1, c), const), pl.BlockSpec((1, c), const),
        ],
        out_specs=pl.BlockSpec((None, tl, c), lambda i, j: (i, j, 0)),
        out_shape=jax.ShapeDtypeStruct((bsz, length, c), BF16),
        scratch_shapes=[pltpu.VMEM((tl + 2 * halo, c), F32),
                        pltpu.VMEM((SUBLANES - 1, tl + 2 * halo - SUBLANES, c), F32)],
        compiler_params=_cparams("parallel", "parallel"),
        name="conf",
    )(u, u, u, w, cb, g, b)


def _filt_ffn_kernel(feat_ref, w1h_ref, w1l_ref, b1_ref, f1_ref, w2h_ref, w2l_ref, b2_ref, f2_ref, o_ref):
    a = _dot3_lhs(w1h_ref[...], w1l_ref[...], feat_ref[...]) + b1_ref[...]
    h = jnp.sin(f1_ref[...] * a)
    a2 = _dot3_lhs(w2h_ref[...], w2l_ref[...], h) + b2_ref[...]
    o_ref[...] = jnp.sin(f2_ref[...] * a2)


def _filt_ffn(feat_t, w1h, w1l, b1, f1, w2h, w2l, b2, f2, *, tn):
    kp, n = feat_t.shape
    m = w1h.shape[0]
    const = lambda i: (0, 0)
    return pl.pallas_call(
        _filt_ffn_kernel,
        grid=(n // tn,),
        in_specs=[pl.BlockSpec((kp, tn), lambda i: (0, i)),
                  pl.BlockSpec((m, kp), const), pl.BlockSpec((m, kp), const),
                  pl.BlockSpec((m, 1), const), pl.BlockSpec((m, 1), const),
                  pl.BlockSpec((m, m), const), pl.BlockSpec((m, m), const),
                  pl.BlockSpec((m, 1), const), pl.BlockSpec((m, 1), const)],
        out_specs=pl.BlockSpec((m, tn), lambda i: (0, i)),
        out_shape=jax.ShapeDtypeStruct((m, n), F32),
        compiler_params=_cparams("parallel"),
        name="filt_ffn",
    )(feat_t, w1h, w1l, b1, f1, w2h, w2l, b2, f2)


def _filt_kernel(h2_ref, w3fh_ref, w3fl_ref, w3bh_ref, w3bl_ref, delta_ref, text_ref, valid_ref, o_ref):
    n = h2_ref.shape[1]
    half = n // 2
    decay = jnp.exp(-text_ref[...] * delta_ref[...]) * valid_ref[...]
    kf = _dot3_lhs(w3fh_ref[...], w3fl_ref[...], h2_ref[:, 0:half]) * decay[:, 0:half]
    kb = _dot3_lhs(w3bh_ref[...], w3bl_ref[...], h2_ref[:, half:n]) * decay[:, half:n]
    ss = jnp.sum(kf * kf, axis=1, keepdims=True) + jnp.sum(kb * kb, axis=1, keepdims=True)
    scale = lax.rsqrt(ss + 1e-6)
    o_ref[:, 0:half] = kf * scale
    o_ref[:, half:n] = kb * scale


def _filt(h2t, w3fh, w3fl, w3bh, w3bl, delta, text, valid, *, g):
    m, n = h2t.shape
    rows = w3fh.shape[0]
    const = lambda i: (0, 0)
    return pl.pallas_call(
        _filt_kernel,
        grid=(rows // g,),
        in_specs=[pl.BlockSpec((m, n), const),
                  pl.BlockSpec((g, m), lambda i: (i, 0)), pl.BlockSpec((g, m), lambda i: (i, 0)),
                  pl.BlockSpec((g, m), lambda i: (i, 0)), pl.BlockSpec((g, m), lambda i: (i, 0)),
                  pl.BlockSpec((g, 1), lambda i: (i, 0)),
                  pl.BlockSpec((1, n), const), pl.BlockSpec((1, n), const)],
        out_specs=pl.BlockSpec((g, n), lambda i: (i, 0)),
        out_shape=jax.ShapeDtypeStruct((rows, n), F32),
        compiler_params=_cparams("parallel"),
        name="filt",
    )(h2t, w3fh, w3fl, w3bh, w3bl, delta, text, valid)


def _dft_tables(r1, n1_used):
    n = r1 * LANES
    a1 = 2.0 * np.pi * np.outer(np.arange(n1_used), np.arange(r1)) / r1
    a2 = 2.0 * np.pi * np.outer(np.arange(LANES), np.arange(LANES)) / LANES
    at = 2.0 * np.pi * np.outer(np.arange(LANES), np.arange(r1)) / n
    return n, a1, a2, at


def _cplx_rhs(ang, sign):
    c, s = np.cos(ang), sign * np.sin(ang)
    return np.block([[c, s], [-s, c]])


def _fwd_steps(x_t, w1, twr, twi, w2):
    g, n2, k1w = x_t.shape
    r1 = twr.shape[1]
    a = _dot(x_t.reshape(g * n2, k1w).astype(BF16), w1).reshape(g, n2, 2 * r1)
    ar, ai = a[:, :, :r1], a[:, :, r1:]
    br = ar * twr - ai * twi
    bi = ar * twi + ai * twr
    bt = jnp.concatenate([jnp.swapaxes(br.astype(BF16), 1, 2), jnp.swapaxes(bi.astype(BF16), 1, 2)],
                         axis=-1)
    c = _dot(bt.reshape(g * r1, 2 * n2), w2)
    return c.reshape(g, r1, 2 * n2)


def _inv_steps(d, w3, twr_t, twi_t, w4):
    g, r1, w = d.shape
    n2 = w // 2
    e = _dot(d.reshape(g * r1, w).astype(BF16), w3).reshape(g, r1, w)
    er, ei = e[:, :, :n2], e[:, :, n2:]
    fr = er * twr_t + ei * twi_t
    fi = ei * twr_t - er * twi_t
    ft = jnp.concatenate([jnp.swapaxes(fr.astype(BF16), 1, 2), jnp.swapaxes(fi.astype(BF16), 1, 2)],
                         axis=-1)
    y = _dot(ft.reshape(g * n2, 2 * r1), w4)
    return y.reshape(g, n2, w4.shape[1])


def _kfft_kernel(k_ref, w1_ref, twr_ref, twi_ref, w2_ref, o_ref):
    x_t = jnp.swapaxes(k_ref[...].astype(BF16), 1, 2)
    o_ref[...] = _fwd_steps(x_t, w1_ref[...], twr_ref[...], twi_ref[...], w2_ref[...])


def _kfft(k3, w1, twr, twi, w2, *, g):
    rows, r1, lanes = k3.shape
    const = lambda i: (0, 0)
    return pl.pallas_call(
        _kfft_kernel,
        grid=(rows // g,),
        in_specs=[pl.BlockSpec((g, r1, lanes), lambda i: (i, 0, 0)),
                  pl.BlockSpec(w1.shape, const), pl.BlockSpec(twr.shape, const),
                  pl.BlockSpec(twi.shape, const), pl.BlockSpec(w2.shape, const)],
        out_specs=pl.BlockSpec((g, r1, 2 * lanes), lambda i: (i, 0, 0)),
        out_shape=jax.ShapeDtypeStruct((rows, r1, 2 * lanes), F32),
        compiler_params=_cparams("parallel"),
        name="kfft",
    )(k3, w1, twr, twi, w2)


def _short_conv(a, w, b):
    p, g, r, l = a.shape
    a3 = a.reshape(p * g, r, l)
    lane = lax.broadcasted_iota(I32, a3.shape, 2)
    row = lax.broadcasted_iota(I32, a3.shape, 1)
    pl_ = pltpu.roll(a3, 1, 2)
    ql = pltpu.roll(pl_, 1, 1)
    prev = jnp.where(lane == 0, jnp.where(row == 0, 0.0, ql), pl_)
    pr = pltpu.roll(a3, l - 1, 2)
    qr = pltpu.roll(pr, r - 1, 1)
    nxt = jnp.where(lane == l - 1, jnp.where(row == r - 1, 0.0, qr), pr)
    prev, nxt = prev.reshape(a.shape), nxt.reshape(a.shape)
    return w[0][None] * prev + w[1][None] * a + w[2][None] * nxt + b[None]


def _hyena_kernel(x1_ref, x2_ref, v_ref, sw1_ref, sb1_ref, sw2_ref, sb2_ref, swv_ref, sbv_ref,
                  kf0_ref, kf1_ref, skip_ref, w1_ref, twr_ref, twi_ref, w2_ref, w3_ref,
                  twrt_ref, twit_ref, w4_ref, z_ref):
    x1 = _short_conv(x1_ref[...].astype(F32), sw1_ref[...], sb1_ref[...])
    x2 = _short_conv(x2_ref[...].astype(F32), sw2_ref[...], sb2_ref[...])
    z = _short_conv(v_ref[...].astype(F32), swv_ref[...], sbv_ref[...])
    rh = z.shape[2]
    for gate, kf_ref, o in ((x1, kf0_ref, 0), (x2, kf1_ref, 1)):
        s = jnp.concatenate([z[0], z[1]], axis=1)
        c = _fwd_steps(jnp.swapaxes(s, 1, 2), w1_ref[...], twr_ref[...], twi_ref[...], w2_ref[...])
        kf = kf_ref[...]
        n2 = c.shape[2] // 2
        cr, ci, kr, ki = c[:, :, :n2], c[:, :, n2:], kf[:, :, :n2], kf[:, :, n2:]
        d = jnp.concatenate([cr * kr - ci * ki, cr * ki + ci * kr], axis=-1)
        y = _inv_steps(d, w3_ref[...], twrt_ref[...], twit_ref[...], w4_ref[...])
        y = jnp.swapaxes(y, 1, 2)
        conv = jnp.stack([y[:, :rh], y[:, rh:]], axis=0)
        z = gate * (conv + skip_ref[o][None] * z)
    z_ref[...] = z


def _hyena(hy4, sw, sb, kf, skip, tabs, *, g):
    bsz, c3, rh, lanes = hy4.shape
    c = c3 // 3
    r1 = 2 * rh
    ncb = c // g
    w1, twr, twi, w2, w3, twrt, twit, w4 = tabs
    const2 = lambda i, p: (0, 0)
    data = lambda off: pl.BlockSpec((2, g, rh, lanes), lambda i, p, off=off: (p, off + i, 0, 0))
    wspec = lambda off: pl.BlockSpec((3, g, 1, lanes), lambda i, p, off=off: (0, off + i, 0, 0))
    bspec = lambda off: pl.BlockSpec((g, 1, lanes), lambda i, p, off=off: (off + i, 0, 0))
    kspec = lambda off: pl.BlockSpec((g, r1, 2 * lanes), lambda i, p, off=off: (off + i, 0, 0))
    return pl.pallas_call(
        _hyena_kernel,
        grid=(ncb, bsz // 2),
        in_specs=[data(0), data(ncb), data(2 * ncb),
                  wspec(0), bspec(0), wspec(ncb), bspec(ncb), wspec(2 * ncb), bspec(2 * ncb),
                  kspec(0), kspec(ncb),
                  pl.BlockSpec((2, g, 1, lanes), lambda i, p: (0, i, 0, 0)),
                  pl.BlockSpec(w1.shape, const2), pl.BlockSpec(twr.shape, const2),
                  pl.BlockSpec(twi.shape, const2), pl.BlockSpec(w2.shape, const2),
                  pl.BlockSpec(w3.shape, const2), pl.BlockSpec(twrt.shape, const2),
                  pl.BlockSpec(twit.shape, const2), pl.BlockSpec(w4.shape, const2)],
        out_specs=pl.BlockSpec((2, g, rh, lanes), lambda i, p: (p, i, 0, 0)),
        out_shape=jax.ShapeDtypeStruct((bsz, c, rh, lanes), F32),
        compiler_params=_cparams("parallel", "parallel"),
        name="hyena",
    )(hy4, hy4, hy4, sw, sb, sw, sb, sw, sb, kf, kf, skip, w1, twr, twi, w2, w3, twrt, twit, w4)


def _mix_kernel(ua_ref, zt_ref, gate_ref, h_ref, wa_ref, wb_ref, wm_ref, g_ref, b_ref, o_ref, *, alpha, sub):
    for r in range(0, h_ref.shape[0], sub):
        rs = slice(r, r + sub)
        ya = _dot(ua_ref[rs, :], wa_ref[...])
        zt = jnp.transpose(zt_ref[:, rs]).astype(BF16)
        yb = _dot(zt, wb_ref[...])
        gt = gate_ref[rs, :].astype(F32)
        d = ya.shape[1]
        m = gt[:, :d] * ya + gt[:, d:] * yb
        mixed = _dot(m.astype(BF16), wm_ref[...])
        o_ref[rs, :] = _ln(alpha * h_ref[rs, :] + mixed, g_ref[...], b_ref[...])


def _mix(ua, zt, gates, h, wa, wb, wm, g, b, *, tm, sub, alpha):
    bsz, length, d = h.shape
    c = ua.shape[2]
    const = lambda i, j: (0, 0)
    return pl.pallas_call(
        functools.partial(_mix_kernel, alpha=alpha, sub=min(sub, tm)),
        grid=(bsz, length // tm),
        in_specs=[pl.BlockSpec((None, tm, c), lambda i, j: (i, j, 0)),
                  pl.BlockSpec((None, c, tm), lambda i, j: (i, 0, j)),
                  pl.BlockSpec((None, tm, 2 * d), lambda i, j: (i, j, 0)),
                  pl.BlockSpec((None, tm, d), lambda i, j: (i, j, 0)),
                  pl.BlockSpec((c, d), const), pl.BlockSpec((c, d), const), pl.BlockSpec((d, d), const),
                  pl.BlockSpec((1, d), const), pl.BlockSpec((1, d), const)],
        out_specs=pl.BlockSpec((None, tm, d), lambda i, j: (i, j, 0)),
        out_shape=jax.ShapeDtypeStruct((bsz, length, d), F32),
        compiler_params=_cparams("parallel", "parallel"),
        name="mix",
    )(ua, zt, gates, h, wa, wb, wm, g, b)


def _kv_kernel(mem_ref, wk_ref, wv_ref, k_ref, v_ref):
    mb = mem_ref[...].astype(BF16)
    k_ref[...] = _dot(mb, wk_ref[...]).astype(BF16)
    v_ref[...] = _dot(mb, wv_ref[...]).astype(BF16)


def _kv(mem, wk, wv):
    bsz, m, d = mem.shape
    const = lambda i: (0, 0)
    blk = pl.BlockSpec((None, m, d), lambda i: (i, 0, 0))
    return pl.pallas_call(
        _kv_kernel,
        grid=(bsz,),
        in_specs=[blk, pl.BlockSpec((d, d), const), pl.BlockSpec((d, d), const)],
        out_specs=[blk, blk],
        out_shape=[jax.ShapeDtypeStruct((bsz, m, d), BF16)] * 2,
        compiler_params=_cparams("parallel"),
        name="kv",
    )(mem, wk, wv)


def _xattn_kernel(x_ref, k_ref, v_ref, wq_ref, wo_ref, g_ref, b_ref, wrh_ref, wrl_ref,
                  x2_ref, x2b_ref, aff_ref, *, alpha, heads, sub):
    d = x_ref.shape[1]
    dh = d // heads
    wrh, wrl = wrh_ref[...], wrl_ref[...]
    for r in range(0, x_ref.shape[0], sub):
        rs = slice(r, r + sub)
        x = x_ref[rs, :]
        q = (_dot(x.astype(BF16), wq_ref[...]) * (dh ** -0.5)).astype(BF16)
        outs = []
        for hd in range(heads):
            sl = slice(hd * dh, (hd + 1) * dh)
            s = _dot_nt(q[:, sl], k_ref[:, sl])
            s = s - jnp.max(s, axis=-1, keepdims=True)
            p = jnp.exp(s)
            p = p / jnp.sum(p, axis=-1, keepdims=True)
            outs.append(_dot(p.astype(BF16), v_ref[:, sl]))
        o = jnp.concatenate(outs, axis=-1)
        xa = _dot(o.astype(BF16), wo_ref[...])
        x2 = _ln(alpha * x + xa, g_ref[...], b_ref[...])
        x2_ref[rs, :] = x2
        x2h, x2l = _split_bf16(x2)
        x2b_ref[rs, :] = x2h
        logits = _dot_nt(wrh, x2h) + (_dot_nt(wrh, x2l) + _dot_nt(wrl, x2h))
        logits = logits - jnp.max(logits, axis=0, keepdims=True)
        ex = jnp.exp(logits)
        aff_ref[:, rs] = ex / jnp.sum(ex, axis=0, keepdims=True)


def _xattn(x, k, v, wq, wo, g, b, wrh, wrl, *, tm, sub, alpha, heads):
    bsz, length, d = x.shape
    m = k.shape[1]
    e = wrh.shape[0]
    const = lambda i, j: (0, 0)
    tok = pl.BlockSpec((None, tm, d), lambda i, j: (i, j, 0))
    mem = pl.BlockSpec((None, m, d), lambda i, j: (i, 0, 0))
    return pl.pallas_call(
        functools.partial(_xattn_kernel, alpha=alpha, heads=heads, sub=min(sub, tm)),
        grid=(bsz, length // tm),
        in_specs=[tok, mem, mem, pl.BlockSpec((d, d), const), pl.BlockSpec((d, d), const),
                  pl.BlockSpec((1, d), const), pl.BlockSpec((1, d), const),
                  pl.BlockSpec((e, d), const), pl.BlockSpec((e, d), const)],
        out_specs=[tok, tok, pl.BlockSpec((None, e, tm), lambda i, j: (i, 0, j))],
        out_shape=[jax.ShapeDtypeStruct((bsz, length, d), F32),
                   jax.ShapeDtypeStruct((bsz, length, d), BF16),
                   jax.ShapeDtypeStruct((bsz, e, length), F32)],
        compiler_params=_cparams("parallel", "parallel"),
        name="xattn",
    )(x, k, v, wq, wo, g, b, wrh, wrl)


def _excl_cumsum_lanes(mask_f32, tri, write):
    e, t = mask_f32.shape
    carry = jnp.zeros((e, 1), F32)
    for c in range(t // LANES):
        m = mask_f32[:, c * LANES:(c + 1) * LANES]
        inc = _dot(m.astype(BF16), tri)
        write(c, inc - m + carry, carry)
        carry = carry + inc[:, LANES - 1:LANES]


def _select_kernel(aff_ref, tri_ref, slot_ref, slott_ref, ghl_ref, r0_ref, jlo_ref, nch_ref, cnt_ref, cum_ref,
                   *, cap, st, kct):
    a = aff_ref[...]
    e = a.shape[0]
    v = jnp.zeros((e, 1), I32)
    for bit in range(30, -1, -1):
        cand = v | (1 << bit)
        cnt = jnp.sum(jnp.where(a >= pltpu.bitcast(cand, F32), 1.0, 0.0), axis=1, keepdims=True)
        v = jnp.where(cnt >= cap, cand, v)
    thr = pltpu.bitcast(v, F32)
    gt = a > thr
    eq = a == thr
    need = cap - jnp.sum(jnp.where(gt, 1.0, 0.0), axis=1, keepdims=True)
    tri = tri_ref[...]

    def write_eq(c, val, before):
        cnt_ref[:, c * LANES:(c + 1) * LANES] = val
    _excl_cumsum_lanes(jnp.where(eq, 1.0, 0.0), tri, write_eq)
    sel = gt | (eq & (cnt_ref[...] < need))
    gate = jnp.where(sel, a, 0.0)

    n_tiles = a.shape[1] // LANES

    def write_rank(c, val, before):
        cnt_ref[:, c * LANES:(c + 1) * LANES] = val
        cum_ref[:, c:c + 1] = before
    _excl_cumsum_lanes(jnp.where(sel, 1.0, 0.0), tri, write_rank)
    slot = jnp.where(sel, cnt_ref[...], -1.0)
    slot_ref[...] = slot.astype(I32)
    slott_ref[...] = jnp.transpose(slot).astype(I32)
    g_hi = gate.astype(BF16).astype(F32)
    g_lo = (gate - g_hi).astype(BF16).astype(F32)
    ghl_ref[...] = jnp.transpose(jnp.concatenate([g_hi, g_lo], axis=0)).astype(BF16)

    before = cum_ref[:, 0:n_tiles]
    r0_ref[...] = before.astype(I32)
    after = jnp.concatenate([before[:, 1:], jnp.full((e, 1), cap, F32)], axis=1)
    for s in range(cap // st):
        lo = jnp.sum(jnp.where(after <= s * st, 1.0, 0.0), axis=1, keepdims=True)
        hi = jnp.sum(jnp.where(before < (s + 1) * st, 1.0, 0.0), axis=1, keepdims=True) - 1.0
        n = jnp.ones_like(lo)
        for m in range(1, -(-n_tiles // kct)):
            n = n + jnp.where(hi - lo >= m * kct, 1.0, 0.0)
        jlo_ref[:, s:s + 1] = lo.astype(I32)
        nch_ref[:, s:s + 1] = n.astype(I32)


def _select(aff, tri, *, cap, st, kct):
    bsz, e, t = aff.shape
    n_tiles, n_st = t // LANES, cap // st
    blk = pl.BlockSpec((None, e, t), lambda i: (i, 0, 0))
    small = lambda n: pl.BlockSpec((None, e, n), lambda i: (i, 0, 0))
    return pl.pallas_call(
        functools.partial(_select_kernel, cap=cap, st=st, kct=kct),
        grid=(bsz,),
        in_specs=[blk, pl.BlockSpec((LANES, LANES), lambda i: (0, 0))],
        out_specs=[blk, pl.BlockSpec((None, t, e), lambda i: (i, 0, 0)),
                   pl.BlockSpec((None, t, 2 * e), lambda i: (i, 0, 0)),
                   small(n_tiles), small(n_st), small(n_st)],
        out_shape=[jax.ShapeDtypeStruct((bsz, e, t), I32), jax.ShapeDtypeStruct((bsz, t, e), I32),
                   jax.ShapeDtypeStruct((bsz, t, 2 * e), BF16),
                   jax.ShapeDtypeStruct((bsz, e, n_tiles), I32),
                   jax.ShapeDtypeStruct((bsz, e, n_st), I32), jax.ShapeDtypeStruct((bsz, e, n_st), I32)],
        scratch_shapes=[pltpu.VMEM((e, t), F32), pltpu.VMEM((e, LANES), F32)],
        compiler_params=_cparams("parallel"),
        name="select",
    )(aff, tri)


def _window_start(r0, cap, win, align):
    return pl.multiple_of(jnp.minimum((r0 // align) * align, cap - win), align)


def _dispatch_kernel(jlo_ref, nch_ref, x_ref, slot_ref, ghl_ref, o_ref, gs_ref, acc_ref, gacc_ref,
                     *, st, kc, tt):
    b, e = pl.program_id(0), pl.program_id(1)
    t_total = x_ref.shape[0]
    n_e = ghl_ref.shape[1] // 2
    for s in range(o_ref.shape[0] // st):
        j0 = jlo_ref[b, e, s]

        def chunk(c, s=s, j0=j0):
            want = (j0 + c * (kc // tt)) * tt
            t0 = pl.multiple_of(jnp.minimum(want, t_total - kc), tt)
            tok = t0 + lax.broadcasted_iota(I32, (1, kc), 1)
            rel = jnp.where(tok >= want, slot_ref[:, pl.ds(t0, kc)] - s * st, -1)
            onehot = jnp.where(lax.broadcasted_iota(I32, (st, kc), 0) == rel, 1.0, 0.0).astype(BF16)
            xs = _dot(onehot, x_ref[pl.ds(t0, kc), :])
            gs = _dot(onehot, ghl_ref[pl.ds(t0, kc), :])
            return xs, gs

        xs, gs = chunk(0)
        acc_ref[...] = xs
        gacc_ref[...] = gs

        def body(c, carry):
            xs, gs = chunk(c)
            acc_ref[...] += xs
            gacc_ref[...] += gs
            return carry
        lax.fori_loop(1, nch_ref[b, e, s], body, 0)
        o_ref[s * st:(s + 1) * st, :] = acc_ref[...].astype(o_ref.dtype)
        lane = lax.broadcasted_iota(I32, (st, 2 * n_e), 1)
        mine = (lane == e) | (lane == e + n_e)
        gs_ref[s * st:(s + 1) * st, :] = jnp.sum(jnp.where(mine, gacc_ref[...], 0.0), axis=1, keepdims=True)


def _dispatch(jlo, nch, xb, slot4, gate_hl, *, cap, st, kc, tt):
    bsz, t, d = xb.shape
    e = slot4.shape[1]
    assert cap % st == 0 and kc % tt == 0 and t % tt == 0 and kc <= t
    return pl.pallas_call(
        functools.partial(_dispatch_kernel, st=st, kc=kc, tt=tt),
        grid_spec=pltpu.PrefetchScalarGridSpec(
            num_scalar_prefetch=2,
            grid=(bsz, e),
            in_specs=[pl.BlockSpec((None, t, d), lambda i, j, a, c: (i, 0, 0)),
                      pl.BlockSpec((None, None, 1, t), lambda i, j, a, c: (i, j, 0, 0)),
                      pl.BlockSpec((None, t, 2 * e), lambda i, j, a, c: (i, 0, 0))],
            out_specs=[pl.BlockSpec((None, None, cap, d), lambda i, j, a, c: (i, j, 0, 0)),
                       pl.BlockSpec((None, None, cap, 1), lambda i, j, a, c: (i, j, 0, 0))],
            scratch_shapes=[pltpu.VMEM((st, d), F32), pltpu.VMEM((st, 2 * e), F32)]),
        out_shape=[jax.ShapeDtypeStruct((bsz, e, cap, d), BF16),
                   jax.ShapeDtypeStruct((bsz, e, cap, 1), F32)],
        compiler_params=_cparams("parallel", "parallel"),
        name="dispatch",
    )(jlo, nch, xb, slot4, gate_hl)


def _expert_kernel(x_ref, gs_ref, wg_ref, wu_ref, wd_ref, o_ref, acc_ref):
    f = pl.program_id(2)
    wg = wg_ref[...].astype(BF16)
    wu = wu_ref[...].astype(BF16)
    wd = wd_ref[...].astype(BF16)
    @pl.when(f == 0)
    def _():
        acc_ref[...] = jnp.zeros_like(acc_ref)

    for bi in range(x_ref.shape[0]):
        x = x_ref[bi]
        hg = _dot(x, wg)
        hu = _dot(x, wu)
        act = (hg * _sigmoid(hg) * hu).astype(BF16)
        acc_ref[bi] += _dot(act, wd)

    @pl.when(f == pl.num_programs(2) - 1)
    def _():
        o_ref[...] = (acc_ref[...] * gs_ref[...]).astype(o_ref.dtype)


def _expert(xg, gs, wg, wu, wd, *, fc, nb):
    bsz, e, cap, d = xg.shape
    f = wg.shape[2]
    assert bsz % nb == 0 and f % fc == 0
    tok = lambda last: pl.BlockSpec((nb, None, cap, last), lambda i, j, k: (j, i, 0, 0))
    return pl.pallas_call(
        _expert_kernel,
        grid=(e, bsz // nb, f // fc),
        in_specs=[tok(d), tok(1),
                  pl.BlockSpec((None, d, fc), lambda i, j, k: (i, 0, k)),
                  pl.BlockSpec((None, d, fc), lambda i, j, k: (i, 0, k)),
                  pl.BlockSpec((None, fc, d), lambda i, j, k: (i, k, 0))],
        out_specs=tok(d),
        out_shape=jax.ShapeDtypeStruct((bsz, e, cap, d), BF16),
        scratch_shapes=[pltpu.VMEM((nb, cap, d), F32)],
        compiler_params=_cparams("parallel", "parallel", "arbitrary"),
        name="expert",
    )(xg, gs, wg, wu, wd)


def _combine_kernel(r0_ref, x_ref, slot_ref, g_ref, b_ref, y_ref, o_ref, *, tt, win, align, alpha):
    b, j = pl.program_id(0), pl.program_id(1)
    n_e, cap = y_ref.shape[0], y_ref.shape[1]
    acc = alpha * x_ref[...]
    slots = slot_ref[...]
    for e in range(n_e):
        start = _window_start(r0_ref[b, e, j], cap, win, align)
        rel = slots[:, e:e + 1] - start
        onehot = jnp.where(lax.broadcasted_iota(I32, (tt, win), 1) == rel, 1.0, 0.0).astype(BF16)
        acc = acc + _dot(onehot, y_ref[e, pl.ds(start, win), :])
    o_ref[...] = _ln(acc, g_ref[...], b_ref[...])


def _combine(r0, x, slot_t, g, b, y, *, tt, win, align, alpha):
    bsz, t, d = x.shape
    e, cap = y.shape[1], y.shape[2]
    assert win >= tt + align and win <= cap and (cap - win) % align == 0
    const = lambda i, j, r: (0, 0)
    return pl.pallas_call(
        functools.partial(_combine_kernel, tt=tt, win=win, align=align, alpha=alpha),
        grid_spec=pltpu.PrefetchScalarGridSpec(
            num_scalar_prefetch=1,
            grid=(bsz, t // tt),
            in_specs=[pl.BlockSpec((None, tt, d), lambda i, j, r: (i, j, 0)),
                      pl.BlockSpec((None, tt, e), lambda i, j, r: (i, j, 0)),
                      pl.BlockSpec((1, d), const), pl.BlockSpec((1, d), const),
                      pl.BlockSpec((None, e, cap, d), lambda i, j, r: (i, 0, 0, 0),
                                   pipeline_mode=pl.Buffered(1))],
            out_specs=pl.BlockSpec((None, tt, d), lambda i, j, r: (i, j, 0))),
        out_shape=jax.ShapeDtypeStruct((bsz, t, d), F32),
        compiler_params=_cparams("parallel", "parallel"),
        name="combine",
    )(r0, x, slot_t, g, b, y)


def _hyena_constants(length):
    n = 2 * length
    m = np.arange(n)
    pos = np.where(m < length, m, n - m).astype(np.float64)
    pos[length] = 0.0
    t = pos / max(length - 1, 1)
    f = np.linspace(1e-4, HY_BANDS - 1, HY_BANDS)
    ang = (2.0 * np.pi * pos / length)[:, None] * f[None, :]
    feat = np.concatenate([t[:, None], np.cos(ang), -np.sin(ang)], axis=-1)
    feat_t = np.zeros((LANES, n), np.float32)
    feat_t[:feat.shape[1]] = feat.T
    valid = np.ones((1, n), np.float32)
    valid[0, length] = 0.0
    return feat_t, t[None, :].astype(np.float32), valid


def _decay_rates(width):
    max_decay = math.log(HY_DECAY_TARGET) / HY_FAST_DECAY
    min_decay = math.log(HY_DECAY_TARGET) / HY_SLOW_DECAY
    return np.abs(np.linspace(min_decay, max_decay, width, dtype=np.float32)).astype(np.float32)


def _dft_constants(r1):
    n, a1h, a2, at = _dft_tables(r1, r1 // 2)
    _, a1f, _, _ = _dft_tables(r1, r1)
    bf = lambda x: jnp.asarray(x, F32).astype(BF16)
    w1_data = bf(_cplx_rhs(a1h, -1.0))
    w1_filt = bf(np.concatenate([np.cos(a1f), -np.sin(a1f)], axis=1))
    w2 = bf(_cplx_rhs(a2, -1.0))
    w3 = bf(_cplx_rhs(a2.T, 1.0))
    w4 = bf(_cplx_rhs(a1h.T, 1.0) / n)
    twr, twi = np.cos(at), -np.sin(at)
    f32 = lambda x: jnp.asarray(x, F32)
    return dict(w1_data=w1_data, w1_filt=w1_filt, w2=w2, w3=w3, w4=w4,
                twr=f32(twr), twi=f32(twi), twr_t=f32(twr.T), twi_t=f32(twi.T))


def kernel(x, mem, ln_in_g, ln_in_b, w_in, b_gate, conf_dw_w, conf_dw_b, conf_ln_g, conf_ln_b, conf_w_out, hy_short_w, hy_short_b, hy_ffn_w1, hy_ffn_b1, hy_freq1, hy_ffn_w2, hy_ffn_b2, hy_freq2, hy_ffn_w3, hy_skip, hy_w_out, w_mix_out, ln_mix_g, ln_mix_b, xa_wq, xa_wk, xa_wv, xa_wo, ln_xa_g, ln_xa_b, moe_w_router, moe_w_gate, moe_w_up, moe_w_down, ln_moe_g, ln_moe_b):
    bsz, length, d = x.shape
    depth = w_in.shape[0]
    alpha = (2.0 * depth) ** 0.25
    cw = conf_dw_w.shape[2]
    hw = hy_skip.shape[2]
    n_glu, n_hy = 2 * cw, 3 * hw
    n_exp = moe_w_router.shape[2]
    cap = max(1, N_EXPERT_CAPACITY * length // n_exp)
    rh = length // LANES
    r1 = 2 * rh
    assert bsz % 2 == 0 and length % LANES == 0

    row = lambda v: v.reshape(1, -1).astype(F32)
    col = lambda v: v.reshape(-1, 1).astype(F32)

    feat_t, t_ext, valid = _hyena_constants(length)
    dft = _dft_constants(r1)
    rates = _decay_rates(hw)
    tri = jnp.asarray(np.triu(np.ones((LANES, LANES), np.float32))).astype(BF16)

    n_ord = hy_skip.shape[1]
    tm = min(512, length)
    lane_b = lambda v: jnp.broadcast_to(v[..., None, None], v.shape + (1, LANES)).astype(F32)
    dft_data = (dft["w1_data"], dft["twr"], dft["twi"], dft["w2"], dft["w3"], dft["twr_t"], dft["twi_t"],
                dft["w4"])

    h = x
    for i in range(depth):
        w1h, w1l = _split_bf16(jnp.pad(hy_ffn_w1[i].T, ((0, 0), (0, LANES - hy_ffn_w1.shape[1]))))
        w2h, w2l = _split_bf16(hy_ffn_w2[i].T)
        h2t = _filt_ffn(jnp.asarray(feat_t), w1h, w1l, col(hy_ffn_b1[i]), col(hy_freq1[i]),
                        w2h, w2l, col(hy_ffn_b2[i]), col(hy_freq2[i]), tn=min(2048, 2 * length))
        w3 = hy_ffn_w3[i].reshape(-1, n_ord, 2, hw)
        w3f = jnp.transpose(w3[:, :, 0], (1, 2, 0)).reshape(n_ord * hw, -1)
        w3b = jnp.transpose(w3[:, :, 1], (1, 2, 0)).reshape(n_ord * hw, -1)
        w3fh, w3fl = _split_bf16(w3f)
        w3bh, w3bl = _split_bf16(w3b)
        delta = jnp.asarray(np.tile(rates, n_ord).reshape(-1, 1))
        kern = _filt(h2t, w3fh, w3fl, w3bh, w3bl, delta, jnp.asarray(t_ext), jnp.asarray(valid),
                     g=min(128, hw))
        kf = _kfft(kern.reshape(n_ord * hw, r1, LANES), dft["w1_filt"], dft["twr"], dft["twi"], dft["w2"],
                   g=min(16, hw))

        wi = w_in[i]
        wglu = wi[:, :n_glu].astype(BF16)
        whyt = wi[:, n_glu:n_glu + n_hy].T.astype(BF16)
        wgate = wi[:, n_glu + n_hy:].astype(BF16)
        hn, u, hyt, gates = _inproj(h, row(ln_in_g), row(ln_in_b), wglu, whyt, wgate, row(b_gate[i]),
                                    tm=tm, apply_ln=(i == 0))

        ua = _conf(u, conf_dw_w[i], row(conf_dw_b[i]), row(conf_ln_g[i]), row(conf_ln_b[i]),
                   tl=min(512, length))

        z = _hyena(hyt.reshape(bsz, n_hy, rh, LANES), lane_b(hy_short_w[i]), lane_b(hy_short_b[i]), kf,
                   lane_b(hy_skip[i]), dft_data, g=min(16, hw))

        x1 = _mix(ua, z.reshape(bsz, hw, length), gates, hn, conf_w_out[i].astype(BF16),
                  hy_w_out[i].astype(BF16), w_mix_out[i].astype(BF16), row(ln_mix_g[i]), row(ln_mix_b[i]),
                  tm=min(1024, length), sub=512, alpha=alpha)

        k, v = _kv(mem, xa_wk[i].astype(BF16), xa_wv[i].astype(BF16))
        wrh, wrl = _split_bf16(moe_w_router[i].T)
        x2, x2b, aff = _xattn(x1, k, v, xa_wq[i].astype(BF16), xa_wo[i].astype(BF16), row(ln_xa_g[i]),
                              row(ln_xa_b[i]), wrh, wrl, tm=min(1024, length), sub=512, alpha=alpha,
                              heads=XA_HEADS)

        tt, st, align = LANES, LANES, 16
        kc = min(12 * tt, length)
        slot, slot_t, gate_hl, r0, jlo, nch = _select(aff, tri, cap=cap, st=st, kct=kc // tt)
        xg, gslot = _dispatch(jlo, nch, x2b, slot.reshape(bsz, n_exp, 1, length), gate_hl,
                              cap=cap, st=st, kc=kc, tt=tt)
        y = _expert(xg, gslot, moe_w_gate[i], moe_w_up[i], moe_w_down[i],
                    fc=min(512, moe_w_gate.shape[3]), nb=2)
        h = _combine(r0, x2, slot_t, row(ln_moe_g[i]), row(ln_moe_b[i]), y,
                     tt=tt, win=tt + align, align=align, alpha=alpha)
    return h
```

```python
import functools
import math

import numpy as np
import jax
import jax.numpy as jnp
from jax import lax
from jax.experimental import pallas as pl
from jax.experimental.pallas import tpu as pltpu

F32 = jnp.float32
BF16 = jnp.bfloat16
I32 = jnp.int32

LANES = 128
SUBLANES = 8
VMEM_LIMIT_BYTES = 56 * 1024 * 1024
SUB_ROWS = 256

LN_EPS = 1e-5
XA_HEADS = 4
N_EXPERT_CAPACITY = 2
HY_DECAY_TARGET = 1e-2
HY_FAST_DECAY = 0.3
HY_SLOW_DECAY = 1.5
HY_BANDS = 16


def _cparams(*sem):
    return pltpu.CompilerParams(dimension_semantics=sem, vmem_limit_bytes=VMEM_LIMIT_BYTES)


def _ln(x, g, b):
    mu = jnp.mean(x, axis=-1, keepdims=True)
    xc = x - mu
    var = jnp.mean(xc * xc, axis=-1, keepdims=True)
    return xc * lax.rsqrt(var + LN_EPS) * g + b


def _sigmoid(x):
    return 1.0 / (1.0 + jnp.exp(-x))


def _dot(a, b):
    return jnp.dot(a, b, preferred_element_type=F32)


def _dot_nt(a, b):
    return lax.dot_general(a, b, (((1,), (1,)), ((), ())), preferred_element_type=F32)


def _split_bf16(x):
    hi = x.astype(BF16)
    lo = (x - hi.astype(F32)).astype(BF16)
    return hi, lo


def _dot3(a, b_hi, b_lo):
    a_hi, a_lo = _split_bf16(a)
    return _dot(a_hi, b_hi) + (_dot(a_hi, b_lo) + _dot(a_lo, b_hi))


def _dot3_lhs(a_hi, a_lo, b):
    b_hi, b_lo = _split_bf16(b)
    return _dot(a_hi, b_hi) + (_dot(a_hi, b_lo) + _dot(a_lo, b_hi))


def _inproj_kernel(x_ref, g_ref, b_ref, wglu_ref, whyt_ref, wgate_ref, bgate_ref,
                   h_ref, u_ref, hyt_ref, gate_ref, *, apply_ln, sub):
    for r in range(0, x_ref.shape[0], sub):
        rs = slice(r, r + sub)
        h = _ln(x_ref[rs, :], g_ref[...], b_ref[...]) if apply_ln else x_ref[rs, :]
        h_ref[rs, :] = h
        hb = h.astype(BF16)
        glu = _dot(hb, wglu_ref[...])
        cw = glu.shape[1] // 2
        u_ref[rs, :] = glu[:, :cw] * _sigmoid(glu[:, cw:])
        hyt_ref[:, rs] = _dot_nt(whyt_ref[...], hb).astype(hyt_ref.dtype)
        gl = _dot(hb, wgate_ref[...]) + bgate_ref[...]
        gate_ref[rs, :] = _sigmoid(gl).astype(BF16)


def _inproj(x, g, b, wglu, whyt, wgate, bgate, *, tm, apply_ln):
    bsz, length, d = x.shape
    n_glu, n_hy, n_gate = wglu.shape[1], whyt.shape[0], wgate.shape[1]
    const = lambda i, j: (0, 0)
    return pl.pallas_call(
        functools.partial(_inproj_kernel, apply_ln=apply_ln, sub=min(SUB_ROWS, tm)),
        grid=(bsz, length // tm),
        in_specs=[
            pl.BlockSpec((None, tm, d), lambda i, j: (i, j, 0)),
            pl.BlockSpec((1, d), const), pl.BlockSpec((1, d), const),
            pl.BlockSpec((d, n_glu), const),
            pl.BlockSpec((n_hy, d), const),
            pl.BlockSpec((d, n_gate), const),
            pl.BlockSpec((1, n_gate), const),
        ],
        out_specs=[
            pl.BlockSpec((None, tm, d), lambda i, j: (i, j, 0)),
            pl.BlockSpec((None, tm, n_glu // 2), lambda i, j: (i, j, 0)),
            pl.BlockSpec((None, n_hy, tm), lambda i, j: (i, 0, j)),
            pl.BlockSpec((None, tm, n_gate), lambda i, j: (i, j, 0)),
        ],
        out_shape=[
            jax.ShapeDtypeStruct((bsz, length, d), F32),
            jax.ShapeDtypeStruct((bsz, length, n_glu // 2), F32),
            jax.ShapeDtypeStruct((bsz, n_hy, length), BF16),
            jax.ShapeDtypeStruct((bsz, length, n_gate), BF16),
        ],
        compiler_params=_cparams("parallel", "parallel"),
        name="inproj",
    )(x, g, b, wglu, whyt, wgate, bgate)


def _conf_kernel(prev_ref, cur_ref, next_ref, w_ref, cb_ref, g_ref, b_ref, o_ref, ext_ref, sh_ref,
                 *, tl, halo, ksize, rows):
    j = pl.program_id(1)
    nj = pl.num_programs(1)
    pad = (ksize - 1) // 2
    ext_ref[0:halo, :] = jnp.where(j > 0, prev_ref[...], 0.0)
    ext_ref[halo:halo + tl, :] = cur_ref[...]
    ext_ref[halo + tl:halo + tl + halo, :] = jnp.where(j < nj - 1, next_ref[...], 0.0)
    n_sh = sh_ref.shape[1]
    for r in range(1, SUBLANES):
        sh_ref[r - 1, :, :] = ext_ref[r:r + n_sh, :]
    w = w_ref[...]
    cb, g, b = cb_ref[...], g_ref[...], b_ref[...]
    for r0 in range(0, tl, rows):
        acc = jnp.zeros((rows, w.shape[1]), F32)
        for k in range(ksize):
            q, r = divmod(halo - pad + k + r0, SUBLANES)
            q *= SUBLANES
            src = ext_ref[q:q + rows, :] if r == 0 else sh_ref[r - 1, q:q + rows, :]
            acc = acc + w[k:k + 1, :] * src
        y = _ln(acc + cb, g, b)
        o_ref[r0:r0 + rows, :] = (y * _sigmoid(y)).astype(o_ref.dtype)


def _conf(u, w, cb, g, b, *, tl, halo=16, rows=64):
    bsz, length, c = u.shape
    ksize = w.shape[0]
    assert (ksize - 1) // 2 <= halo and tl % halo == 0 and tl % rows == 0
    nh = tl // halo
    last = length // halo - 1
    const = lambda i, j: (0, 0)
    return pl.pallas_call(
        functools.partial(_conf_kernel, tl=tl, halo=halo, ksize=ksize, rows=rows),
        grid=(bsz, length // tl),
        in_specs=[
            pl.BlockSpec((None, halo, c), lambda i, j: (i, jnp.maximum(j * nh - 1, 0), 0)),
            pl.BlockSpec((None, tl, c), lambda i, j: (i, j, 0)),
            pl.BlockSpec((None, halo, c), lambda i, j: (i, jnp.minimum((j + 1) * nh, last), 0)),
            pl.BlockSpec((ksize, c), const),
            pl.BlockSpec((1, c), const), pl.BlockSpec((1, c), const), pl.BlockSpec((1, c), const),
        ],
        out_specs=pl.BlockSpec((None, tl, c), lambda i, j: (i, j, 0)),
        out_shape=jax.ShapeDtypeStruct((bsz, length, c), BF16),
        scratch_shapes=[pltpu.VMEM((tl + 2 * halo, c), F32),
                        pltpu.VMEM((SUBLANES - 1, tl + 2 * halo - SUBLANES, c), F32)],
        compiler_params=_cparams("parallel", "parallel"),
        name="conf",
    )(u, u, u, w, cb, g, b)


def _filt_ffn_kernel(feat_ref, w1h_ref, w1l_ref, b1_ref, f1_ref, w2h_ref, w2l_ref, b2_ref, f2_ref, o_ref):
    a = _dot3_lhs(w1h_ref[...], w1l_ref[...], feat_ref[...]) + b1_ref[...]
    h = jnp.sin(f1_ref[...] * a)
    a2 = _dot3_lhs(w2h_ref[...], w2l_ref[...], h) + b2_ref[...]
    o_ref[...] = jnp.sin(f2_ref[...] * a2)


def _filt_ffn(feat_t, w1h, w1l, b1, f1, w2h, w2l, b2, f2, *, tn):
    kp, n = feat_t.shape
    m = w1h.shape[0]
    const = lambda i: (0, 0)
    return pl.pallas_call(
        _filt_ffn_kernel,
        grid=(n // tn,),
        in_specs=[pl.BlockSpec((kp, tn), lambda i: (0, i)),
                  pl.BlockSpec((m, kp), const), pl.BlockSpec((m, kp), const),
                  pl.BlockSpec((m, 1), const), pl.BlockSpec((m, 1), const),
                  pl.BlockSpec((m, m), const), pl.BlockSpec((m, m), const),
                  pl.BlockSpec((m, 1), const), pl.BlockSpec((m, 1), const)],
        out_specs=pl.BlockSpec((m, tn), lambda i: (0, i)),
        out_shape=jax.ShapeDtypeStruct((m, n), F32),
        compiler_params=_cparams("parallel"),
        name="filt_ffn",
    )(feat_t, w1h, w1l, b1, f1, w2h, w2l, b2, f2)


def _filt_kernel(h2_ref, w3fh_ref, w3fl_ref, w3bh_ref, w3bl_ref, delta_ref, text_ref, valid_ref, o_ref):
    n = h2_ref.shape[1]
    half = n // 2
    decay = jnp.exp(-text_ref[...] * delta_ref[...]) * valid_ref[...]
    kf = _dot3_lhs(w3fh_ref[...], w3fl_ref[...], h2_ref[:, 0:half]) * decay[:, 0:half]
    kb = _dot3_lhs(w3bh_ref[...], w3bl_ref[...], h2_ref[:, half:n]) * decay[:, half:n]
    ss = jnp.sum(kf * kf, axis=1, keepdims=True) + jnp.sum(kb * kb, axis=1, keepdims=True)
    scale = lax.rsqrt(ss + 1e-6)
    o_ref[:, 0:half] = kf * scale
    o_ref[:, half:n] = kb * scale


def _filt(h2t, w3fh, w3fl, w3bh, w3bl, delta, text, valid, *, g):
    m, n = h2t.shape
    rows = w3fh.shape[0]
    const = lambda i: (0, 0)
    return pl.pallas_call(
        _filt_kernel,
        grid=(rows // g,),
        in_specs=[pl.BlockSpec((m, n), const),
                  pl.BlockSpec((g, m), lambda i: (i, 0)), pl.BlockSpec((g, m), lambda i: (i, 0)),
                  pl.BlockSpec((g, m), lambda i: (i, 0)), pl.BlockSpec((g, m), lambda i: (i, 0)),
                  pl.BlockSpec((g, 1), lambda i: (i, 0)),
                  pl.BlockSpec((1, n), const), pl.BlockSpec((1, n), const)],
        out_specs=pl.BlockSpec((g, n), lambda i: (i, 0)),
        out_shape=jax.ShapeDtypeStruct((rows, n), F32),
        compiler_params=_cparams("parallel"),
        name="filt",
    )(h2t, w3fh, w3fl, w3bh, w3bl, delta, text, valid)


def _dft_tables(r1, n1_used):
    n = r1 * LANES
    a1 = 2.0 * np.pi * np.outer(np.arange(n1_used), np.arange(r1)) / r1
    a2 = 2.0 * np.pi * np.outer(np.arange(LANES), np.arange(LANES)) / LANES
    at = 2.0 * np.pi * np.outer(np.arange(LANES), np.arange(r1)) / n
    return n, a1, a2, at


def _cplx_rhs(ang, sign):
    c, s = np.cos(ang), sign * np.sin(ang)
    return np.block([[c, s], [-s, c]])


def _fwd_steps(x_t, w1, twr, twi, w2):
    g, n2, k1w = x_t.shape
    r1 = twr.shape[1]
    a = _dot(x_t.reshape(g * n2, k1w).astype(BF16), w1).reshape(g, n2, 2 * r1)
    ar, ai = a[:, :, :r1], a[:, :, r1:]
    br = ar * twr - ai * twi
    bi = ar * twi + ai * twr
    bt = jnp.concatenate([jnp.swapaxes(br.astype(BF16), 1, 2), jnp.swapaxes(bi.astype(BF16), 1, 2)],
                         axis=-1)
    c = _dot(bt.reshape(g * r1, 2 * n2), w2)
    return c.reshape(g, r1, 2 * n2)


def _inv_steps(d, w3, twr_t, twi_t, w4):
    g, r1, w = d.shape
    n2 = w // 2
    e = _dot(d.reshape(g * r1, w).astype(BF16), w3).reshape(g, r1, w)
    er, ei = e[:, :, :n2], e[:, :, n2:]
    fr = er * twr_t + ei * twi_t
    fi = ei * twr_t - er * twi_t
    ft = jnp.concatenate([jnp.swapaxes(fr.astype(BF16), 1, 2), jnp.swapaxes(fi.astype(BF16), 1, 2)],
                         axis=-1)
    y = _dot(ft.reshape(g * n2, 2 * r1), w4)
    return y.reshape(g, n2, w4.shape[1])


def _kfft_kernel(k_ref, w1_ref, twr_ref, twi_ref, w2_ref, o_ref):
    x_t = jnp.swapaxes(k_ref[...].astype(BF16), 1, 2)
    o_ref[...] = _fwd_steps(x_t, w1_ref[...], twr_ref[...], twi_ref[...], w2_ref[...])


def _kfft(k3, w1, twr, twi, w2, *, g):
    rows, r1, lanes = k3.shape
    const = lambda i: (0, 0)
    return pl.pallas_call(
        _kfft_kernel,
        grid=(rows // g,),
        in_specs=[pl.BlockSpec((g, r1, lanes), lambda i: (i, 0, 0)),
                  pl.BlockSpec(w1.shape, const), pl.BlockSpec(twr.shape, const),
                  pl.BlockSpec(twi.shape, const), pl.BlockSpec(w2.shape, const)],
        out_specs=pl.BlockSpec((g, r1, 2 * lanes), lambda i: (i, 0, 0)),
        out_shape=jax.ShapeDtypeStruct((rows, r1, 2 * lanes), F32),
        compiler_params=_cparams("parallel"),
        name="kfft",
    )(k3, w1, twr, twi, w2)


def _short_conv(a, w, b):
    p, g, r, l = a.shape
    a3 = a.reshape(p * g, r, l)
    lane = lax.broadcasted_iota(I32, a3.shape, 2)
    row = lax.broadcasted_iota(I32, a3.shape, 1)
    pl_ = pltpu.roll(a3, 1, 2)
    ql = pltpu.roll(pl_, 1, 1)
    prev = jnp.where(lane == 0, jnp.where(row == 0, 0.0, ql), pl_)
    pr = pltpu.roll(a3, l - 1, 2)
    qr = pltpu.roll(pr, r - 1, 1)
    nxt = jnp.where(lane == l - 1, jnp.where(row == r - 1, 0.0, qr), pr)
    prev, nxt = prev.reshape(a.shape), nxt.reshape(a.shape)
    return w[0][None] * prev + w[1][None] * a + w[2][None] * nxt + b[None]


def _hyena_kernel(x1_ref, x2_ref, v_ref, sw1_ref, sb1_ref, sw2_ref, sb2_ref, swv_ref, sbv_ref,
                  kf0_ref, kf1_ref, skip_ref, w1_ref, twr_ref, twi_ref, w2_ref, w3_ref,
                  twrt_ref, twit_ref, w4_ref, z_ref):
    x1 = _short_conv(x1_ref[...].astype(F32), sw1_ref[...], sb1_ref[...])
    x2 = _short_conv(x2_ref[...].astype(F32), sw2_ref[...], sb2_ref[...])
    z = _short_conv(v_ref[...].astype(F32), swv_ref[...], sbv_ref[...])
    rh = z.shape[2]
    for gate, kf_ref, o in ((x1, kf0_ref, 0), (x2, kf1_ref, 1)):
        s = jnp.concatenate([z[0], z[1]], axis=1)
        c = _fwd_steps(jnp.swapaxes(s, 1, 2), w1_ref[...], twr_ref[...], twi_ref[...], w2_ref[...])
        kf = kf_ref[...]
        n2 = c.shape[2] // 2
        cr, ci, kr, ki = c[:, :, :n2], c[:, :, n2:], kf[:, :, :n2], kf[:, :, n2:]
        d = jnp.concatenate([cr * kr - ci * ki, cr * ki + ci * kr], axis=-1)
        y = _inv_steps(d, w3_ref[...], twrt_ref[...], twit_ref[...], w4_ref[...])
        y = jnp.swapaxes(y, 1, 2)
        conv = jnp.stack([y[:, :rh], y[:, rh:]], axis=0)
        z = gate * (conv + skip_ref[o][None] * z)
    z_ref[...] = z


def _hyena(hy4, sw, sb, kf, skip, tabs, *, g):
    bsz, c3, rh, lanes = hy4.shape
    c = c3 // 3
    r1 = 2 * rh
    ncb = c // g
    w1, twr, twi, w2, w3, twrt, twit, w4 = tabs
    const2 = lambda i, p: (0, 0)
    data = lambda off: pl.BlockSpec((2, g, rh, lanes), lambda i, p, off=off: (p, off + i, 0, 0))
    wspec = lambda off: pl.BlockSpec((3, g, 1, lanes), lambda i, p, off=off: (0, off + i, 0, 0))
    bspec = lambda off: pl.BlockSpec((g, 1, lanes), lambda i, p, off=off: (off + i, 0, 0))
    kspec = lambda off: pl.BlockSpec((g, r1, 2 * lanes), lambda i, p, off=off: (off + i, 0, 0))
    return pl.pallas_call(
        _hyena_kernel,
        grid=(ncb, bsz // 2),
        in_specs=[data(0), data(ncb), data(2 * ncb),
                  wspec(0), bspec(0), wspec(ncb), bspec(ncb), wspec(2 * ncb), bspec(2 * ncb),
                  kspec(0), kspec(ncb),
                  pl.BlockSpec((2, g, 1, lanes), lambda i, p: (0, i, 0, 0)),
                  pl.BlockSpec(w1.shape, const2), pl.BlockSpec(twr.shape, const2),
                  pl.BlockSpec(twi.shape, const2), pl.BlockSpec(w2.shape, const2),
                  pl.BlockSpec(w3.shape, const2), pl.BlockSpec(twrt.shape, const2),
                  pl.BlockSpec(twit.shape, const2), pl.BlockSpec(w4.shape, const2)],
        out_specs=pl.BlockSpec((2, g, rh, lanes), lambda i, p: (p, i, 0, 0)),
        out_shape=jax.ShapeDtypeStruct((bsz, c, rh, lanes), F32),
        compiler_params=_cparams("parallel", "parallel"),
        name="hyena",
    )(hy4, hy4, hy4, sw, sb, sw, sb, sw, sb, kf, kf, skip, w1, twr, twi, w2, w3, twrt, twit, w4)


def _mix_kernel(ua_ref, z_ref, gate_ref, h_ref, wa_ref, wb_ref, wm_ref, g_ref, b_ref, o_ref, *, alpha, sub):
    for r in range(0, h_ref.shape[0], sub):
        rs = slice(r, r + sub)
        ya = _dot(ua_ref[rs, :], wa_ref[...])
        zc = jnp.concatenate([z_ref[:, k, :] for k in range(r // LANES, (r + sub) // LANES)], axis=-1)
        yb = _dot(jnp.transpose(zc).astype(BF16), wb_ref[...])
        gt = gate_ref[rs, :].astype(F32)
        d = ya.shape[1]
        m = gt[:, :d] * ya + gt[:, d:] * yb
        mixed = _dot(m.astype(BF16), wm_ref[...])
        o_ref[rs, :] = _ln(alpha * h_ref[rs, :] + mixed, g_ref[...], b_ref[...])


def _mix(ua, z4, gates, h, wa, wb, wm, g, b, *, tm, sub, alpha):
    bsz, length, d = h.shape
    c = ua.shape[2]
    sub = min(sub, tm)
    assert sub % LANES == 0 and ((tm // LANES) % SUBLANES == 0 or tm == length)
    const = lambda i, j: (0, 0)
    return pl.pallas_call(
        functools.partial(_mix_kernel, alpha=alpha, sub=sub),
        grid=(bsz, length // tm),
        in_specs=[pl.BlockSpec((None, tm, c), lambda i, j: (i, j, 0)),
                  pl.BlockSpec((None, c, tm // LANES, LANES), lambda i, j: (i, 0, j, 0)),
                  pl.BlockSpec((None, tm, 2 * d), lambda i, j: (i, j, 0)),
                  pl.BlockSpec((None, tm, d), lambda i, j: (i, j, 0)),
                  pl.BlockSpec((c, d), const), pl.BlockSpec((c, d), const), pl.BlockSpec((d, d), const),
                  pl.BlockSpec((1, d), const), pl.BlockSpec((1, d), const)],
        out_specs=pl.BlockSpec((None, tm, d), lambda i, j: (i, j, 0)),
        out_shape=jax.ShapeDtypeStruct((bsz, length, d), F32),
        compiler_params=_cparams("parallel", "parallel"),
        name="mix",
    )(ua, z4, gates, h, wa, wb, wm, g, b)


def _kv_kernel(mem_ref, wk_ref, wv_ref, k_ref, v_ref):
    mb = mem_ref[...].astype(BF16)
    k_ref[...] = _dot(mb, wk_ref[...]).astype(BF16)
    v_ref[...] = _dot(mb, wv_ref[...]).astype(BF16)


def _kv(mem, wk, wv):
    bsz, m, d = mem.shape
    const = lambda i: (0, 0)
    blk = pl.BlockSpec((None, m, d), lambda i: (i, 0, 0))
    return pl.pallas_call(
        _kv_kernel,
        grid=(bsz,),
        in_specs=[blk, pl.BlockSpec((d, d), const), pl.BlockSpec((d, d), const)],
        out_specs=[blk, blk],
        out_shape=[jax.ShapeDtypeStruct((bsz, m, d), BF16)] * 2,
        compiler_params=_cparams("parallel"),
        name="kv",
    )(mem, wk, wv)


def _xattn_kernel(x_ref, k_ref, v_ref, wq_ref, wo_ref, g_ref, b_ref, wrh_ref, wrl_ref,
                  x2_ref, x2b_ref, aff_ref, *, alpha, heads, sub):
    d = x_ref.shape[1]
    dh = d // heads
    wrh, wrl = wrh_ref[...], wrl_ref[...]
    for r in range(0, x_ref.shape[0], sub):
        rs = slice(r, r + sub)
        x = x_ref[rs, :]
        q = (_dot(x.astype(BF16), wq_ref[...]) * (dh ** -0.5)).astype(BF16)
        outs = []
        for hd in range(heads):
            sl = slice(hd * dh, (hd + 1) * dh)
            s = _dot_nt(q[:, sl], k_ref[:, sl])
            s = s - jnp.max(s, axis=-1, keepdims=True)
            p = jnp.exp(s)
            p = p / jnp.sum(p, axis=-1, keepdims=True)
            outs.append(_dot(p.astype(BF16), v_ref[:, sl]))
        o = jnp.concatenate(outs, axis=-1)
        xa = _dot(o.astype(BF16), wo_ref[...])
        x2 = _ln(alpha * x + xa, g_ref[...], b_ref[...])
        x2_ref[rs, :] = x2
        x2h, x2l = _split_bf16(x2)
        x2b_ref[rs, :] = x2h
        logits = _dot_nt(wrh, x2h) + (_dot_nt(wrh, x2l) + _dot_nt(wrl, x2h))
        logits = logits - jnp.max(logits, axis=0, keepdims=True)
        ex = jnp.exp(logits)
        aff_ref[:, rs] = ex / jnp.sum(ex, axis=0, keepdims=True)


def _xattn(x, k, v, wq, wo, g, b, wrh, wrl, *, tm, sub, alpha, heads):
    bsz, length, d = x.shape
    m = k.shape[1]
    e = wrh.shape[0]
    const = lambda i, j: (0, 0)
    tok = pl.BlockSpec((None, tm, d), lambda i, j: (i, j, 0))
    mem = pl.BlockSpec((None, m, d), lambda i, j: (i, 0, 0))
    return pl.pallas_call(
        functools.partial(_xattn_kernel, alpha=alpha, heads=heads, sub=min(sub, tm)),
        grid=(bsz, length // tm),
        in_specs=[tok, mem, mem, pl.BlockSpec((d, d), const), pl.BlockSpec((d, d), const),
                  pl.BlockSpec((1, d), const), pl.BlockSpec((1, d), const),
                  pl.BlockSpec((e, d), const), pl.BlockSpec((e, d), const)],
        out_specs=[tok, tok, pl.BlockSpec((None, e, tm), lambda i, j: (i, 0, j))],
        out_shape=[jax.ShapeDtypeStruct((bsz, length, d), F32),
                   jax.ShapeDtypeStruct((bsz, length, d), BF16),
                   jax.ShapeDtypeStruct((bsz, e, length), F32)],
        compiler_params=_cparams("parallel", "parallel"),
        name="xattn",
    )(x, k, v, wq, wo, g, b, wrh, wrl)


def _excl_cumsum_lanes(mask_f32, tri, write):
    e, t = mask_f32.shape
    carry = jnp.zeros((e, 1), F32)
    for c in range(t // LANES):
        m = mask_f32[:, c * LANES:(c + 1) * LANES]
        inc = _dot(m.astype(BF16), tri)
        write(c, inc - m + carry, carry)
        carry = carry + inc[:, LANES - 1:LANES]


def _select_kernel(aff_ref, tri_ref, slot_ref, slott_ref, ghl_ref, r0_ref, jlo_ref, nch_ref, cnt_ref, cum_ref,
                   *, cap, st, kct):
    a = aff_ref[...]
    e = a.shape[0]
    v = jnp.zeros((e, 1), I32)
    for bit in range(30, -1, -1):
        cand = v | (1 << bit)
        cnt = jnp.sum(jnp.where(a >= pltpu.bitcast(cand, F32), 1.0, 0.0), axis=1, keepdims=True)
        v = jnp.where(cnt >= cap, cand, v)
    thr = pltpu.bitcast(v, F32)
    gt = a > thr
    eq = a == thr
    need = cap - jnp.sum(jnp.where(gt, 1.0, 0.0), axis=1, keepdims=True)
    tri = tri_ref[...]

    def write_eq(c, val, before):
        cnt_ref[:, c * LANES:(c + 1) * LANES] = val
    _excl_cumsum_lanes(jnp.where(eq, 1.0, 0.0), tri, write_eq)
    sel = gt | (eq & (cnt_ref[...] < need))
    gate = jnp.where(sel, a, 0.0)

    n_tiles = a.shape[1] // LANES

    def write_rank(c, val, before):
        cnt_ref[:, c * LANES:(c + 1) * LANES] = val
        cum_ref[:, c:c + 1] = before
    _excl_cumsum_lanes(jnp.where(sel, 1.0, 0.0), tri, write_rank)
    slot = jnp.where(sel, cnt_ref[...], -1.0)
    slot_ref[...] = slot.astype(I32)
    slott_ref[...] = jnp.transpose(slot).astype(I32)
    g_hi = gate.astype(BF16).astype(F32)
    g_lo = (gate - g_hi).astype(BF16).astype(F32)
    ghl_ref[...] = jnp.transpose(jnp.concatenate([g_hi, g_lo], axis=0)).astype(BF16)

    before = cum_ref[:, 0:n_tiles]
    r0_ref[...] = before.astype(I32)
    after = jnp.concatenate([before[:, 1:], jnp.full((e, 1), cap, F32)], axis=1)
    for s in range(cap // st):
        lo = jnp.sum(jnp.where(after <= s * st, 1.0, 0.0), axis=1, keepdims=True)
        hi = jnp.sum(jnp.where(before < (s + 1) * st, 1.0, 0.0), axis=1, keepdims=True) - 1.0
        n = jnp.ones_like(lo)
        for m in range(1, -(-n_tiles // kct)):
            n = n + jnp.where(hi - lo >= m * kct, 1.0, 0.0)
        jlo_ref[:, s:s + 1] = lo.astype(I32)
        nch_ref[:, s:s + 1] = n.astype(I32)


def _select(aff, tri, *, cap, st, kct):
    bsz, e, t = aff.shape
    n_tiles, n_st = t // LANES, cap // st
    blk = pl.BlockSpec((None, e, t), lambda i: (i, 0, 0))
    small = lambda n: pl.BlockSpec((None, e, n), lambda i: (i, 0, 0))
    return pl.pallas_call(
        functools.partial(_select_kernel, cap=cap, st=st, kct=kct),
        grid=(bsz,),
        in_specs=[blk, pl.BlockSpec((LANES, LANES), lambda i: (0, 0))],
        out_specs=[blk, pl.BlockSpec((None, t, e), lambda i: (i, 0, 0)),
                   pl.BlockSpec((None, t, 2 * e), lambda i: (i, 0, 0)),
                   small(n_tiles), small(n_st), small(n_st)],
        out_shape=[jax.ShapeDtypeStruct((bsz, e, t), I32), jax.ShapeDtypeStruct((bsz, t, e), I32),
                   jax.ShapeDtypeStruct((bsz, t, 2 * e), BF16),
                   jax.ShapeDtypeStruct((bsz, e, n_tiles), I32),
                   jax.ShapeDtypeStruct((bsz, e, n_st), I32), jax.ShapeDtypeStruct((bsz, e, n_st), I32)],
        scratch_shapes=[pltpu.VMEM((e, t), F32), pltpu.VMEM((e, LANES), F32)],
        compiler_params=_cparams("parallel"),
        name="select",
    )(aff, tri)


def _window_start(r0, cap, win, align):
    return pl.multiple_of(jnp.minimum((r0 // align) * align, cap - win), align)


def _dispatch_kernel(jlo_ref, nch_ref, x_ref, slot_ref, ghl_ref, o_ref, gs_ref, acc_ref, gacc_ref,
                     *, st, kc, tt):
    b, e = pl.program_id(0), pl.program_id(1)
    t_total = x_ref.shape[0]
    n_e = ghl_ref.shape[1] // 2
    for s in range(o_ref.shape[0] // st):
        j0 = jlo_ref[b, e, s]

        def chunk(c, s=s, j0=j0):
            want = (j0 + c * (kc // tt)) * tt
            t0 = pl.multiple_of(jnp.minimum(want, t_total - kc), tt)
            tok = t0 + lax.broadcasted_iota(I32, (1, kc), 1)
            rel = jnp.where(tok >= want, slot_ref[:, pl.ds(t0, kc)] - s * st, -1)
            onehot = jnp.where(lax.broadcasted_iota(I32, (st, kc), 0) == rel, 1.0, 0.0).astype(BF16)
            xs = _dot(onehot, x_ref[pl.ds(t0, kc), :])
            gs = _dot(onehot, ghl_ref[pl.ds(t0, kc), :])
            return xs, gs

        xs, gs = chunk(0)
        acc_ref[...] = xs
        gacc_ref[...] = gs

        def body(c, carry):
            xs, gs = chunk(c)
            acc_ref[...] += xs
            gacc_ref[...] += gs
            return carry
        lax.fori_loop(1, nch_ref[b, e, s], body, 0)
        o_ref[s * st:(s + 1) * st, :] = acc_ref[...].astype(o_ref.dtype)
        lane = lax.broadcasted_iota(I32, (st, 2 * n_e), 1)
        mine = (lane == e) | (lane == e + n_e)
        gs_ref[s * st:(s + 1) * st, :] = jnp.sum(jnp.where(mine, gacc_ref[...], 0.0), axis=1, keepdims=True)


def _dispatch(jlo, nch, xb, slot4, gate_hl, *, cap, st, kc, tt):
    bsz, t, d = xb.shape
    e = slot4.shape[1]
    assert cap % st == 0 and kc % tt == 0 and t % tt == 0 and kc <= t
    return pl.pallas_call(
        functools.partial(_dispatch_kernel, st=st, kc=kc, tt=tt),
        grid_spec=pltpu.PrefetchScalarGridSpec(
            num_scalar_prefetch=2,
            grid=(bsz, e),
            in_specs=[pl.BlockSpec((None, t, d), lambda i, j, a, c: (i, 0, 0)),
                      pl.BlockSpec((None, None, 1, t), lambda i, j, a, c: (i, j, 0, 0)),
                      pl.BlockSpec((None, t, 2 * e), lambda i, j, a, c: (i, 0, 0))],
            out_specs=[pl.BlockSpec((None, None, cap, d), lambda i, j, a, c: (i, j, 0, 0)),
                       pl.BlockSpec((None, None, cap, 1), lambda i, j, a, c: (i, j, 0, 0))],
            scratch_shapes=[pltpu.VMEM((st, d), F32), pltpu.VMEM((st, 2 * e), F32)]),
        out_shape=[jax.ShapeDtypeStruct((bsz, e, cap, d), BF16),
                   jax.ShapeDtypeStruct((bsz, e, cap, 1), F32)],
        compiler_params=_cparams("parallel", "parallel"),
        name="dispatch",
    )(jlo, nch, xb, slot4, gate_hl)


def _expert_kernel(x_ref, gs_ref, wg_ref, wu_ref, wd_ref, o_ref, acc_ref):
    f = pl.program_id(2)
    wg = wg_ref[...].astype(BF16)
    wu = wu_ref[...].astype(BF16)
    wd = wd_ref[...].astype(BF16)
    @pl.when(f == 0)
    def _():
        acc_ref[...] = jnp.zeros_like(acc_ref)

    for bi in range(x_ref.shape[0]):
        x = x_ref[bi]
        hg = _dot(x, wg)
        hu = _dot(x, wu)
        act = (hg * _sigmoid(hg) * hu).astype(BF16)
        acc_ref[bi] += _dot(act, wd)

    @pl.when(f == pl.num_programs(2) - 1)
    def _():
        o_ref[...] = (acc_ref[...] * gs_ref[...]).astype(o_ref.dtype)


def _expert(xg, gs, wg, wu, wd, *, fc, nb):
    bsz, e, cap, d = xg.shape
    f = wg.shape[2]
    assert bsz % nb == 0 and f % fc == 0
    tok = lambda last: pl.BlockSpec((nb, None, cap, last), lambda i, j, k: (j, i, 0, 0))
    return pl.pallas_call(
        _expert_kernel,
        grid=(e, bsz // nb, f // fc),
        in_specs=[tok(d), tok(1),
                  pl.BlockSpec((None, d, fc), lambda i, j, k: (i, 0, k)),
                  pl.BlockSpec((None, d, fc), lambda i, j, k: (i, 0, k)),
                  pl.BlockSpec((None, fc, d), lambda i, j, k: (i, k, 0))],
        out_specs=tok(d),
        out_shape=jax.ShapeDtypeStruct((bsz, e, cap, d), BF16),
        scratch_shapes=[pltpu.VMEM((nb, cap, d), F32)],
        compiler_params=_cparams("parallel", "parallel", "arbitrary"),
        name="expert",
    )(xg, gs, wg, wu, wd)


def _combine_kernel(r0_ref, x_ref, slot_ref, g_ref, b_ref, y_ref, o_ref, *, tt, win, align, alpha):
    b, j = pl.program_id(0), pl.program_id(1)
    n_e, cap = y_ref.shape[0], y_ref.shape[1]
    acc = alpha * x_ref[...]
    slots = slot_ref[...]
    for e in range(n_e):
        start = _window_start(r0_ref[b, e, j], cap, win, align)
        rel = slots[:, e:e + 1] - start
        onehot = jnp.where(lax.broadcasted_iota(I32, (tt, win), 1) == rel, 1.0, 0.0).astype(BF16)
        acc = acc + _dot(onehot, y_ref[e, pl.ds(start, win), :])
    o_ref[...] = _ln(acc, g_ref[...], b_ref[...])


def _combine(r0, x, slot_t, g, b, y, *, tt, win, align, alpha):
    bsz, t, d = x.shape
    e, cap = y.shape[1], y.shape[2]
    assert win >= tt + align and win <= cap and (cap - win) % align == 0
    const = lambda i, j, r: (0, 0)
    return pl.pallas_call(
        functools.partial(_combine_kernel, tt=tt, win=win, align=align, alpha=alpha),
        grid_spec=pltpu.PrefetchScalarGridSpec(
            num_scalar_prefetch=1,
            grid=(bsz, t // tt),
            in_specs=[pl.BlockSpec((None, tt, d), lambda i, j, r: (i, j, 0)),
                      pl.BlockSpec((None, tt, e), lambda i, j, r: (i, j, 0)),
                      pl.BlockSpec((1, d), const), pl.BlockSpec((1, d), const),
                      pl.BlockSpec((None, e, cap, d), lambda i, j, r: (i, 0, 0, 0),
                                   pipeline_mode=pl.Buffered(1))],
            out_specs=pl.BlockSpec((None, tt, d), lambda i, j, r: (i, j, 0))),
        out_shape=jax.ShapeDtypeStruct((bsz, t, d), F32),
        compiler_params=_cparams("parallel", "parallel"),
        name="combine",
    )(r0, x, slot_t, g, b, y)


def _hyena_constants(length):
    n = 2 * length
    m = np.arange(n)
    pos = np.where(m < length, m, n - m).astype(np.float64)
    pos[length] = 0.0
    t = pos / max(length - 1, 1)
    f = np.linspace(1e-4, HY_BANDS - 1, HY_BANDS)
    ang = (2.0 * np.pi * pos / length)[:, None] * f[None, :]
    feat = np.concatenate([t[:, None], np.cos(ang), -np.sin(ang)], axis=-1)
    feat_t = np.zeros((LANES, n), np.float32)
    feat_t[:feat.shape[1]] = feat.T
    valid = np.ones((1, n), np.float32)
    valid[0, length] = 0.0
    return feat_t, t[None, :].astype(np.float32), valid


def _decay_rates(width):
    max_decay = math.log(HY_DECAY_TARGET) / HY_FAST_DECAY
    min_decay = math.log(HY_DECAY_TARGET) / HY_SLOW_DECAY
    return np.abs(np.linspace(min_decay, max_decay, width, dtype=np.float32)).astype(np.float32)


def _dft_constants(r1):
    n, a1h, a2, at = _dft_tables(r1, r1 // 2)
    _, a1f, _, _ = _dft_tables(r1, r1)
    bf = lambda x: jnp.asarray(x, F32).astype(BF16)
    w1_data = bf(_cplx_rhs(a1h, -1.0))
    w1_filt = bf(np.concatenate([np.cos(a1f), -np.sin(a1f)], axis=1))
    w2 = bf(_cplx_rhs(a2, -1.0))
    w3 = bf(_cplx_rhs(a2.T, 1.0))
    w4 = bf(_cplx_rhs(a1h.T, 1.0) / n)
    twr, twi = np.cos(at), -np.sin(at)
    f32 = lambda x: jnp.asarray(x, F32)
    return dict(w1_data=w1_data, w1_filt=w1_filt, w2=w2, w3=w3, w4=w4,
                twr=f32(twr), twi=f32(twi), twr_t=f32(twr.T), twi_t=f32(twi.T))


def kernel(x, mem, ln_in_g, ln_in_b, w_in, b_gate, conf_dw_w, conf_dw_b, conf_ln_g, conf_ln_b, conf_w_out, hy_short_w, hy_short_b, hy_ffn_w1, hy_ffn_b1, hy_freq1, hy_ffn_w2, hy_ffn_b2, hy_freq2, hy_ffn_w3, hy_skip, hy_w_out, w_mix_out, ln_mix_g, ln_mix_b, xa_wq, xa_wk, xa_wv, xa_wo, ln_xa_g, ln_xa_b, moe_w_router, moe_w_gate, moe_w_up, moe_w_down, ln_moe_g, ln_moe_b):
    bsz, length, d = x.shape
    depth = w_in.shape[0]
    alpha = (2.0 * depth) ** 0.25
    cw = conf_dw_w.shape[2]
    hw = hy_skip.shape[2]
    n_glu, n_hy = 2 * cw, 3 * hw
    n_exp = moe_w_router.shape[2]
    cap = max(1, N_EXPERT_CAPACITY * length // n_exp)
    rh = length // LANES
    r1 = 2 * rh
    assert bsz % 2 == 0 and length % LANES == 0

    row = lambda v: v.reshape(1, -1).astype(F32)
    col = lambda v: v.reshape(-1, 1).astype(F32)

    feat_t, t_ext, valid = _hyena_constants(length)
    dft = _dft_constants(r1)
    rates = _decay_rates(hw)
    tri = jnp.asarray(np.triu(np.ones((LANES, LANES), np.float32))).astype(BF16)

    n_ord = hy_skip.shape[1]
    tm = min(512, length)
    lane_b = lambda v: jnp.broadcast_to(v[..., None, None], v.shape + (1, LANES)).astype(F32)
    dft_data = (dft["w1_data"], dft["twr"], dft["twi"], dft["w2"], dft["w3"], dft["twr_t"], dft["twi_t"],
                dft["w4"])

    h = x
    for i in range(depth):
        w1h, w1l = _split_bf16(jnp.pad(hy_ffn_w1[i].T, ((0, 0), (0, LANES - hy_ffn_w1.shape[1]))))
        w2h, w2l = _split_bf16(hy_ffn_w2[i].T)
        h2t = _filt_ffn(jnp.asarray(feat_t), w1h, w1l, col(hy_ffn_b1[i]), col(hy_freq1[i]),
                        w2h, w2l, col(hy_ffn_b2[i]), col(hy_freq2[i]), tn=min(2048, 2 * length))
        w3 = hy_ffn_w3[i].reshape(-1, n_ord, 2, hw)
        w3f = jnp.transpose(w3[:, :, 0], (1, 2, 0)).reshape(n_ord * hw, -1)
        w3b = jnp.transpose(w3[:, :, 1], (1, 2, 0)).reshape(n_ord * hw, -1)
        w3fh, w3fl = _split_bf16(w3f)
        w3bh, w3bl = _split_bf16(w3b)
        delta = jnp.asarray(np.tile(rates, n_ord).reshape(-1, 1))
        kern = _filt(h2t, w3fh, w3fl, w3bh, w3bl, delta, jnp.asarray(t_ext), jnp.asarray(valid),
                     g=min(128, hw))
        kf = _kfft(kern.reshape(n_ord * hw, r1, LANES), dft["w1_filt"], dft["twr"], dft["twi"], dft["w2"],
                   g=min(16, hw))

        wi = w_in[i]
        wglu = wi[:, :n_glu].astype(BF16)
        whyt = wi[:, n_glu:n_glu + n_hy].T.astype(BF16)
        wgate = wi[:, n_glu + n_hy:].astype(BF16)
        hn, u, hyt, gates = _inproj(h, row(ln_in_g), row(ln_in_b), wglu, whyt, wgate, row(b_gate[i]),
                                    tm=tm, apply_ln=(i == 0))

        ua = _conf(u, conf_dw_w[i], row(conf_dw_b[i]), row(conf_ln_g[i]), row(conf_ln_b[i]),
                   tl=min(512, length))

        z = _hyena(hyt.reshape(bsz, n_hy, rh, LANES), lane_b(hy_short_w[i]), lane_b(hy_short_b[i]), kf,
                   lane_b(hy_skip[i]), dft_data, g=min(16, hw))

        x1 = _mix(ua, z, gates, hn, conf_w_out[i].astype(BF16),
                  hy_w_out[i].astype(BF16), w_mix_out[i].astype(BF16), row(ln_mix_g[i]), row(ln_mix_b[i]),
                  tm=min(1024, length), sub=512, alpha=alpha)

        k, v = _kv(mem, xa_wk[i].astype(BF16), xa_wv[i].astype(BF16))
        wrh, wrl = _split_bf16(moe_w_router[i].T)
        x2, x2b, aff = _xattn(x1, k, v, xa_wq[i].astype(BF16), xa_wo[i].astype(BF16), row(ln_xa_g[i]),
                              row(ln_xa_b[i]), wrh, wrl, tm=min(1024, length), sub=512, alpha=alpha,
                              heads=XA_HEADS)

        tt, st, align = LANES, LANES, 16
        kc = min(10 * tt, length)
        slot, slot_t, gate_hl, r0, jlo, nch = _select(aff, tri, cap=cap, st=st, kct=kc // tt)
        xg, gslot = _dispatch(jlo, nch, x2b, slot.reshape(bsz, n_exp, 1, length), gate_hl,
                              cap=cap, st=st, kc=kc, tt=tt)
        y = _expert(xg, gslot, moe_w_gate[i], moe_w_up[i], moe_w_down[i],
                    fc=min(512, moe_w_gate.shape[3]), nb=2)
        h = _combine(r0, x2, slot_t, row(ln_moe_g[i]), row(ln_moe_b[i]), y,
                     tt=tt, win=tt + align, align=align, alpha=alpha)
    return h
```

```python
import functools
import math

import numpy as np
import jax
import jax.numpy as jnp
from jax import lax
from jax.experimental import pallas as pl
from jax.experimental.pallas import tpu as pltpu

F32 = jnp.float32
BF16 = jnp.bfloat16
I32 = jnp.int32

LANES = 128
SUBLANES = 8
VMEM_LIMIT_BYTES = 56 * 1024 * 1024
SUB_ROWS = 256

LN_EPS = 1e-5
XA_HEADS = 4
N_EXPERT_CAPACITY = 2
HY_DECAY_TARGET = 1e-2
HY_FAST_DECAY = 0.3
HY_SLOW_DECAY = 1.5
HY_BANDS = 16


def _cparams(*sem):
    return pltpu.CompilerParams(dimension_semantics=sem, vmem_limit_bytes=VMEM_LIMIT_BYTES)


def _ln(x, g, b):
    mu = jnp.mean(x, axis=-1, keepdims=True)
    xc = x - mu
    var = jnp.mean(xc * xc, axis=-1, keepdims=True)
    return xc * lax.rsqrt(var + LN_EPS) * g + b


def _sigmoid(x):
    return 1.0 / (1.0 + jnp.exp(-x))


def _dot(a, b):
    return jnp.dot(a, b, preferred_element_type=F32)


def _dot_nt(a, b):
    return lax.dot_general(a, b, (((1,), (1,)), ((), ())), preferred_element_type=F32)


def _split_bf16(x):
    hi = x.astype(BF16)
    lo = (x - hi.astype(F32)).astype(BF16)
    return hi, lo


def _dot3(a, b_hi, b_lo):
    a_hi, a_lo = _split_bf16(a)
    return _dot(a_hi, b_hi) + (_dot(a_hi, b_lo) + _dot(a_lo, b_hi))


def _dot3_lhs(a_hi, a_lo, b):
    b_hi, b_lo = _split_bf16(b)
    return _dot(a_hi, b_hi) + (_dot(a_hi, b_lo) + _dot(a_lo, b_hi))


def _inproj_kernel(x_ref, g_ref, b_ref, wglu_ref, whyt_ref, wgate_ref, bgate_ref,
                   h_ref, u_ref, hyt_ref, gate_ref, *, apply_ln, sub):
    for r in range(0, x_ref.shape[0], sub):
        rs = slice(r, r + sub)
        h = _ln(x_ref[rs, :], g_ref[...], b_ref[...]) if apply_ln else x_ref[rs, :]
        h_ref[rs, :] = h
        hb = h.astype(BF16)
        glu = _dot(hb, wglu_ref[...])
        cw = glu.shape[1] // 2
        u_ref[rs, :] = glu[:, :cw] * _sigmoid(glu[:, cw:])
        hyt_ref[:, rs] = _dot_nt(whyt_ref[...], hb).astype(hyt_ref.dtype)
        gl = _dot(hb, wgate_ref[...]) + bgate_ref[...]
        gate_ref[rs, :] = _sigmoid(gl).astype(BF16)


def _inproj(x, g, b, wglu, whyt, wgate, bgate, *, tm, apply_ln):
    bsz, length, d = x.shape
    n_glu, n_hy, n_gate = wglu.shape[1], whyt.shape[0], wgate.shape[1]
    const = lambda i, j: (0, 0)
    return pl.pallas_call(
        functools.partial(_inproj_kernel, apply_ln=apply_ln, sub=min(SUB_ROWS, tm)),
        grid=(bsz, length // tm),
        in_specs=[
            pl.BlockSpec((None, tm, d), lambda i, j: (i, j, 0)),
            pl.BlockSpec((1, d), const), pl.BlockSpec((1, d), const),
            pl.BlockSpec((d, n_glu), const),
            pl.BlockSpec((n_hy, d), const),
            pl.BlockSpec((d, n_gate), const),
            pl.BlockSpec((1, n_gate), const),
        ],
        out_specs=[
            pl.BlockSpec((None, tm, d), lambda i, j: (i, j, 0)),
            pl.BlockSpec((None, tm, n_glu // 2), lambda i, j: (i, j, 0)),
            pl.BlockSpec((None, n_hy, tm), lambda i, j: (i, 0, j)),
            pl.BlockSpec((None, tm, n_gate), lambda i, j: (i, j, 0)),
        ],
        out_shape=[
            jax.ShapeDtypeStruct((bsz, length, d), F32),
            jax.ShapeDtypeStruct((bsz, length, n_glu // 2), F32),
            jax.ShapeDtypeStruct((bsz, n_hy, length), BF16),
            jax.ShapeDtypeStruct((bsz, length, n_gate), BF16),
        ],
        compiler_params=_cparams("parallel", "parallel"),
        name="inproj",
    )(x, g, b, wglu, whyt, wgate, bgate)


def _conf_kernel(prev_ref, cur_ref, next_ref, w_ref, cb_ref, g_ref, b_ref, o_ref, ext_ref, sh_ref,
                 *, tl, halo, ksize, rows):
    j = pl.program_id(1)
    nj = pl.num_programs(1)
    pad = (ksize - 1) // 2
    ext_ref[0:halo, :] = jnp.where(j > 0, prev_ref[...], 0.0)
    ext_ref[halo:halo + tl, :] = cur_ref[...]
    ext_ref[halo + tl:halo + tl + halo, :] = jnp.where(j < nj - 1, next_ref[...], 0.0)
    n_sh = sh_ref.shape[1]
    for r in range(1, SUBLANES):
        sh_ref[r - 1, :, :] = ext_ref[r:r + n_sh, :]
    w = w_ref[...]
    cb, g, b = cb_ref[...], g_ref[...], b_ref[...]
    for r0 in range(0, tl, rows):
        acc = jnp.zeros((rows, w.shape[1]), F32)
        for k in range(ksize):
            q, r = divmod(halo - pad + k + r0, SUBLANES)
            q *= SUBLANES
            src = ext_ref[q:q + rows, :] if r == 0 else sh_ref[r - 1, q:q + rows, :]
            acc = acc + w[k:k + 1, :] * src
        y = _ln(acc + cb, g, b)
        o_ref[r0:r0 + rows, :] = (y * _sigmoid(y)).astype(o_ref.dtype)


def _conf(u, w, cb, g, b, *, tl, halo=16, rows=64):
    bsz, length, c = u.shape
    ksize = w.shape[0]
    assert (ksize - 1) // 2 <= halo and tl % halo == 0 and tl % rows == 0
    nh = tl // halo
    last = length // halo - 1
    const = lambda i, j: (0, 0)
    return pl.pallas_call(
        functools.partial(_conf_kernel, tl=tl, halo=halo, ksize=ksize, rows=rows),
        grid=(bsz, length // tl),
        in_specs=[
            pl.BlockSpec((None, halo, c), lambda i, j: (i, jnp.maximum(j * nh - 1, 0), 0)),
            pl.BlockSpec((None, tl, c), lambda i, j: (i, j, 0)),
            pl.BlockSpec((None, halo, c), lambda i, j: (i, jnp.minimum((j + 1) * nh, last), 0)),
            pl.BlockSpec((ksize, c), const),
            pl.BlockSpec((1, c), const), pl.BlockSpec((1, c), const), pl.BlockSpec((1, c), const),
        ],
        out_specs=pl.BlockSpec((None, tl, c), lambda i, j: (i, j, 0)),
        out_shape=jax.ShapeDtypeStruct((bsz, length, c), BF16),
        scratch_shapes=[pltpu.VMEM((tl + 2 * halo, c), F32),
                        pltpu.VMEM((SUBLANES - 1, tl + 2 * halo - SUBLANES, c), F32)],
        compiler_params=_cparams("parallel", "parallel"),
        name="conf",
    )(u, u, u, w, cb, g, b)


def _filt_ffn_kernel(feat_ref, w1h_ref, w1l_ref, b1_ref, f1_ref, w2h_ref, w2l_ref, b2_ref, f2_ref, o_ref):
    a = _dot3_lhs(w1h_ref[...], w1l_ref[...], feat_ref[...]) + b1_ref[...]
    h = jnp.sin(f1_ref[...] * a)
    a2 = _dot3_lhs(w2h_ref[...], w2l_ref[...], h) + b2_ref[...]
    o_ref[...] = jnp.sin(f2_ref[...] * a2)


def _filt_ffn(feat_t, w1h, w1l, b1, f1, w2h, w2l, b2, f2, *, tn):
    kp, n = feat_t.shape
    m = w1h.shape[0]
    const = lambda i: (0, 0)
    return pl.pallas_call(
        _filt_ffn_kernel,
        grid=(n // tn,),
        in_specs=[pl.BlockSpec((kp, tn), lambda i: (0, i)),
                  pl.BlockSpec((m, kp), const), pl.BlockSpec((m, kp), const),
                  pl.BlockSpec((m, 1), const), pl.BlockSpec((m, 1), const),
                  pl.BlockSpec((m, m), const), pl.BlockSpec((m, m), const),
                  pl.BlockSpec((m, 1), const), pl.BlockSpec((m, 1), const)],
        out_specs=pl.BlockSpec((m, tn), lambda i: (0, i)),
        out_shape=jax.ShapeDtypeStruct((m, n), F32),
        compiler_params=_cparams("parallel"),
        name="filt_ffn",
    )(feat_t, w1h, w1l, b1, f1, w2h, w2l, b2, f2)


def _filt_kernel(h2_ref, w3fh_ref, w3fl_ref, w3bh_ref, w3bl_ref, delta_ref, text_ref, valid_ref, o_ref):
    n = h2_ref.shape[1]
    half = n // 2
    decay = jnp.exp(-text_ref[...] * delta_ref[...]) * valid_ref[...]
    kf = _dot3_lhs(w3fh_ref[...], w3fl_ref[...], h2_ref[:, 0:half]) * decay[:, 0:half]
    kb = _dot3_lhs(w3bh_ref[...], w3bl_ref[...], h2_ref[:, half:n]) * decay[:, half:n]
    ss = jnp.sum(kf * kf, axis=1, keepdims=True) + jnp.sum(kb * kb, axis=1, keepdims=True)
    scale = lax.rsqrt(ss + 1e-6)
    o_ref[:, 0:half] = kf * scale
    o_ref[:, half:n] = kb * scale


def _filt(h2t, w3fh, w3fl, w3bh, w3bl, delta, text, valid, *, g):
    m, n = h2t.shape
    rows = w3fh.shape[0]
    const = lambda i: (0, 0)
    return pl.pallas_call(
        _filt_kernel,
        grid=(rows // g,),
        in_specs=[pl.BlockSpec((m, n), const),
                  pl.BlockSpec((g, m), lambda i: (i, 0)), pl.BlockSpec((g, m), lambda i: (i, 0)),
                  pl.BlockSpec((g, m), lambda i: (i, 0)), pl.BlockSpec((g, m), lambda i: (i, 0)),
                  pl.BlockSpec((g, 1), lambda i: (i, 0)),
                  pl.BlockSpec((1, n), const), pl.BlockSpec((1, n), const)],
        out_specs=pl.BlockSpec((g, n), lambda i: (i, 0)),
        out_shape=jax.ShapeDtypeStruct((rows, n), F32),
        compiler_params=_cparams("parallel"),
        name="filt",
    )(h2t, w3fh, w3fl, w3bh, w3bl, delta, text, valid)


def _dft_tables(r1, n1_used):
    n = r1 * LANES
    a1 = 2.0 * np.pi * np.outer(np.arange(n1_used), np.arange(r1)) / r1
    a2 = 2.0 * np.pi * np.outer(np.arange(LANES), np.arange(LANES)) / LANES
    at = 2.0 * np.pi * np.outer(np.arange(LANES), np.arange(r1)) / n
    return n, a1, a2, at


def _cplx_rhs(ang, sign):
    c, s = np.cos(ang), sign * np.sin(ang)
    return np.block([[c, s], [-s, c]])


def _fwd_steps(x_t, w1, twr, twi, w2):
    g, n2, k1w = x_t.shape
    r1 = twr.shape[1]
    a = _dot(x_t.reshape(g * n2, k1w).astype(BF16), w1).reshape(g, n2, 2 * r1)
    ar, ai = a[:, :, :r1], a[:, :, r1:]
    br = ar * twr - ai * twi
    bi = ar * twi + ai * twr
    bt = jnp.concatenate([jnp.swapaxes(br.astype(BF16), 1, 2), jnp.swapaxes(bi.astype(BF16), 1, 2)],
                         axis=-1)
    c = _dot(bt.reshape(g * r1, 2 * n2), w2)
    return c.reshape(g, r1, 2 * n2)


def _inv_steps(d, w3, twr_t, twi_t, w4):
    g, r1, w = d.shape
    n2 = w // 2
    e = _dot(d.reshape(g * r1, w).astype(BF16), w3).reshape(g, r1, w)
    er, ei = e[:, :, :n2], e[:, :, n2:]
    fr = er * twr_t + ei * twi_t
    fi = ei * twr_t - er * twi_t
    ft = jnp.concatenate([jnp.swapaxes(fr.astype(BF16), 1, 2), jnp.swapaxes(fi.astype(BF16), 1, 2)],
                         axis=-1)
    y = _dot(ft.reshape(g * n2, 2 * r1), w4)
    return y.reshape(g, n2, w4.shape[1])


def _kfft_kernel(k_ref, w1_ref, twr_ref, twi_ref, w2_ref, o_ref):
    x_t = jnp.swapaxes(k_ref[...].astype(BF16), 1, 2)
    o_ref[...] = _fwd_steps(x_t, w1_ref[...], twr_ref[...], twi_ref[...], w2_ref[...])


def _kfft(k3, w1, twr, twi, w2, *, g):
    rows, r1, lanes = k3.shape
    const = lambda i: (0, 0)
    return pl.pallas_call(
        _kfft_kernel,
        grid=(rows // g,),
        in_specs=[pl.BlockSpec((g, r1, lanes), lambda i: (i, 0, 0)),
                  pl.BlockSpec(w1.shape, const), pl.BlockSpec(twr.shape, const),
                  pl.BlockSpec(twi.shape, const), pl.BlockSpec(w2.shape, const)],
        out_specs=pl.BlockSpec((g, r1, 2 * lanes), lambda i: (i, 0, 0)),
        out_shape=jax.ShapeDtypeStruct((rows, r1, 2 * lanes), F32),
        compiler_params=_cparams("parallel"),
        name="kfft",
    )(k3, w1, twr, twi, w2)


def _short_conv(a, w, b):
    p, g, r, l = a.shape
    a3 = a.reshape(p * g, r, l)
    lane = lax.broadcasted_iota(I32, a3.shape, 2)
    row = lax.broadcasted_iota(I32, a3.shape, 1)
    pl_ = pltpu.roll(a3, 1, 2)
    ql = pltpu.roll(pl_, 1, 1)
    prev = jnp.where(lane == 0, jnp.where(row == 0, 0.0, ql), pl_)
    pr = pltpu.roll(a3, l - 1, 2)
    qr = pltpu.roll(pr, r - 1, 1)
    nxt = jnp.where(lane == l - 1, jnp.where(row == r - 1, 0.0, qr), pr)
    prev, nxt = prev.reshape(a.shape), nxt.reshape(a.shape)
    return w[0][None] * prev + w[1][None] * a + w[2][None] * nxt + b[None]


def _hyena_kernel(x1_ref, x2_ref, v_ref, sw1_ref, sb1_ref, sw2_ref, sb2_ref, swv_ref, sbv_ref,
                  kf0_ref, kf1_ref, skip_ref, w1_ref, twr_ref, twi_ref, w2_ref, w3_ref,
                  twrt_ref, twit_ref, w4_ref, z_ref):
    x1 = _short_conv(x1_ref[...].astype(F32), sw1_ref[...], sb1_ref[...])
    x2 = _short_conv(x2_ref[...].astype(F32), sw2_ref[...], sb2_ref[...])
    z = _short_conv(v_ref[...].astype(F32), swv_ref[...], sbv_ref[...])
    rh = z.shape[2]
    for gate, kf_ref, o in ((x1, kf0_ref, 0), (x2, kf1_ref, 1)):
        s = jnp.concatenate([z[0], z[1]], axis=1)
        c = _fwd_steps(jnp.swapaxes(s, 1, 2), w1_ref[...], twr_ref[...], twi_ref[...], w2_ref[...])
        kf = kf_ref[...]
        n2 = c.shape[2] // 2
        cr, ci, kr, ki = c[:, :, :n2], c[:, :, n2:], kf[:, :, :n2], kf[:, :, n2:]
        d = jnp.concatenate([cr * kr - ci * ki, cr * ki + ci * kr], axis=-1)
        y = _inv_steps(d, w3_ref[...], twrt_ref[...], twit_ref[...], w4_ref[...])
        y = jnp.swapaxes(y, 1, 2)
        conv = jnp.stack([y[:, :rh], y[:, rh:]], axis=0)
        z = gate * (conv + skip_ref[o][None] * z)
    z_ref[...] = z


def _hyena(hy4, sw, sb, kf, skip, tabs, *, g):
    bsz, c3, rh, lanes = hy4.shape
    c = c3 // 3
    r1 = 2 * rh
    ncb = c // g
    w1, twr, twi, w2, w3, twrt, twit, w4 = tabs
    const2 = lambda i, p: (0, 0)
    data = lambda off: pl.BlockSpec((2, g, rh, lanes), lambda i, p, off=off: (p, off + i, 0, 0))
    wspec = lambda off: pl.BlockSpec((3, g, 1, lanes), lambda i, p, off=off: (0, off + i, 0, 0))
    bspec = lambda off: pl.BlockSpec((g, 1, lanes), lambda i, p, off=off: (off + i, 0, 0))
    kspec = lambda off: pl.BlockSpec((g, r1, 2 * lanes), lambda i, p, off=off: (off + i, 0, 0))
    return pl.pallas_call(
        _hyena_kernel,
        grid=(ncb, bsz // 2),
        in_specs=[data(0), data(ncb), data(2 * ncb),
                  wspec(0), bspec(0), wspec(ncb), bspec(ncb), wspec(2 * ncb), bspec(2 * ncb),
                  kspec(0), kspec(ncb),
                  pl.BlockSpec((2, g, 1, lanes), lambda i, p: (0, i, 0, 0)),
                  pl.BlockSpec(w1.shape, const2), pl.BlockSpec(twr.shape, const2),
                  pl.BlockSpec(twi.shape, const2), pl.BlockSpec(w2.shape, const2),
                  pl.BlockSpec(w3.shape, const2), pl.BlockSpec(twrt.shape, const2),
                  pl.BlockSpec(twit.shape, const2), pl.BlockSpec(w4.shape, const2)],
        out_specs=pl.BlockSpec((2, g, rh, lanes), lambda i, p: (p, i, 0, 0)),
        out_shape=jax.ShapeDtypeStruct((bsz, c, rh, lanes), F32),
        compiler_params=_cparams("parallel", "parallel"),
        name="hyena",
    )(hy4, hy4, hy4, sw, sb, sw, sb, sw, sb, kf, kf, skip, w1, twr, twi, w2, w3, twrt, twit, w4)


def _mix_kernel(ua_ref, z_ref, gate_ref, h_ref, wa_ref, wb_ref, wm_ref, g_ref, b_ref, o_ref, *, alpha, sub):
    for r in range(0, h_ref.shape[0], sub):
        rs = slice(r, r + sub)
        ya = _dot(ua_ref[rs, :], wa_ref[...])
        zc = jnp.concatenate([z_ref[:, k, :] for k in range(r // LANES, (r + sub) // LANES)], axis=-1)
        yb = _dot(jnp.transpose(zc).astype(BF16), wb_ref[...])
        gt = gate_ref[rs, :].astype(F32)
        d = ya.shape[1]
        m = gt[:, :d] * ya + gt[:, d:] * yb
        mixed = _dot(m.astype(BF16), wm_ref[...])
        o_ref[rs, :] = _ln(alpha * h_ref[rs, :] + mixed, g_ref[...], b_ref[...])


def _mix(ua, z4, gates, h, wa, wb, wm, g, b, *, tm, sub, alpha):
    bsz, length, d = h.shape
    c = ua.shape[2]
    sub = min(sub, tm)
    assert sub % LANES == 0 and ((tm // LANES) % SUBLANES == 0 or tm == length)
    const = lambda i, j: (0, 0)
    return pl.pallas_call(
        functools.partial(_mix_kernel, alpha=alpha, sub=sub),
        grid=(bsz, length // tm),
        in_specs=[pl.BlockSpec((None, tm, c), lambda i, j: (i, j, 0)),
                  pl.BlockSpec((None, c, tm // LANES, LANES), lambda i, j: (i, 0, j, 0)),
                  pl.BlockSpec((None, tm, 2 * d), lambda i, j: (i, j, 0)),
                  pl.BlockSpec((None, tm, d), lambda i, j: (i, j, 0)),
                  pl.BlockSpec((c, d), const), pl.BlockSpec((c, d), const), pl.BlockSpec((d, d), const),
                  pl.BlockSpec((1, d), const), pl.BlockSpec((1, d), const)],
        out_specs=pl.BlockSpec((None, tm, d), lambda i, j: (i, j, 0)),
        out_shape=jax.ShapeDtypeStruct((bsz, length, d), F32),
        compiler_params=_cparams("parallel", "parallel"),
        name="mix",
    )(ua, z4, gates, h, wa, wb, wm, g, b)


def _kv_kernel(mem_ref, wk_ref, wv_ref, k_ref, v_ref):
    mb = mem_ref[...].astype(BF16)
    k_ref[...] = _dot(mb, wk_ref[...]).astype(BF16)
    v_ref[...] = _dot(mb, wv_ref[...]).astype(BF16)


def _kv(mem, wk, wv):
    bsz, m, d = mem.shape
    const = lambda i: (0, 0)
    blk = pl.BlockSpec((None, m, d), lambda i: (i, 0, 0))
    return pl.pallas_call(
        _kv_kernel,
        grid=(bsz,),
        in_specs=[blk, pl.BlockSpec((d, d), const), pl.BlockSpec((d, d), const)],
        out_specs=[blk, blk],
        out_shape=[jax.ShapeDtypeStruct((bsz, m, d), BF16)] * 2,
        compiler_params=_cparams("parallel"),
        name="kv",
    )(mem, wk, wv)


def _xattn_kernel(x_ref, k_ref, v_ref, wq_ref, wo_ref, g_ref, b_ref, wrh_ref, wrl_ref,
                  x2_ref, x2b_ref, aff_ref, *, alpha, heads, sub):
    d = x_ref.shape[1]
    dh = d // heads
    wrh, wrl = wrh_ref[...], wrl_ref[...]
    for r in range(0, x_ref.shape[0], sub):
        rs = slice(r, r + sub)
        x = x_ref[rs, :]
        q = (_dot(x.astype(BF16), wq_ref[...]) * (dh ** -0.5)).astype(BF16)
        outs = []
        for hd in range(heads):
            sl = slice(hd * dh, (hd + 1) * dh)
            s = _dot_nt(q[:, sl], k_ref[:, sl])
            s = s - jnp.max(s, axis=-1, keepdims=True)
            p = jnp.exp(s)
            p = p / jnp.sum(p, axis=-1, keepdims=True)
            outs.append(_dot(p.astype(BF16), v_ref[:, sl]))
        o = jnp.concatenate(outs, axis=-1)
        xa = _dot(o.astype(BF16), wo_ref[...])
        x2 = _ln(alpha * x + xa, g_ref[...], b_ref[...])
        x2_ref[rs, :] = x2
        x2h, x2l = _split_bf16(x2)
        x2b_ref[rs, :] = x2h
        logits = _dot_nt(wrh, x2h) + (_dot_nt(wrh, x2l) + _dot_nt(wrl, x2h))
        logits = logits - jnp.max(logits, axis=0, keepdims=True)
        ex = jnp.exp(logits)
        aff_ref[:, rs] = ex / jnp.sum(ex, axis=0, keepdims=True)


def _xattn(x, k, v, wq, wo, g, b, wrh, wrl, *, tm, sub, alpha, heads):
    bsz, length, d = x.shape
    m = k.shape[1]
    e = wrh.shape[0]
    const = lambda i, j: (0, 0)
    tok = pl.BlockSpec((None, tm, d), lambda i, j: (i, j, 0))
    mem = pl.BlockSpec((None, m, d), lambda i, j: (i, 0, 0))
    return pl.pallas_call(
        functools.partial(_xattn_kernel, alpha=alpha, heads=heads, sub=min(sub, tm)),
        grid=(bsz, length // tm),
        in_specs=[tok, mem, mem, pl.BlockSpec((d, d), const), pl.BlockSpec((d, d), const),
                  pl.BlockSpec((1, d), const), pl.BlockSpec((1, d), const),
                  pl.BlockSpec((e, d), const), pl.BlockSpec((e, d), const)],
        out_specs=[tok, tok, pl.BlockSpec((None, e, tm), lambda i, j: (i, 0, j))],
        out_shape=[jax.ShapeDtypeStruct((bsz, length, d), F32),
                   jax.ShapeDtypeStruct((bsz, length, d), BF16),
                   jax.ShapeDtypeStruct((bsz, e, length), F32)],
        compiler_params=_cparams("parallel", "parallel"),
        name="xattn",
    )(x, k, v, wq, wo, g, b, wrh, wrl)


def _excl_cumsum_lanes(mask_f32, tri, write):
    e, t = mask_f32.shape
    carry = jnp.zeros((e, 1), F32)
    for c in range(t // LANES):
        m = mask_f32[:, c * LANES:(c + 1) * LANES]
        inc = _dot(m.astype(BF16), tri)
        write(c, inc - m + carry, carry)
        carry = carry + inc[:, LANES - 1:LANES]


def _select_kernel(aff_ref, tri_ref, slot_ref, slott_ref, ghl_ref, r0_ref, jlo_ref, nch_ref, cnt_ref, cum_ref,
                   *, cap, st, kct):
    a = aff_ref[...]
    e = a.shape[0]
    v = jnp.zeros((e, 1), I32)
    for bit in range(30, -1, -1):
        cand = v | (1 << bit)
        cnt = jnp.sum(jnp.where(a >= pltpu.bitcast(cand, F32), 1.0, 0.0), axis=1, keepdims=True)
        v = jnp.where(cnt >= cap, cand, v)
    thr = pltpu.bitcast(v, F32)
    gt = a > thr
    eq = a == thr
    need = cap - jnp.sum(jnp.where(gt, 1.0, 0.0), axis=1, keepdims=True)
    tri = tri_ref[...]

    def write_eq(c, val, before):
        cnt_ref[:, c * LANES:(c + 1) * LANES] = val
    _excl_cumsum_lanes(jnp.where(eq, 1.0, 0.0), tri, write_eq)
    sel = gt | (eq & (cnt_ref[...] < need))
    gate = jnp.where(sel, a, 0.0)

    n_tiles = a.shape[1] // LANES

    def write_rank(c, val, before):
        cnt_ref[:, c * LANES:(c + 1) * LANES] = val
        cum_ref[:, c:c + 1] = before
    _excl_cumsum_lanes(jnp.where(sel, 1.0, 0.0), tri, write_rank)
    slot = jnp.where(sel, cnt_ref[...], -1.0)
    slot_ref[...] = slot.astype(I32)
    slott_ref[...] = jnp.transpose(slot).astype(I32)
    g_hi = gate.astype(BF16).astype(F32)
    g_lo = (gate - g_hi).astype(BF16).astype(F32)
    ghl_ref[...] = jnp.transpose(jnp.concatenate([g_hi, g_lo], axis=0)).astype(BF16)

    before = cum_ref[:, 0:n_tiles]
    r0_ref[...] = before.astype(I32)
    after = jnp.concatenate([before[:, 1:], jnp.full((e, 1), cap, F32)], axis=1)
    for s in range(cap // st):
        lo = jnp.sum(jnp.where(after <= s * st, 1.0, 0.0), axis=1, keepdims=True)
        hi = jnp.sum(jnp.where(before < (s + 1) * st, 1.0, 0.0), axis=1, keepdims=True) - 1.0
        n = jnp.ones_like(lo)
        for m in range(1, -(-n_tiles // kct)):
            n = n + jnp.where(hi - lo >= m * kct, 1.0, 0.0)
        jlo_ref[:, s:s + 1] = lo.astype(I32)
        nch_ref[:, s:s + 1] = n.astype(I32)


def _select(aff, tri, *, cap, st, kct):
    bsz, e, t = aff.shape
    n_tiles, n_st = t // LANES, cap // st
    blk = pl.BlockSpec((None, e, t), lambda i: (i, 0, 0))
    small = lambda n: pl.BlockSpec((None, e, n), lambda i: (i, 0, 0))
    return pl.pallas_call(
        functools.partial(_select_kernel, cap=cap, st=st, kct=kct),
        grid=(bsz,),
        in_specs=[blk, pl.BlockSpec((LANES, LANES), lambda i: (0, 0))],
        out_specs=[blk, pl.BlockSpec((None, t, e), lambda i: (i, 0, 0)),
                   pl.BlockSpec((None, t, 2 * e), lambda i: (i, 0, 0)),
                   small(n_tiles), small(n_st), small(n_st)],
        out_shape=[jax.ShapeDtypeStruct((bsz, e, t), I32), jax.ShapeDtypeStruct((bsz, t, e), I32),
                   jax.ShapeDtypeStruct((bsz, t, 2 * e), BF16),
                   jax.ShapeDtypeStruct((bsz, e, n_tiles), I32),
                   jax.ShapeDtypeStruct((bsz, e, n_st), I32), jax.ShapeDtypeStruct((bsz, e, n_st), I32)],
        scratch_shapes=[pltpu.VMEM((e, t), F32), pltpu.VMEM((e, LANES), F32)],
        compiler_params=_cparams("parallel"),
        name="select",
    )(aff, tri)


def _window_start(r0, cap, win, align):
    return pl.multiple_of(jnp.minimum((r0 // align) * align, cap - win), align)


def _dispatch_kernel(jlo_ref, nch_ref, x_ref, slot_ref, ghl_ref, o_ref, gs_ref, acc_ref, gacc_ref,
                     *, st, kc, tt):
    b, e = pl.program_id(0), pl.program_id(1)
    t_total = x_ref.shape[0]
    n_e = ghl_ref.shape[1] // 2
    for s in range(o_ref.shape[0] // st):
        j0 = jlo_ref[b, e, s]

        def chunk(c, s=s, j0=j0):
            want = (j0 + c * (kc // tt)) * tt
            t0 = pl.multiple_of(jnp.minimum(want, t_total - kc), tt)
            tok = t0 + lax.broadcasted_iota(I32, (1, kc), 1)
            rel = jnp.where(tok >= want, slot_ref[:, pl.ds(t0, kc)] - s * st, -1)
            onehot = jnp.where(lax.broadcasted_iota(I32, (st, kc), 0) == rel, 1.0, 0.0).astype(BF16)
            xs = _dot(onehot, x_ref[pl.ds(t0, kc), :])
            gs = _dot(onehot, ghl_ref[pl.ds(t0, kc), :])
            return xs, gs

        xs, gs = chunk(0)
        acc_ref[...] = xs
        gacc_ref[...] = gs

        def body(c, carry):
            xs, gs = chunk(c)
            acc_ref[...] += xs
            gacc_ref[...] += gs
            return carry
        lax.fori_loop(1, nch_ref[b, e, s], body, 0)
        o_ref[s * st:(s + 1) * st, :] = acc_ref[...].astype(o_ref.dtype)
        lane = lax.broadcasted_iota(I32, (st, 2 * n_e), 1)
        mine = (lane == e) | (lane == e + n_e)
        gs_ref[s * st:(s + 1) * st, :] = jnp.sum(jnp.where(mine, gacc_ref[...], 0.0), axis=1, keepdims=True)


def _dispatch(jlo, nch, xb, slot4, gate_hl, *, cap, st, kc, tt):
    bsz, t, d = xb.shape
    e = slot4.shape[1]
    assert cap % st == 0 and kc % tt == 0 and t % tt == 0 and kc <= t
    return pl.pallas_call(
        functools.partial(_dispatch_kernel, st=st, kc=kc, tt=tt),
        grid_spec=pltpu.PrefetchScalarGridSpec(
            num_scalar_prefetch=2,
            grid=(bsz, e),
            in_specs=[pl.BlockSpec((None, t, d), lambda i, j, a, c: (i, 0, 0)),
                      pl.BlockSpec((None, None, 1, t), lambda i, j, a, c: (i, j, 0, 0)),
                      pl.BlockSpec((None, t, 2 * e), lambda i, j, a, c: (i, 0, 0))],
            out_specs=[pl.BlockSpec((None, None, cap, d), lambda i, j, a, c: (i, j, 0, 0)),
                       pl.BlockSpec((None, None, cap, 1), lambda i, j, a, c: (i, j, 0, 0))],
            scratch_shapes=[pltpu.VMEM((st, d), F32), pltpu.VMEM((st, 2 * e), F32)]),
        out_shape=[jax.ShapeDtypeStruct((bsz, e, cap, d), BF16),
                   jax.ShapeDtypeStruct((bsz, e, cap, 1), F32)],
        compiler_params=_cparams("parallel", "parallel"),
        name="dispatch",
    )(jlo, nch, xb, slot4, gate_hl)


def _expert_kernel(x_ref, gs_ref, wg_ref, wu_ref, wd_ref, o_ref, acc_ref):
    f = pl.program_id(2)
    wg = wg_ref[...].astype(BF16)
    wu = wu_ref[...].astype(BF16)
    wd = wd_ref[...].astype(BF16)
    @pl.when(f == 0)
    def _():
        acc_ref[...] = jnp.zeros_like(acc_ref)

    for bi in range(x_ref.shape[0]):
        x = x_ref[bi]
        hg = _dot(x, wg)
        hu = _dot(x, wu)
        act = (hg * _sigmoid(hg) * hu).astype(BF16)
        acc_ref[bi] += _dot(act, wd)

    @pl.when(f == pl.num_programs(2) - 1)
    def _():
        o_ref[...] = (acc_ref[...] * gs_ref[...]).astype(o_ref.dtype)


def _expert(xg, gs, wg, wu, wd, *, fc, nb):
    bsz, e, cap, d = xg.shape
    f = wg.shape[2]
    assert bsz % nb == 0 and f % fc == 0
    tok = lambda last: pl.BlockSpec((nb, None, cap, last), lambda i, j, k: (j, i, 0, 0))
    return pl.pallas_call(
        _expert_kernel,
        grid=(e, bsz // nb, f // fc),
        in_specs=[tok(d), tok(1),
                  pl.BlockSpec((None, d, fc), lambda i, j, k: (i, 0, k)),
                  pl.BlockSpec((None, d, fc), lambda i, j, k: (i, 0, k)),
                  pl.BlockSpec((None, fc, d), lambda i, j, k: (i, k, 0))],
        out_specs=tok(d),
        out_shape=jax.ShapeDtypeStruct((bsz, e, cap, d), BF16),
        scratch_shapes=[pltpu.VMEM((nb, cap, d), F32)],
        compiler_params=_cparams("parallel", "parallel", "arbitrary"),
        name="expert",
    )(xg, gs, wg, wu, wd)


def _combine_kernel(r0_ref, x_ref, slot_ref, g_ref, b_ref, y_ref, o_ref, *, tt, win, align, alpha):
    b, j = pl.program_id(0), pl.program_id(1)
    n_e, cap = y_ref.shape[0], y_ref.shape[1]
    n_sub = x_ref.shape[0] // tt
    for u in range(n_sub):
        rs = slice(u * tt, (u + 1) * tt)
        acc = alpha * x_ref[rs, :]
        slots = slot_ref[rs, :]
        for e in range(n_e):
            start = _window_start(r0_ref[b, e, j * n_sub + u], cap, win, align)
            rel = slots[:, e:e + 1] - start
            onehot = jnp.where(lax.broadcasted_iota(I32, (tt, win), 1) == rel, 1.0, 0.0).astype(BF16)
            acc = acc + _dot(onehot, y_ref[e, pl.ds(start, win), :])
        o_ref[rs, :] = _ln(acc, g_ref[...], b_ref[...])


def _combine(r0, x, slot_t, g, b, y, *, tt, n_sub, win, align, alpha):
    bsz, t, d = x.shape
    e, cap = y.shape[1], y.shape[2]
    assert win >= tt + align and win <= cap and (cap - win) % align == 0
    ts = tt * n_sub
    const = lambda i, j, r: (0, 0)
    return pl.pallas_call(
        functools.partial(_combine_kernel, tt=tt, win=win, align=align, alpha=alpha),
        grid_spec=pltpu.PrefetchScalarGridSpec(
            num_scalar_prefetch=1,
            grid=(bsz, t // ts),
            in_specs=[pl.BlockSpec((None, ts, d), lambda i, j, r: (i, j, 0)),
                      pl.BlockSpec((None, ts, e), lambda i, j, r: (i, j, 0)),
                      pl.BlockSpec((1, d), const), pl.BlockSpec((1, d), const),
                      pl.BlockSpec((None, e, cap, d), lambda i, j, r: (i, 0, 0, 0),
                                   pipeline_mode=pl.Buffered(1))],
            out_specs=pl.BlockSpec((None, ts, d), lambda i, j, r: (i, j, 0))),
        out_shape=jax.ShapeDtypeStruct((bsz, t, d), F32),
        compiler_params=_cparams("parallel", "parallel"),
        name="combine",
    )(r0, x, slot_t, g, b, y)


def _hyena_constants(length):
    n = 2 * length
    m = np.arange(n)
    pos = np.where(m < length, m, n - m).astype(np.float64)
    pos[length] = 0.0
    t = pos / max(length - 1, 1)
    f = np.linspace(1e-4, HY_BANDS - 1, HY_BANDS)
    ang = (2.0 * np.pi * pos / length)[:, None] * f[None, :]
    feat = np.concatenate([t[:, None], np.cos(ang), -np.sin(ang)], axis=-1)
    feat_t = np.zeros((LANES, n), np.float32)
    feat_t[:feat.shape[1]] = feat.T
    valid = np.ones((1, n), np.float32)
    valid[0, length] = 0.0
    return feat_t, t[None, :].astype(np.float32), valid


def _decay_rates(width):
    max_decay = math.log(HY_DECAY_TARGET) / HY_FAST_DECAY
    min_decay = math.log(HY_DECAY_TARGET) / HY_SLOW_DECAY
    return np.abs(np.linspace(min_decay, max_decay, width, dtype=np.float32)).astype(np.float32)


def _dft_constants(r1):
    n, a1h, a2, at = _dft_tables(r1, r1 // 2)
    _, a1f, _, _ = _dft_tables(r1, r1)
    bf = lambda x: jnp.asarray(x, F32).astype(BF16)
    w1_data = bf(_cplx_rhs(a1h, -1.0))
    w1_filt = bf(np.concatenate([np.cos(a1f), -np.sin(a1f)], axis=1))
    w2 = bf(_cplx_rhs(a2, -1.0))
    w3 = bf(_cplx_rhs(a2.T, 1.0))
    w4 = bf(_cplx_rhs(a1h.T, 1.0) / n)
    twr, twi = np.cos(at), -np.sin(at)
    f32 = lambda x: jnp.asarray(x, F32)
    return dict(w1_data=w1_data, w1_filt=w1_filt, w2=w2, w3=w3, w4=w4,
                twr=f32(twr), twi=f32(twi), twr_t=f32(twr.T), twi_t=f32(twi.T))


def kernel(x, mem, ln_in_g, ln_in_b, w_in, b_gate, conf_dw_w, conf_dw_b, conf_ln_g, conf_ln_b, conf_w_out, hy_short_w, hy_short_b, hy_ffn_w1, hy_ffn_b1, hy_freq1, hy_ffn_w2, hy_ffn_b2, hy_freq2, hy_ffn_w3, hy_skip, hy_w_out, w_mix_out, ln_mix_g, ln_mix_b, xa_wq, xa_wk, xa_wv, xa_wo, ln_xa_g, ln_xa_b, moe_w_router, moe_w_gate, moe_w_up, moe_w_down, ln_moe_g, ln_moe_b):
    bsz, length, d = x.shape
    depth = w_in.shape[0]
    alpha = (2.0 * depth) ** 0.25
    cw = conf_dw_w.shape[2]
    hw = hy_skip.shape[2]
    n_glu, n_hy = 2 * cw, 3 * hw
    n_exp = moe_w_router.shape[2]
    cap = max(1, N_EXPERT_CAPACITY * length // n_exp)
    rh = length // LANES
    r1 = 2 * rh
    assert bsz % 2 == 0 and length % LANES == 0

    row = lambda v: v.reshape(1, -1).astype(F32)
    col = lambda v: v.reshape(-1, 1).astype(F32)

    feat_t, t_ext, valid = _hyena_constants(length)
    dft = _dft_constants(r1)
    rates = _decay_rates(hw)
    tri = jnp.asarray(np.triu(np.ones((LANES, LANES), np.float32))).astype(BF16)

    n_ord = hy_skip.shape[1]
    tm = min(512, length)
    lane_b = lambda v: jnp.broadcast_to(v[..., None, None], v.shape + (1, LANES)).astype(F32)
    dft_data = (dft["w1_data"], dft["twr"], dft["twi"], dft["w2"], dft["w3"], dft["twr_t"], dft["twi_t"],
                dft["w4"])

    h = x
    for i in range(depth):
        w1h, w1l = _split_bf16(jnp.pad(hy_ffn_w1[i].T, ((0, 0), (0, LANES - hy_ffn_w1.shape[1]))))
        w2h, w2l = _split_bf16(hy_ffn_w2[i].T)
        h2t = _filt_ffn(jnp.asarray(feat_t), w1h, w1l, col(hy_ffn_b1[i]), col(hy_freq1[i]),
                        w2h, w2l, col(hy_ffn_b2[i]), col(hy_freq2[i]), tn=min(2048, 2 * length))
        w3 = hy_ffn_w3[i].reshape(-1, n_ord, 2, hw)
        w3f = jnp.transpose(w3[:, :, 0], (1, 2, 0)).reshape(n_ord * hw, -1)
        w3b = jnp.transpose(w3[:, :, 1], (1, 2, 0)).reshape(n_ord * hw, -1)
        w3fh, w3fl = _split_bf16(w3f)
        w3bh, w3bl = _split_bf16(w3b)
        delta = jnp.asarray(np.tile(rates, n_ord).reshape(-1, 1))
        kern = _filt(h2t, w3fh, w3fl, w3bh, w3bl, delta, jnp.asarray(t_ext), jnp.asarray(valid),
                     g=min(128, hw))
        kf = _kfft(kern.reshape(n_ord * hw, r1, LANES), dft["w1_filt"], dft["twr"], dft["twi"], dft["w2"],
                   g=min(32, hw))

        wi = w_in[i]
        wglu = wi[:, :n_glu].astype(BF16)
        whyt = wi[:, n_glu:n_glu + n_hy].T.astype(BF16)
        wgate = wi[:, n_glu + n_hy:].astype(BF16)
        hn, u, hyt, gates = _inproj(h, row(ln_in_g), row(ln_in_b), wglu, whyt, wgate, row(b_gate[i]),
                                    tm=tm, apply_ln=(i == 0))

        ua = _conf(u, conf_dw_w[i], row(conf_dw_b[i]), row(conf_ln_g[i]), row(conf_ln_b[i]),
                   tl=min(1024, length))

        z = _hyena(hyt.reshape(bsz, n_hy, rh, LANES), lane_b(hy_short_w[i]), lane_b(hy_short_b[i]), kf,
                   lane_b(hy_skip[i]), dft_data, g=min(16, hw))

        x1 = _mix(ua, z, gates, hn, conf_w_out[i].astype(BF16),
                  hy_w_out[i].astype(BF16), w_mix_out[i].astype(BF16), row(ln_mix_g[i]), row(ln_mix_b[i]),
                  tm=min(1024, length), sub=512, alpha=alpha)

        k, v = _kv(mem, xa_wk[i].astype(BF16), xa_wv[i].astype(BF16))
        wrh, wrl = _split_bf16(moe_w_router[i].T)
        x2, x2b, aff = _xattn(x1, k, v, xa_wq[i].astype(BF16), xa_wo[i].astype(BF16), row(ln_xa_g[i]),
                              row(ln_xa_b[i]), wrh, wrl, tm=min(1024, length), sub=512, alpha=alpha,
                              heads=XA_HEADS)

        tt, st, align = LANES, LANES, 16
        kc = min(10 * tt, length)
        slot, slot_t, gate_hl, r0, jlo, nch = _select(aff, tri, cap=cap, st=st, kct=kc // tt)
        xg, gslot = _dispatch(jlo, nch, x2b, slot.reshape(bsz, n_exp, 1, length), gate_hl,
                              cap=cap, st=st, kc=kc, tt=tt)
        y = _expert(xg, gslot, moe_w_gate[i], moe_w_up[i], moe_w_down[i],
                    fc=min(512, moe_w_gate.shape[3]), nb=2)
        h = _combine(r0, x2, slot_t, row(ln_moe_g[i]), row(ln_moe_b[i]), y,
                     tt=tt, n_sub=min(4, length // tt), win=tt + align, align=align, alpha=alpha)
    return h
```

```python
import functools
import math

import numpy as np
import jax
import jax.numpy as jnp
from jax import lax
from jax.experimental import pallas as pl
from jax.experimental.pallas import tpu as pltpu

F32 = jnp.float32
BF16 = jnp.bfloat16
I32 = jnp.int32

LANES = 128
SUBLANES = 8
VMEM_LIMIT_BYTES = 56 * 1024 * 1024
SUB_ROWS = 256

LN_EPS = 1e-5
XA_HEADS = 4
N_EXPERT_CAPACITY = 2
HY_DECAY_TARGET = 1e-2
HY_FAST_DECAY = 0.3
HY_SLOW_DECAY = 1.5
HY_BANDS = 16


def _cparams(*sem):
    return pltpu.CompilerParams(dimension_semantics=sem, vmem_limit_bytes=VMEM_LIMIT_BYTES)


def _ln(x, g, b):
    mu = jnp.mean(x, axis=-1, keepdims=True)
    xc = x - mu
    var = jnp.mean(xc * xc, axis=-1, keepdims=True)
    return xc * lax.rsqrt(var + LN_EPS) * g + b


def _sigmoid(x):
    return 1.0 / (1.0 + jnp.exp(-x))


def _dot(a, b):
    return jnp.dot(a, b, preferred_element_type=F32)


def _dot_nt(a, b):
    return lax.dot_general(a, b, (((1,), (1,)), ((), ())), preferred_element_type=F32)


def _split_bf16(x):
    hi = x.astype(BF16)
    lo = (x - hi.astype(F32)).astype(BF16)
    return hi, lo


def _dot3(a, b_hi, b_lo):
    a_hi, a_lo = _split_bf16(a)
    return _dot(a_hi, b_hi) + (_dot(a_hi, b_lo) + _dot(a_lo, b_hi))


def _dot3_lhs(a_hi, a_lo, b):
    b_hi, b_lo = _split_bf16(b)
    return _dot(a_hi, b_hi) + (_dot(a_hi, b_lo) + _dot(a_lo, b_hi))


def _inproj_kernel(x_ref, g_ref, b_ref, wglu_ref, whyt_ref, wgate_ref, bgate_ref,
                   h_ref, u_ref, hyt_ref, gate_ref, *, apply_ln, sub):
    for r in range(0, x_ref.shape[0], sub):
        rs = slice(r, r + sub)
        h = _ln(x_ref[rs, :], g_ref[...], b_ref[...]) if apply_ln else x_ref[rs, :]
        h_ref[rs, :] = h
        hb = h.astype(BF16)
        glu = _dot(hb, wglu_ref[...])
        cw = glu.shape[1] // 2
        u_ref[rs, :] = glu[:, :cw] * _sigmoid(glu[:, cw:])
        hyt_ref[:, rs] = _dot_nt(whyt_ref[...], hb).astype(hyt_ref.dtype)
        gl = _dot(hb, wgate_ref[...]) + bgate_ref[...]
        gate_ref[rs, :] = _sigmoid(gl).astype(BF16)


def _inproj(x, g, b, wglu, whyt, wgate, bgate, *, tm, apply_ln):
    bsz, length, d = x.shape
    n_glu, n_hy, n_gate = wglu.shape[1], whyt.shape[0], wgate.shape[1]
    const = lambda i, j: (0, 0)
    return pl.pallas_call(
        functools.partial(_inproj_kernel, apply_ln=apply_ln, sub=min(SUB_ROWS, tm)),
        grid=(bsz, length // tm),
        in_specs=[
            pl.BlockSpec((None, tm, d), lambda i, j: (i, j, 0)),
            pl.BlockSpec((1, d), const), pl.BlockSpec((1, d), const),
            pl.BlockSpec((d, n_glu), const),
            pl.BlockSpec((n_hy, d), const),
            pl.BlockSpec((d, n_gate), const),
            pl.BlockSpec((1, n_gate), const),
        ],
        out_specs=[
            pl.BlockSpec((None, tm, d), lambda i, j: (i, j, 0)),
            pl.BlockSpec((None, tm, n_glu // 2), lambda i, j: (i, j, 0)),
            pl.BlockSpec((None, n_hy, tm), lambda i, j: (i, 0, j)),
            pl.BlockSpec((None, tm, n_gate), lambda i, j: (i, j, 0)),
        ],
        out_shape=[
            jax.ShapeDtypeStruct((bsz, length, d), F32),
            jax.ShapeDtypeStruct((bsz, length, n_glu // 2), F32),
            jax.ShapeDtypeStruct((bsz, n_hy, length), BF16),
            jax.ShapeDtypeStruct((bsz, length, n_gate), BF16),
        ],
        compiler_params=_cparams("parallel", "parallel"),
        name="inproj",
    )(x, g, b, wglu, whyt, wgate, bgate)


def _conf_kernel(prev_ref, cur_ref, next_ref, w_ref, cb_ref, g_ref, b_ref, o_ref, ext_ref, sh_ref,
                 *, tl, halo, ksize, rows):
    j = pl.program_id(1)
    nj = pl.num_programs(1)
    pad = (ksize - 1) // 2
    ext_ref[0:halo, :] = jnp.where(j > 0, prev_ref[...], 0.0)
    ext_ref[halo:halo + tl, :] = cur_ref[...]
    ext_ref[halo + tl:halo + tl + halo, :] = jnp.where(j < nj - 1, next_ref[...], 0.0)
    n_sh = sh_ref.shape[1]
    for r in range(1, SUBLANES):
        sh_ref[r - 1, :, :] = ext_ref[r:r + n_sh, :]
    w = w_ref[...]
    cb, g, b = cb_ref[...], g_ref[...], b_ref[...]
    for r0 in range(0, tl, rows):
        acc = jnp.zeros((rows, w.shape[1]), F32)
        for k in range(ksize):
            q, r = divmod(halo - pad + k + r0, SUBLANES)
            q *= SUBLANES
            src = ext_ref[q:q + rows, :] if r == 0 else sh_ref[r - 1, q:q + rows, :]
            acc = acc + w[k:k + 1, :] * src
        y = _ln(acc + cb, g, b)
        o_ref[r0:r0 + rows, :] = (y * _sigmoid(y)).astype(o_ref.dtype)


def _conf(u, w, cb, g, b, *, tl, halo=16, rows=64):
    bsz, length, c = u.shape
    ksize = w.shape[0]
    assert (ksize - 1) // 2 <= halo and tl % halo == 0 and tl % rows == 0
    nh = tl // halo
    last = length // halo - 1
    const = lambda i, j: (0, 0)
    return pl.pallas_call(
        functools.partial(_conf_kernel, tl=tl, halo=halo, ksize=ksize, rows=rows),
        grid=(bsz, length // tl),
        in_specs=[
            pl.BlockSpec((None, halo, c), lambda i, j: (i, jnp.maximum(j * nh - 1, 0), 0)),
            pl.BlockSpec((None, tl, c), lambda i, j: (i, j, 0)),
            pl.BlockSpec((None, halo, c), lambda i, j: (i, jnp.minimum((j + 1) * nh, last), 0)),
            pl.BlockSpec((ksize, c), const),
            pl.BlockSpec((1, c), const), pl.BlockSpec((1, c), const), pl.BlockSpec((1, c), const),
        ],
        out_specs=pl.BlockSpec((None, tl, c), lambda i, j: (i, j, 0)),
        out_shape=jax.ShapeDtypeStruct((bsz, length, c), BF16),
        scratch_shapes=[pltpu.VMEM((tl + 2 * halo, c), F32),
                        pltpu.VMEM((SUBLANES - 1, tl + 2 * halo - SUBLANES, c), F32)],
        compiler_params=_cparams("parallel", "parallel"),
        name="conf",
    )(u, u, u, w, cb, g, b)


def _filt_ffn_kernel(feat_ref, w1h_ref, w1l_ref, b1_ref, f1_ref, w2h_ref, w2l_ref, b2_ref, f2_ref, o_ref):
    a = _dot3_lhs(w1h_ref[...], w1l_ref[...], feat_ref[...]) + b1_ref[...]
    h = jnp.sin(f1_ref[...] * a)
    a2 = _dot3_lhs(w2h_ref[...], w2l_ref[...], h) + b2_ref[...]
    o_ref[...] = jnp.sin(f2_ref[...] * a2)


def _filt_ffn(feat_t, w1h, w1l, b1, f1, w2h, w2l, b2, f2, *, tn):
    kp, n = feat_t.shape
    m = w1h.shape[0]
    const = lambda i: (0, 0)
    return pl.pallas_call(
        _filt_ffn_kernel,
        grid=(n // tn,),
        in_specs=[pl.BlockSpec((kp, tn), lambda i: (0, i)),
                  pl.BlockSpec((m, kp), const), pl.BlockSpec((m, kp), const),
                  pl.BlockSpec((m, 1), const), pl.BlockSpec((m, 1), const),
                  pl.BlockSpec((m, m), const), pl.BlockSpec((m, m), const),
                  pl.BlockSpec((m, 1), const), pl.BlockSpec((m, 1), const)],
        out_specs=pl.BlockSpec((m, tn), lambda i: (0, i)),
        out_shape=jax.ShapeDtypeStruct((m, n), F32),
        compiler_params=_cparams("parallel"),
        name="filt_ffn",
    )(feat_t, w1h, w1l, b1, f1, w2h, w2l, b2, f2)


def _filt_kernel(h2_ref, w3fh_ref, w3fl_ref, w3bh_ref, w3bl_ref, delta_ref, text_ref, valid_ref, o_ref):
    n = h2_ref.shape[1]
    half = n // 2
    decay = jnp.exp(-text_ref[...] * delta_ref[...]) * valid_ref[...]
    kf = _dot3_lhs(w3fh_ref[...], w3fl_ref[...], h2_ref[:, 0:half]) * decay[:, 0:half]
    kb = _dot3_lhs(w3bh_ref[...], w3bl_ref[...], h2_ref[:, half:n]) * decay[:, half:n]
    ss = jnp.sum(kf * kf, axis=1, keepdims=True) + jnp.sum(kb * kb, axis=1, keepdims=True)
    scale = lax.rsqrt(ss + 1e-6)
    o_ref[:, 0:half] = kf * scale
    o_ref[:, half:n] = kb * scale


def _filt(h2t, w3fh, w3fl, w3bh, w3bl, delta, text, valid, *, g):
    m, n = h2t.shape
    rows = w3fh.shape[0]
    const = lambda i: (0, 0)
    return pl.pallas_call(
        _filt_kernel,
        grid=(rows // g,),
        in_specs=[pl.BlockSpec((m, n), const),
                  pl.BlockSpec((g, m), lambda i: (i, 0)), pl.BlockSpec((g, m), lambda i: (i, 0)),
                  pl.BlockSpec((g, m), lambda i: (i, 0)), pl.BlockSpec((g, m), lambda i: (i, 0)),
                  pl.BlockSpec((g, 1), lambda i: (i, 0)),
                  pl.BlockSpec((1, n), const), pl.BlockSpec((1, n), const)],
        out_specs=pl.BlockSpec((g, n), lambda i: (i, 0)),
        out_shape=jax.ShapeDtypeStruct((rows, n), F32),
        compiler_params=_cparams("parallel"),
        name="filt",
    )(h2t, w3fh, w3fl, w3bh, w3bl, delta, text, valid)


def _dft_tables(r1, n1_used):
    n = r1 * LANES
    a1 = 2.0 * np.pi * np.outer(np.arange(n1_used), np.arange(r1)) / r1
    a2 = 2.0 * np.pi * np.outer(np.arange(LANES), np.arange(LANES)) / LANES
    at = 2.0 * np.pi * np.outer(np.arange(LANES), np.arange(r1)) / n
    return n, a1, a2, at


def _cplx_rhs(ang, sign):
    c, s = np.cos(ang), sign * np.sin(ang)
    return np.block([[c, s], [-s, c]])


def _fwd_steps(x_t, w1, twr, twi, w2):
    g, n2, k1w = x_t.shape
    r1 = twr.shape[1]
    a = _dot(x_t.reshape(g * n2, k1w).astype(BF16), w1).reshape(g, n2, 2 * r1)
    ar, ai = a[:, :, :r1], a[:, :, r1:]
    br = ar * twr - ai * twi
    bi = ar * twi + ai * twr
    bt = jnp.concatenate([jnp.swapaxes(br.astype(BF16), 1, 2), jnp.swapaxes(bi.astype(BF16), 1, 2)],
                         axis=-1)
    c = _dot(bt.reshape(g * r1, 2 * n2), w2)
    return c.reshape(g, r1, 2 * n2)


def _inv_steps(d, w3, twr_t, twi_t, w4):
    g, r1, w = d.shape
    n2 = w // 2
    e = _dot(d.reshape(g * r1, w).astype(BF16), w3).reshape(g, r1, w)
    er, ei = e[:, :, :n2], e[:, :, n2:]
    fr = er * twr_t + ei * twi_t
    fi = ei * twr_t - er * twi_t
    ft = jnp.concatenate([jnp.swapaxes(fr.astype(BF16), 1, 2), jnp.swapaxes(fi.astype(BF16), 1, 2)],
                         axis=-1)
    y = _dot(ft.reshape(g * n2, 2 * r1), w4)
    return y.reshape(g, n2, w4.shape[1])


def _kfft_kernel(k_ref, w1_ref, twr_ref, twi_ref, w2_ref, o_ref):
    x_t = jnp.swapaxes(k_ref[...].astype(BF16), 1, 2)
    o_ref[...] = _fwd_steps(x_t, w1_ref[...], twr_ref[...], twi_ref[...], w2_ref[...])


def _kfft(k3, w1, twr, twi, w2, *, g):
    rows, r1, lanes = k3.shape
    const = lambda i: (0, 0)
    return pl.pallas_call(
        _kfft_kernel,
        grid=(rows // g,),
        in_specs=[pl.BlockSpec((g, r1, lanes), lambda i: (i, 0, 0)),
                  pl.BlockSpec(w1.shape, const), pl.BlockSpec(twr.shape, const),
                  pl.BlockSpec(twi.shape, const), pl.BlockSpec(w2.shape, const)],
        out_specs=pl.BlockSpec((g, r1, 2 * lanes), lambda i: (i, 0, 0)),
        out_shape=jax.ShapeDtypeStruct((rows, r1, 2 * lanes), F32),
        compiler_params=_cparams("parallel"),
        name="kfft",
    )(k3, w1, twr, twi, w2)


def _short_conv(a, w, b):
    p, g, r, l = a.shape
    a3 = a.reshape(p * g, r, l)
    lane = lax.broadcasted_iota(I32, a3.shape, 2)
    row = lax.broadcasted_iota(I32, a3.shape, 1)
    pl_ = pltpu.roll(a3, 1, 2)
    ql = pltpu.roll(pl_, 1, 1)
    prev = jnp.where(lane == 0, jnp.where(row == 0, 0.0, ql), pl_)
    pr = pltpu.roll(a3, l - 1, 2)
    qr = pltpu.roll(pr, r - 1, 1)
    nxt = jnp.where(lane == l - 1, jnp.where(row == r - 1, 0.0, qr), pr)
    prev, nxt = prev.reshape(a.shape), nxt.reshape(a.shape)
    return w[0][None] * prev + w[1][None] * a + w[2][None] * nxt + b[None]


def _hyena_kernel(x1_ref, x2_ref, v_ref, sw1_ref, sb1_ref, sw2_ref, sb2_ref, swv_ref, sbv_ref,
                  kf0_ref, kf1_ref, skip_ref, w1_ref, twr_ref, twi_ref, w2_ref, w3_ref,
                  twrt_ref, twit_ref, w4_ref, z_ref):
    x1 = _short_conv(x1_ref[...].astype(F32), sw1_ref[...], sb1_ref[...])
    x2 = _short_conv(x2_ref[...].astype(F32), sw2_ref[...], sb2_ref[...])
    z = _short_conv(v_ref[...].astype(F32), swv_ref[...], sbv_ref[...])
    rh = z.shape[2]
    for gate, kf_ref, o in ((x1, kf0_ref, 0), (x2, kf1_ref, 1)):
        s = jnp.concatenate([z[0], z[1]], axis=1)
        c = _fwd_steps(jnp.swapaxes(s, 1, 2), w1_ref[...], twr_ref[...], twi_ref[...], w2_ref[...])
        kf = kf_ref[...]
        n2 = c.shape[2] // 2
        cr, ci, kr, ki = c[:, :, :n2], c[:, :, n2:], kf[:, :, :n2], kf[:, :, n2:]
        d = jnp.concatenate([cr * kr - ci * ki, cr * ki + ci * kr], axis=-1)
        y = _inv_steps(d, w3_ref[...], twrt_ref[...], twit_ref[...], w4_ref[...])
        y = jnp.swapaxes(y, 1, 2)
        conv = jnp.stack([y[:, :rh], y[:, rh:]], axis=0)
        z = gate * (conv + skip_ref[o][None] * z)
    z_ref[...] = z


def _hyena(hy4, sw, sb, kf, skip, tabs, *, g):
    bsz, c3, rh, lanes = hy4.shape
    c = c3 // 3
    r1 = 2 * rh
    ncb = c // g
    w1, twr, twi, w2, w3, twrt, twit, w4 = tabs
    const2 = lambda i, p: (0, 0)
    data = lambda off: pl.BlockSpec((2, g, rh, lanes), lambda i, p, off=off: (p, off + i, 0, 0))
    wspec = lambda off: pl.BlockSpec((3, g, 1, lanes), lambda i, p, off=off: (0, off + i, 0, 0))
    bspec = lambda off: pl.BlockSpec((g, 1, lanes), lambda i, p, off=off: (off + i, 0, 0))
    kspec = lambda off: pl.BlockSpec((g, r1, 2 * lanes), lambda i, p, off=off: (off + i, 0, 0))
    return pl.pallas_call(
        _hyena_kernel,
        grid=(ncb, bsz // 2),
        in_specs=[data(0), data(ncb), data(2 * ncb),
                  wspec(0), bspec(0), wspec(ncb), bspec(ncb), wspec(2 * ncb), bspec(2 * ncb),
                  kspec(0), kspec(ncb),
                  pl.BlockSpec((2, g, 1, lanes), lambda i, p: (0, i, 0, 0)),
                  pl.BlockSpec(w1.shape, const2), pl.BlockSpec(twr.shape, const2),
                  pl.BlockSpec(twi.shape, const2), pl.BlockSpec(w2.shape, const2),
                  pl.BlockSpec(w3.shape, const2), pl.BlockSpec(twrt.shape, const2),
                  pl.BlockSpec(twit.shape, const2), pl.BlockSpec(w4.shape, const2)],
        out_specs=pl.BlockSpec((2, g, rh, lanes), lambda i, p: (p, i, 0, 0)),
        out_shape=jax.ShapeDtypeStruct((bsz, c, rh, lanes), F32),
        compiler_params=_cparams("parallel", "parallel"),
        name="hyena",
    )(hy4, hy4, hy4, sw, sb, sw, sb, sw, sb, kf, kf, skip, w1, twr, twi, w2, w3, twrt, twit, w4)


def _mix_kernel(ua_ref, z_ref, gate_ref, h_ref, wa_ref, wb_ref, wm_ref, g_ref, b_ref, o_ref, *, alpha, sub):
    for r in range(0, h_ref.shape[0], sub):
        rs = slice(r, r + sub)
        ya = _dot(ua_ref[rs, :], wa_ref[...])
        zc = jnp.concatenate([z_ref[:, k, :] for k in range(r // LANES, (r + sub) // LANES)], axis=-1)
        yb = _dot(jnp.transpose(zc).astype(BF16), wb_ref[...])
        gt = gate_ref[rs, :].astype(F32)
        d = ya.shape[1]
        m = gt[:, :d] * ya + gt[:, d:] * yb
        mixed = _dot(m.astype(BF16), wm_ref[...])
        o_ref[rs, :] = _ln(alpha * h_ref[rs, :] + mixed, g_ref[...], b_ref[...])


def _mix(ua, z4, gates, h, wa, wb, wm, g, b, *, tm, sub, alpha):
    bsz, length, d = h.shape
    c = ua.shape[2]
    sub = min(sub, tm)
    assert sub % LANES == 0 and ((tm // LANES) % SUBLANES == 0 or tm == length)
    const = lambda i, j: (0, 0)
    return pl.pallas_call(
        functools.partial(_mix_kernel, alpha=alpha, sub=sub),
        grid=(bsz, length // tm),
        in_specs=[pl.BlockSpec((None, tm, c), lambda i, j: (i, j, 0)),
                  pl.BlockSpec((None, c, tm // LANES, LANES), lambda i, j: (i, 0, j, 0)),
                  pl.BlockSpec((None, tm, 2 * d), lambda i, j: (i, j, 0)),
                  pl.BlockSpec((None, tm, d), lambda i, j: (i, j, 0)),
                  pl.BlockSpec((c, d), const), pl.BlockSpec((c, d), const), pl.BlockSpec((d, d), const),
                  pl.BlockSpec((1, d), const), pl.BlockSpec((1, d), const)],
        out_specs=pl.BlockSpec((None, tm, d), lambda i, j: (i, j, 0)),
        out_shape=jax.ShapeDtypeStruct((bsz, length, d), F32),
        compiler_params=_cparams("parallel", "parallel"),
        name="mix",
    )(ua, z4, gates, h, wa, wb, wm, g, b)


def _kv_kernel(mem_ref, wk_ref, wv_ref, k_ref, v_ref):
    mb = mem_ref[...].astype(BF16)
    k_ref[...] = _dot(mb, wk_ref[...]).astype(BF16)
    v_ref[...] = _dot(mb, wv_ref[...]).astype(BF16)


def _kv(mem, wk, wv):
    bsz, m, d = mem.shape
    const = lambda i: (0, 0)
    blk = pl.BlockSpec((None, m, d), lambda i: (i, 0, 0))
    return pl.pallas_call(
        _kv_kernel,
        grid=(bsz,),
        in_specs=[blk, pl.BlockSpec((d, d), const), pl.BlockSpec((d, d), const)],
        out_specs=[blk, blk],
        out_shape=[jax.ShapeDtypeStruct((bsz, m, d), BF16)] * 2,
        compiler_params=_cparams("parallel"),
        name="kv",
    )(mem, wk, wv)


def _xattn_kernel(x_ref, k_ref, v_ref, wq_ref, wo_ref, g_ref, b_ref, wrh_ref, wrl_ref,
                  x2_ref, x2b_ref, aff_ref, *, alpha, heads, sub):
    d = x_ref.shape[1]
    dh = d // heads
    wrh, wrl = wrh_ref[...], wrl_ref[...]
    for r in range(0, x_ref.shape[0], sub):
        rs = slice(r, r + sub)
        x = x_ref[rs, :]
        q = (_dot(x.astype(BF16), wq_ref[...]) * (dh ** -0.5)).astype(BF16)
        outs = []
        for hd in range(heads):
            sl = slice(hd * dh, (hd + 1) * dh)
            s = _dot_nt(q[:, sl], k_ref[:, sl])
            s = s - jnp.max(s, axis=-1, keepdims=True)
            p = jnp.exp(s)
            p = p / jnp.sum(p, axis=-1, keepdims=True)
            outs.append(_dot(p.astype(BF16), v_ref[:, sl]))
        o = jnp.concatenate(outs, axis=-1)
        xa = _dot(o.astype(BF16), wo_ref[...])
        x2 = _ln(alpha * x + xa, g_ref[...], b_ref[...])
        x2_ref[rs, :] = x2
        x2h, x2l = _split_bf16(x2)
        x2b_ref[rs, :] = x2h
        logits = _dot_nt(wrh, x2h) + (_dot_nt(wrh, x2l) + _dot_nt(wrl, x2h))
        logits = logits - jnp.max(logits, axis=0, keepdims=True)
        ex = jnp.exp(logits)
        aff_ref[:, rs] = ex / jnp.sum(ex, axis=0, keepdims=True)


def _xattn(x, k, v, wq, wo, g, b, wrh, wrl, *, tm, sub, alpha, heads):
    bsz, length, d = x.shape
    m = k.shape[1]
    e = wrh.shape[0]
    const = lambda i, j: (0, 0)
    tok = pl.BlockSpec((None, tm, d), lambda i, j: (i, j, 0))
    mem = pl.BlockSpec((None, m, d), lambda i, j: (i, 0, 0))
    return pl.pallas_call(
        functools.partial(_xattn_kernel, alpha=alpha, heads=heads, sub=min(sub, tm)),
        grid=(bsz, length // tm),
        in_specs=[tok, mem, mem, pl.BlockSpec((d, d), const), pl.BlockSpec((d, d), const),
                  pl.BlockSpec((1, d), const), pl.BlockSpec((1, d), const),
                  pl.BlockSpec((e, d), const), pl.BlockSpec((e, d), const)],
        out_specs=[tok, tok, pl.BlockSpec((None, e, tm), lambda i, j: (i, 0, j))],
        out_shape=[jax.ShapeDtypeStruct((bsz, length, d), F32),
                   jax.ShapeDtypeStruct((bsz, length, d), BF16),
                   jax.ShapeDtypeStruct((bsz, e, length), F32)],
        compiler_params=_cparams("parallel", "parallel"),
        name="xattn",
    )(x, k, v, wq, wo, g, b, wrh, wrl)


def _excl_cumsum_lanes(mask_f32, tri, write):
    e, t = mask_f32.shape
    carry = jnp.zeros((e, 1), F32)
    for c in range(t // LANES):
        m = mask_f32[:, c * LANES:(c + 1) * LANES]
        inc = _dot(m.astype(BF16), tri)
        write(c, inc - m + carry, carry)
        carry = carry + inc[:, LANES - 1:LANES]


def _select_kernel(aff_ref, tri_ref, slot_ref, slott_ref, ghl_ref, r0_ref, jlo_ref, nch_ref, cnt_ref, cum_ref,
                   *, cap, st, kct):
    a = aff_ref[...]
    e = a.shape[0]
    v = jnp.zeros((e, 1), I32)
    for bit in range(30, -1, -1):
        cand = v | (1 << bit)
        cnt = jnp.sum(jnp.where(a >= pltpu.bitcast(cand, F32), 1.0, 0.0), axis=1, keepdims=True)
        v = jnp.where(cnt >= cap, cand, v)
    thr = pltpu.bitcast(v, F32)
    gt = a > thr
    eq = a == thr
    need = cap - jnp.sum(jnp.where(gt, 1.0, 0.0), axis=1, keepdims=True)
    tri = tri_ref[...]

    def write_eq(c, val, before):
        cnt_ref[:, c * LANES:(c + 1) * LANES] = val
    _excl_cumsum_lanes(jnp.where(eq, 1.0, 0.0), tri, write_eq)
    sel = gt | (eq & (cnt_ref[...] < need))
    gate = jnp.where(sel, a, 0.0)

    n_tiles = a.shape[1] // LANES

    def write_rank(c, val, before):
        cnt_ref[:, c * LANES:(c + 1) * LANES] = val
        cum_ref[:, c:c + 1] = before
    _excl_cumsum_lanes(jnp.where(sel, 1.0, 0.0), tri, write_rank)
    slot = jnp.where(sel, cnt_ref[...], -1.0)
    slot_ref[...] = slot.astype(I32)
    slott_ref[...] = jnp.transpose(slot).astype(I32)
    g_hi = gate.astype(BF16).astype(F32)
    g_lo = (gate - g_hi).astype(BF16).astype(F32)
    ghl_ref[...] = jnp.transpose(jnp.concatenate([g_hi, g_lo], axis=0)).astype(BF16)

    before = cum_ref[:, 0:n_tiles]
    r0_ref[...] = before.astype(I32)
    after = jnp.concatenate([before[:, 1:], jnp.full((e, 1), cap, F32)], axis=1)
    for s in range(cap // st):
        lo = jnp.sum(jnp.where(after <= s * st, 1.0, 0.0), axis=1, keepdims=True)
        hi = jnp.sum(jnp.where(before < (s + 1) * st, 1.0, 0.0), axis=1, keepdims=True) - 1.0
        n = jnp.ones_like(lo)
        for m in range(1, -(-n_tiles // kct)):
            n = n + jnp.where(hi - lo >= m * kct, 1.0, 0.0)
        jlo_ref[:, s:s + 1] = lo.astype(I32)
        nch_ref[:, s:s + 1] = n.astype(I32)


def _select(aff, tri, *, cap, st, kct):
    bsz, e, t = aff.shape
    n_tiles, n_st = t // LANES, cap // st
    blk = pl.BlockSpec((None, e, t), lambda i: (i, 0, 0))
    small = lambda n: pl.BlockSpec((None, e, n), lambda i: (i, 0, 0))
    return pl.pallas_call(
        functools.partial(_select_kernel, cap=cap, st=st, kct=kct),
        grid=(bsz,),
        in_specs=[blk, pl.BlockSpec((LANES, LANES), lambda i: (0, 0))],
        out_specs=[blk, pl.BlockSpec((None, t, e), lambda i: (i, 0, 0)),
                   pl.BlockSpec((None, t, 2 * e), lambda i: (i, 0, 0)),
                   small(n_tiles), small(n_st), small(n_st)],
        out_shape=[jax.ShapeDtypeStruct((bsz, e, t), I32), jax.ShapeDtypeStruct((bsz, t, e), I32),
                   jax.ShapeDtypeStruct((bsz, t, 2 * e), BF16),
                   jax.ShapeDtypeStruct((bsz, e, n_tiles), I32),
                   jax.ShapeDtypeStruct((bsz, e, n_st), I32), jax.ShapeDtypeStruct((bsz, e, n_st), I32)],
        scratch_shapes=[pltpu.VMEM((e, t), F32), pltpu.VMEM((e, LANES), F32)],
        compiler_params=_cparams("parallel"),
        name="select",
    )(aff, tri)


def _window_start(r0, cap, win, align):
    return pl.multiple_of(jnp.minimum((r0 // align) * align, cap - win), align)


def _dispatch_kernel(jlo_ref, nch_ref, x_ref, slot_ref, ghl_ref, o_ref, gs_ref, acc_ref, gacc_ref,
                     *, st, kc, tt):
    b, e = pl.program_id(0), pl.program_id(1)
    t_total = x_ref.shape[0]
    n_e = ghl_ref.shape[1] // 2
    n_st = o_ref.shape[0] // st

    def chunk(s, c):
        want = (jlo_ref[b, e, s] + c * (kc // tt)) * tt
        t0 = pl.multiple_of(jnp.minimum(want, t_total - kc), tt)
        tok = t0 + lax.broadcasted_iota(I32, (1, kc), 1)
        rel = jnp.where(tok >= want, slot_ref[:, pl.ds(t0, kc)] - s * st, -1)
        onehot = jnp.where(lax.broadcasted_iota(I32, (st, kc), 0) == rel, 1.0, 0.0).astype(BF16)
        return _dot(onehot, x_ref[pl.ds(t0, kc), :]), _dot(onehot, ghl_ref[pl.ds(t0, kc), :])

    for s in range(n_st):
        acc_ref[s], gacc_ref[s] = chunk(s, 0)
    for s in range(n_st):
        def body(c, carry, s=s):
            xs, gs = chunk(s, c)
            acc_ref[s] += xs
            gacc_ref[s] += gs
            return carry
        lax.fori_loop(1, nch_ref[b, e, s], body, 0)
    lane = lax.broadcasted_iota(I32, (st, 2 * n_e), 1)
    mine = (lane == e) | (lane == e + n_e)
    for s in range(n_st):
        o_ref[s * st:(s + 1) * st, :] = acc_ref[s].astype(o_ref.dtype)
        gs_ref[s * st:(s + 1) * st, :] = jnp.sum(jnp.where(mine, gacc_ref[s], 0.0), axis=1, keepdims=True)


def _dispatch(jlo, nch, xb, slot4, gate_hl, *, cap, st, kc, tt):
    bsz, t, d = xb.shape
    e = slot4.shape[1]
    assert cap % st == 0 and kc % tt == 0 and t % tt == 0 and kc <= t
    return pl.pallas_call(
        functools.partial(_dispatch_kernel, st=st, kc=kc, tt=tt),
        grid_spec=pltpu.PrefetchScalarGridSpec(
            num_scalar_prefetch=2,
            grid=(bsz, e),
            in_specs=[pl.BlockSpec((None, t, d), lambda i, j, a, c: (i, 0, 0)),
                      pl.BlockSpec((None, None, 1, t), lambda i, j, a, c: (i, j, 0, 0)),
                      pl.BlockSpec((None, t, 2 * e), lambda i, j, a, c: (i, 0, 0))],
            out_specs=[pl.BlockSpec((None, None, cap, d), lambda i, j, a, c: (i, j, 0, 0)),
                       pl.BlockSpec((None, None, cap, 1), lambda i, j, a, c: (i, j, 0, 0))],
            scratch_shapes=[pltpu.VMEM((cap // st, st, d), F32), pltpu.VMEM((cap // st, st, 2 * e), F32)]),
        out_shape=[jax.ShapeDtypeStruct((bsz, e, cap, d), BF16),
                   jax.ShapeDtypeStruct((bsz, e, cap, 1), F32)],
        compiler_params=_cparams("parallel", "parallel"),
        name="dispatch",
    )(jlo, nch, xb, slot4, gate_hl)


def _expert_kernel(x_ref, gs_ref, wg_ref, wu_ref, wd_ref, o_ref, acc_ref):
    f = pl.program_id(2)
    wg = wg_ref[...].astype(BF16)
    wu = wu_ref[...].astype(BF16)
    wd = wd_ref[...].astype(BF16)
    @pl.when(f == 0)
    def _():
        acc_ref[...] = jnp.zeros_like(acc_ref)

    for bi in range(x_ref.shape[0]):
        x = x_ref[bi]
        hg = _dot(x, wg)
        hu = _dot(x, wu)
        act = (hg * _sigmoid(hg) * hu).astype(BF16)
        acc_ref[bi] += _dot(act, wd)

    @pl.when(f == pl.num_programs(2) - 1)
    def _():
        o_ref[...] = (acc_ref[...] * gs_ref[...]).astype(o_ref.dtype)


def _expert(xg, gs, wg, wu, wd, *, fc, nb):
    bsz, e, cap, d = xg.shape
    f = wg.shape[2]
    assert bsz % nb == 0 and f % fc == 0
    tok = lambda last: pl.BlockSpec((nb, None, cap, last), lambda i, j, k: (j, i, 0, 0))
    return pl.pallas_call(
        _expert_kernel,
        grid=(e, bsz // nb, f // fc),
        in_specs=[tok(d), tok(1),
                  pl.BlockSpec((None, d, fc), lambda i, j, k: (i, 0, k)),
                  pl.BlockSpec((None, d, fc), lambda i, j, k: (i, 0, k)),
                  pl.BlockSpec((None, fc, d), lambda i, j, k: (i, k, 0))],
        out_specs=tok(d),
        out_shape=jax.ShapeDtypeStruct((bsz, e, cap, d), BF16),
        scratch_shapes=[pltpu.VMEM((nb, cap, d), F32)],
        compiler_params=_cparams("parallel", "parallel", "arbitrary"),
        name="expert",
    )(xg, gs, wg, wu, wd)


def _combine_kernel(r0_ref, x_ref, slot_ref, g_ref, b_ref, y_ref, o_ref, *, tt, win, align, alpha):
    b, j = pl.program_id(0), pl.program_id(1)
    n_e, cap = y_ref.shape[0], y_ref.shape[1]
    n_sub = x_ref.shape[0] // tt
    for u in range(n_sub):
        rs = slice(u * tt, (u + 1) * tt)
        acc = alpha * x_ref[rs, :]
        slots = slot_ref[rs, :]
        for e in range(n_e):
            start = _window_start(r0_ref[b, e, j * n_sub + u], cap, win, align)
            rel = slots[:, e:e + 1] - start
            onehot = jnp.where(lax.broadcasted_iota(I32, (tt, win), 1) == rel, 1.0, 0.0).astype(BF16)
            acc = acc + _dot(onehot, y_ref[e, pl.ds(start, win), :])
        o_ref[rs, :] = _ln(acc, g_ref[...], b_ref[...])


def _combine(r0, x, slot_t, g, b, y, *, tt, n_sub, win, align, alpha):
    bsz, t, d = x.shape
    e, cap = y.shape[1], y.shape[2]
    assert win >= tt + align and win <= cap and (cap - win) % align == 0
    ts = tt * n_sub
    const = lambda i, j, r: (0, 0)
    return pl.pallas_call(
        functools.partial(_combine_kernel, tt=tt, win=win, align=align, alpha=alpha),
        grid_spec=pltpu.PrefetchScalarGridSpec(
            num_scalar_prefetch=1,
            grid=(bsz, t // ts),
            in_specs=[pl.BlockSpec((None, ts, d), lambda i, j, r: (i, j, 0)),
                      pl.BlockSpec((None, ts, e), lambda i, j, r: (i, j, 0)),
                      pl.BlockSpec((1, d), const), pl.BlockSpec((1, d), const),
                      pl.BlockSpec((None, e, cap, d), lambda i, j, r: (i, 0, 0, 0),
                                   pipeline_mode=pl.Buffered(1))],
            out_specs=pl.BlockSpec((None, ts, d), lambda i, j, r: (i, j, 0))),
        out_shape=jax.ShapeDtypeStruct((bsz, t, d), F32),
        compiler_params=_cparams("parallel", "parallel"),
        name="combine",
    )(r0, x, slot_t, g, b, y)


def _hyena_constants(length):
    n = 2 * length
    m = np.arange(n)
    pos = np.where(m < length, m, n - m).astype(np.float64)
    pos[length] = 0.0
    t = pos / max(length - 1, 1)
    f = np.linspace(1e-4, HY_BANDS - 1, HY_BANDS)
    ang = (2.0 * np.pi * pos / length)[:, None] * f[None, :]
    feat = np.concatenate([t[:, None], np.cos(ang), -np.sin(ang)], axis=-1)
    feat_t = np.zeros((LANES, n), np.float32)
    feat_t[:feat.shape[1]] = feat.T
    valid = np.ones((1, n), np.float32)
    valid[0, length] = 0.0
    return feat_t, t[None, :].astype(np.float32), valid


def _decay_rates(width):
    max_decay = math.log(HY_DECAY_TARGET) / HY_FAST_DECAY
    min_decay = math.log(HY_DECAY_TARGET) / HY_SLOW_DECAY
    return np.abs(np.linspace(min_decay, max_decay, width, dtype=np.float32)).astype(np.float32)


def _dft_constants(r1):
    n, a1h, a2, at = _dft_tables(r1, r1 // 2)
    _, a1f, _, _ = _dft_tables(r1, r1)
    bf = lambda x: jnp.asarray(x, F32).astype(BF16)
    w1_data = bf(_cplx_rhs(a1h, -1.0))
    w1_filt = bf(np.concatenate([np.cos(a1f), -np.sin(a1f)], axis=1))
    w2 = bf(_cplx_rhs(a2, -1.0))
    w3 = bf(_cplx_rhs(a2.T, 1.0))
    w4 = bf(_cplx_rhs(a1h.T, 1.0) / n)
    twr, twi = np.cos(at), -np.sin(at)
    f32 = lambda x: jnp.asarray(x, F32)
    return dict(w1_data=w1_data, w1_filt=w1_filt, w2=w2, w3=w3, w4=w4,
                twr=f32(twr), twi=f32(twi), twr_t=f32(twr.T), twi_t=f32(twi.T))


def kernel(x, mem, ln_in_g, ln_in_b, w_in, b_gate, conf_dw_w, conf_dw_b, conf_ln_g, conf_ln_b, conf_w_out, hy_short_w, hy_short_b, hy_ffn_w1, hy_ffn_b1, hy_freq1, hy_ffn_w2, hy_ffn_b2, hy_freq2, hy_ffn_w3, hy_skip, hy_w_out, w_mix_out, ln_mix_g, ln_mix_b, xa_wq, xa_wk, xa_wv, xa_wo, ln_xa_g, ln_xa_b, moe_w_router, moe_w_gate, moe_w_up, moe_w_down, ln_moe_g, ln_moe_b):
    bsz, length, d = x.shape
    depth = w_in.shape[0]
    alpha = (2.0 * depth) ** 0.25
    cw = conf_dw_w.shape[2]
    hw = hy_skip.shape[2]
    n_glu, n_hy = 2 * cw, 3 * hw
    n_exp = moe_w_router.shape[2]
    cap = max(1, N_EXPERT_CAPACITY * length // n_exp)
    rh = length // LANES
    r1 = 2 * rh
    assert bsz % 2 == 0 and length % LANES == 0

    row = lambda v: v.reshape(1, -1).astype(F32)
    col = lambda v: v.reshape(-1, 1).astype(F32)

    feat_t, t_ext, valid = _hyena_constants(length)
    dft = _dft_constants(r1)
    rates = _decay_rates(hw)
    tri = jnp.asarray(np.triu(np.ones((LANES, LANES), np.float32))).astype(BF16)

    n_ord = hy_skip.shape[1]
    tm = min(512, length)
    lane_b = lambda v: jnp.broadcast_to(v[..., None, None], v.shape + (1, LANES)).astype(F32)
    dft_data = (dft["w1_data"], dft["twr"], dft["twi"], dft["w2"], dft["w3"], dft["twr_t"], dft["twi_t"],
                dft["w4"])

    h = x
    for i in range(depth):
        w1h, w1l = _split_bf16(jnp.pad(hy_ffn_w1[i].T, ((0, 0), (0, LANES - hy_ffn_w1.shape[1]))))
        w2h, w2l = _split_bf16(hy_ffn_w2[i].T)
        h2t = _filt_ffn(jnp.asarray(feat_t), w1h, w1l, col(hy_ffn_b1[i]), col(hy_freq1[i]),
                        w2h, w2l, col(hy_ffn_b2[i]), col(hy_freq2[i]), tn=min(2048, 2 * length))
        w3 = hy_ffn_w3[i].reshape(-1, n_ord, 2, hw)
        w3f = jnp.transpose(w3[:, :, 0], (1, 2, 0)).reshape(n_ord * hw, -1)
        w3b = jnp.transpose(w3[:, :, 1], (1, 2, 0)).reshape(n_ord * hw, -1)
        w3fh, w3fl = _split_bf16(w3f)
        w3bh, w3bl = _split_bf16(w3b)
        delta = jnp.asarray(np.tile(rates, n_ord).reshape(-1, 1))
        kern = _filt(h2t, w3fh, w3fl, w3bh, w3bl, delta, jnp.asarray(t_ext), jnp.asarray(valid),
                     g=min(128, hw))
        kf = _kfft(kern.reshape(n_ord * hw, r1, LANES), dft["w1_filt"], dft["twr"], dft["twi"], dft["w2"],
                   g=min(32, hw))

        wi = w_in[i]
        wglu = wi[:, :n_glu].astype(BF16)
        whyt = wi[:, n_glu:n_glu + n_hy].T.astype(BF16)
        wgate = wi[:, n_glu + n_hy:].astype(BF16)
        hn, u, hyt, gates = _inproj(h, row(ln_in_g), row(ln_in_b), wglu, whyt, wgate, row(b_gate[i]),
                                    tm=tm, apply_ln=(i == 0))

        ua = _conf(u, conf_dw_w[i], row(conf_dw_b[i]), row(conf_ln_g[i]), row(conf_ln_b[i]),
                   tl=min(1024, length))

        z = _hyena(hyt.reshape(bsz, n_hy, rh, LANES), lane_b(hy_short_w[i]), lane_b(hy_short_b[i]), kf,
                   lane_b(hy_skip[i]), dft_data, g=min(32, hw))

        x1 = _mix(ua, z, gates, hn, conf_w_out[i].astype(BF16),
                  hy_w_out[i].astype(BF16), w_mix_out[i].astype(BF16), row(ln_mix_g[i]), row(ln_mix_b[i]),
                  tm=min(1024, length), sub=512, alpha=alpha)

        k, v = _kv(mem, xa_wk[i].astype(BF16), xa_wv[i].astype(BF16))
        wrh, wrl = _split_bf16(moe_w_router[i].T)
        x2, x2b, aff = _xattn(x1, k, v, xa_wq[i].astype(BF16), xa_wo[i].astype(BF16), row(ln_xa_g[i]),
                              row(ln_xa_b[i]), wrh, wrl, tm=min(1024, length), sub=512, alpha=alpha,
                              heads=XA_HEADS)

        tt, st, align = LANES, LANES, 16
        kc = min(10 * tt, length)
        slot, slot_t, gate_hl, r0, jlo, nch = _select(aff, tri, cap=cap, st=st, kct=kc // tt)
        xg, gslot = _dispatch(jlo, nch, x2b, slot.reshape(bsz, n_exp, 1, length), gate_hl,
                              cap=cap, st=st, kc=kc, tt=tt)
        y = _expert(xg, gslot, moe_w_gate[i], moe_w_up[i], moe_w_down[i],
                    fc=min(512, moe_w_gate.shape[3]), nb=2)
        h = _combine(r0, x2, slot_t, row(ln_moe_g[i]), row(ln_moe_b[i]), y,
                     tt=tt, n_sub=min(4, length // tt), win=tt + align, align=align, alpha=alpha)
    return h
```

```python
import functools
import math

import numpy as np
import jax
import jax.numpy as jnp
from jax import lax
from jax.experimental import pallas as pl
from jax.experimental.pallas import tpu as pltpu

F32 = jnp.float32
BF16 = jnp.bfloat16
I32 = jnp.int32

LANES = 128
SUBLANES = 8
VMEM_LIMIT_BYTES = 56 * 1024 * 1024
SUB_ROWS = 256

LN_EPS = 1e-5
XA_HEADS = 4
N_EXPERT_CAPACITY = 2
HY_DECAY_TARGET = 1e-2
HY_FAST_DECAY = 0.3
HY_SLOW_DECAY = 1.5
HY_BANDS = 16


def _cparams(*sem):
    return pltpu.CompilerParams(dimension_semantics=sem, vmem_limit_bytes=VMEM_LIMIT_BYTES)


def _ln(x, g, b):
    mu = jnp.mean(x, axis=-1, keepdims=True)
    xc = x - mu
    var = jnp.mean(xc * xc, axis=-1, keepdims=True)
    return xc * lax.rsqrt(var + LN_EPS) * g + b


def _sigmoid(x):
    return 1.0 / (1.0 + jnp.exp(-x))


def _dot(a, b):
    return jnp.dot(a, b, preferred_element_type=F32)


def _dot_nt(a, b):
    return lax.dot_general(a, b, (((1,), (1,)), ((), ())), preferred_element_type=F32)


def _split_bf16(x):
    hi = x.astype(BF16)
    lo = (x - hi.astype(F32)).astype(BF16)
    return hi, lo


def _dot3(a, b_hi, b_lo):
    a_hi, a_lo = _split_bf16(a)
    return _dot(a_hi, b_hi) + (_dot(a_hi, b_lo) + _dot(a_lo, b_hi))


def _dot3_lhs(a_hi, a_lo, b):
    b_hi, b_lo = _split_bf16(b)
    return _dot(a_hi, b_hi) + (_dot(a_hi, b_lo) + _dot(a_lo, b_hi))


def _inproj_kernel(x_ref, g_ref, b_ref, wglu_ref, whyt_ref, wgate_ref, bgate_ref,
                   h_ref, u_ref, hyt_ref, gate_ref, *, apply_ln, sub):
    for r in range(0, x_ref.shape[0], sub):
        rs = slice(r, r + sub)
        h = _ln(x_ref[rs, :], g_ref[...], b_ref[...]) if apply_ln else x_ref[rs, :]
        h_ref[rs, :] = h
        hb = h.astype(BF16)
        glu = _dot(hb, wglu_ref[...])
        cw = glu.shape[1] // 2
        u_ref[rs, :] = glu[:, :cw] * _sigmoid(glu[:, cw:])
        hyt_ref[:, rs] = _dot_nt(whyt_ref[...], hb).astype(hyt_ref.dtype)
        gl = _dot(hb, wgate_ref[...]) + bgate_ref[...]
        gate_ref[rs, :] = _sigmoid(gl).astype(BF16)


def _inproj(x, g, b, wglu, whyt, wgate, bgate, *, tm, apply_ln):
    bsz, length, d = x.shape
    n_glu, n_hy, n_gate = wglu.shape[1], whyt.shape[0], wgate.shape[1]
    const = lambda i, j: (0, 0)
    return pl.pallas_call(
        functools.partial(_inproj_kernel, apply_ln=apply_ln, sub=min(SUB_ROWS, tm)),
        grid=(bsz, length // tm),
        in_specs=[
            pl.BlockSpec((None, tm, d), lambda i, j: (i, j, 0)),
            pl.BlockSpec((1, d), const), pl.BlockSpec((1, d), const),
            pl.BlockSpec((d, n_glu), const),
            pl.BlockSpec((n_hy, d), const),
            pl.BlockSpec((d, n_gate), const),
            pl.BlockSpec((1, n_gate), const),
        ],
        out_specs=[
            pl.BlockSpec((None, tm, d), lambda i, j: (i, j, 0)),
            pl.BlockSpec((None, tm, n_glu // 2), lambda i, j: (i, j, 0)),
            pl.BlockSpec((None, n_hy, tm), lambda i, j: (i, 0, j)),
            pl.BlockSpec((None, tm, n_gate), lambda i, j: (i, j, 0)),
        ],
        out_shape=[
            jax.ShapeDtypeStruct((bsz, length, d), F32),
            jax.ShapeDtypeStruct((bsz, length, n_glu // 2), F32),
            jax.ShapeDtypeStruct((bsz, n_hy, length), BF16),
            jax.ShapeDtypeStruct((bsz, length, n_gate), BF16),
        ],
        compiler_params=_cparams("parallel", "parallel"),
        name="inproj",
    )(x, g, b, wglu, whyt, wgate, bgate)


def _conf_kernel(prev_ref, cur_ref, next_ref, w_ref, cb_ref, g_ref, b_ref, o_ref, ext_ref, sh_ref,
                 *, tl, halo, ksize, rows):
    j = pl.program_id(1)
    nj = pl.num_programs(1)
    pad = (ksize - 1) // 2
    ext_ref[0:halo, :] = jnp.where(j > 0, prev_ref[...], 0.0)
    ext_ref[halo:halo + tl, :] = cur_ref[...]
    ext_ref[halo + tl:halo + tl + halo, :] = jnp.where(j < nj - 1, next_ref[...], 0.0)
    n_sh = sh_ref.shape[1]
    for r in range(1, SUBLANES):
        sh_ref[r - 1, :, :] = ext_ref[r:r + n_sh, :]
    w = w_ref[...]
    cb, g, b = cb_ref[...], g_ref[...], b_ref[...]
    for r0 in range(0, tl, rows):
        acc = jnp.zeros((rows, w.shape[1]), F32)
        for k in range(ksize):
            q, r = divmod(halo - pad + k + r0, SUBLANES)
            q *= SUBLANES
            src = ext_ref[q:q + rows, :] if r == 0 else sh_ref[r - 1, q:q + rows, :]
            acc = acc + w[k:k + 1, :] * src
        y = _ln(acc + cb, g, b)
        o_ref[r0:r0 + rows, :] = (y * _sigmoid(y)).astype(o_ref.dtype)


def _conf(u, w, cb, g, b, *, tl, halo=16, rows=64):
    bsz, length, c = u.shape
    ksize = w.shape[0]
    assert (ksize - 1) // 2 <= halo and tl % halo == 0 and tl % rows == 0
    nh = tl // halo
    last = length // halo - 1
    const = lambda i, j: (0, 0)
    return pl.pallas_call(
        functools.partial(_conf_kernel, tl=tl, halo=halo, ksize=ksize, rows=rows),
        grid=(bsz, length // tl),
        in_specs=[
            pl.BlockSpec((None, halo, c), lambda i, j: (i, jnp.maximum(j * nh - 1, 0), 0)),
            pl.BlockSpec((None, tl, c), lambda i, j: (i, j, 0)),
            pl.BlockSpec((None, halo, c), lambda i, j: (i, jnp.minimum((j + 1) * nh, last), 0)),
            pl.BlockSpec((ksize, c), const),
            pl.BlockSpec((1, c), const), pl.BlockSpec((1, c), const), pl.BlockSpec((1, c), const),
        ],
        out_specs=pl.BlockSpec((None, tl, c), lambda i, j: (i, j, 0)),
        out_shape=jax.ShapeDtypeStruct((bsz, length, c), BF16),
        scratch_shapes=[pltpu.VMEM((tl + 2 * halo, c), F32),
                        pltpu.VMEM((SUBLANES - 1, tl + 2 * halo - SUBLANES, c), F32)],
        compiler_params=_cparams("parallel", "parallel"),
        name="conf",
    )(u, u, u, w, cb, g, b)


def _filt_ffn_kernel(feat_ref, w1h_ref, w1l_ref, b1_ref, f1_ref, w2h_ref, w2l_ref, b2_ref, f2_ref, o_ref):
    a = _dot3_lhs(w1h_ref[...], w1l_ref[...], feat_ref[...]) + b1_ref[...]
    h = jnp.sin(f1_ref[...] * a)
    a2 = _dot3_lhs(w2h_ref[...], w2l_ref[...], h) + b2_ref[...]
    o_ref[...] = jnp.sin(f2_ref[...] * a2)


def _filt_ffn(feat_t, w1h, w1l, b1, f1, w2h, w2l, b2, f2, *, tn):
    kp, n = feat_t.shape
    m = w1h.shape[0]
    const = lambda i: (0, 0)
    return pl.pallas_call(
        _filt_ffn_kernel,
        grid=(n // tn,),
        in_specs=[pl.BlockSpec((kp, tn), lambda i: (0, i)),
                  pl.BlockSpec((m, kp), const), pl.BlockSpec((m, kp), const),
                  pl.BlockSpec((m, 1), const), pl.BlockSpec((m, 1), const),
                  pl.BlockSpec((m, m), const), pl.BlockSpec((m, m), const),
                  pl.BlockSpec((m, 1), const), pl.BlockSpec((m, 1), const)],
        out_specs=pl.BlockSpec((m, tn), lambda i: (0, i)),
        out_shape=jax.ShapeDtypeStruct((m, n), F32),
        compiler_params=_cparams("parallel"),
        name="filt_ffn",
    )(feat_t, w1h, w1l, b1, f1, w2h, w2l, b2, f2)


def _filt_kernel(h2_ref, w3fh_ref, w3fl_ref, w3bh_ref, w3bl_ref, delta_ref, text_ref, valid_ref, o_ref):
    n = h2_ref.shape[1]
    half = n // 2
    decay = jnp.exp(-text_ref[...] * delta_ref[...]) * valid_ref[...]
    kf = _dot3_lhs(w3fh_ref[...], w3fl_ref[...], h2_ref[:, 0:half]) * decay[:, 0:half]
    kb = _dot3_lhs(w3bh_ref[...], w3bl_ref[...], h2_ref[:, half:n]) * decay[:, half:n]
    ss = jnp.sum(kf * kf, axis=1, keepdims=True) + jnp.sum(kb * kb, axis=1, keepdims=True)
    scale = lax.rsqrt(ss + 1e-6)
    o_ref[:, 0:half] = kf * scale
    o_ref[:, half:n] = kb * scale


def _filt(h2t, w3fh, w3fl, w3bh, w3bl, delta, text, valid, *, g):
    m, n = h2t.shape
    rows = w3fh.shape[0]
    const = lambda i: (0, 0)
    return pl.pallas_call(
        _filt_kernel,
        grid=(rows // g,),
        in_specs=[pl.BlockSpec((m, n), const),
                  pl.BlockSpec((g, m), lambda i: (i, 0)), pl.BlockSpec((g, m), lambda i: (i, 0)),
                  pl.BlockSpec((g, m), lambda i: (i, 0)), pl.BlockSpec((g, m), lambda i: (i, 0)),
                  pl.BlockSpec((g, 1), lambda i: (i, 0)),
                  pl.BlockSpec((1, n), const), pl.BlockSpec((1, n), const)],
        out_specs=pl.BlockSpec((g, n), lambda i: (i, 0)),
        out_shape=jax.ShapeDtypeStruct((rows, n), F32),
        compiler_params=_cparams("parallel"),
        name="filt",
    )(h2t, w3fh, w3fl, w3bh, w3bl, delta, text, valid)


def _dft_tables(r1, n1_used):
    n = r1 * LANES
    a1 = 2.0 * np.pi * np.outer(np.arange(n1_used), np.arange(r1)) / r1
    a2 = 2.0 * np.pi * np.outer(np.arange(LANES), np.arange(LANES)) / LANES
    at = 2.0 * np.pi * np.outer(np.arange(LANES), np.arange(r1)) / n
    return n, a1, a2, at


def _cplx_rhs(ang, sign):
    c, s = np.cos(ang), sign * np.sin(ang)
    return np.block([[c, s], [-s, c]])


def _fwd_steps(x_t, w1, twr, twi, w2):
    g, n2, k1w = x_t.shape
    r1 = twr.shape[1]
    a = _dot(x_t.reshape(g * n2, k1w).astype(BF16), w1).reshape(g, n2, 2 * r1)
    ar, ai = a[:, :, :r1], a[:, :, r1:]
    br = ar * twr - ai * twi
    bi = ar * twi + ai * twr
    bt = jnp.concatenate([jnp.swapaxes(br.astype(BF16), 1, 2), jnp.swapaxes(bi.astype(BF16), 1, 2)],
                         axis=-1)
    c = _dot(bt.reshape(g * r1, 2 * n2), w2)
    return c.reshape(g, r1, 2 * n2)


def _inv_steps(d, w3, twr_t, twi_t, w4):
    g, r1, w = d.shape
    n2 = w // 2
    e = _dot(d.reshape(g * r1, w).astype(BF16), w3).reshape(g, r1, w)
    er, ei = e[:, :, :n2], e[:, :, n2:]
    fr = er * twr_t + ei * twi_t
    fi = ei * twr_t - er * twi_t
    ft = jnp.concatenate([jnp.swapaxes(fr.astype(BF16), 1, 2), jnp.swapaxes(fi.astype(BF16), 1, 2)],
                         axis=-1)
    y = _dot(ft.reshape(g * n2, 2 * r1), w4)
    return y.reshape(g, n2, w4.shape[1])


def _kfft_kernel(k_ref, w1_ref, twr_ref, twi_ref, w2_ref, o_ref):
    x_t = jnp.swapaxes(k_ref[...].astype(BF16), 1, 2)
    o_ref[...] = _fwd_steps(x_t, w1_ref[...], twr_ref[...], twi_ref[...], w2_ref[...])


def _kfft(k3, w1, twr, twi, w2, *, g):
    rows, r1, lanes = k3.shape
    const = lambda i: (0, 0)
    return pl.pallas_call(
        _kfft_kernel,
        grid=(rows // g,),
        in_specs=[pl.BlockSpec((g, r1, lanes), lambda i: (i, 0, 0)),
                  pl.BlockSpec(w1.shape, const), pl.BlockSpec(twr.shape, const),
                  pl.BlockSpec(twi.shape, const), pl.BlockSpec(w2.shape, const)],
        out_specs=pl.BlockSpec((g, r1, 2 * lanes), lambda i: (i, 0, 0)),
        out_shape=jax.ShapeDtypeStruct((rows, r1, 2 * lanes), F32),
        compiler_params=_cparams("parallel"),
        name="kfft",
    )(k3, w1, twr, twi, w2)


def _short_conv(a, w, b):
    p, g, r, l = a.shape
    a3 = a.reshape(p * g, r, l)
    lane = lax.broadcasted_iota(I32, a3.shape, 2)
    row = lax.broadcasted_iota(I32, a3.shape, 1)
    pl_ = pltpu.roll(a3, 1, 2)
    ql = pltpu.roll(pl_, 1, 1)
    prev = jnp.where(lane == 0, jnp.where(row == 0, 0.0, ql), pl_)
    pr = pltpu.roll(a3, l - 1, 2)
    qr = pltpu.roll(pr, r - 1, 1)
    nxt = jnp.where(lane == l - 1, jnp.where(row == r - 1, 0.0, qr), pr)
    prev, nxt = prev.reshape(a.shape), nxt.reshape(a.shape)
    return w[0][None] * prev + w[1][None] * a + w[2][None] * nxt + b[None]


def _hyena_kernel(x1_ref, x2_ref, v_ref, sw1_ref, sb1_ref, sw2_ref, sb2_ref, swv_ref, sbv_ref,
                  kf0_ref, kf1_ref, skip_ref, w1_ref, twr_ref, twi_ref, w2_ref, w3_ref,
                  twrt_ref, twit_ref, w4_ref, z_ref):
    x1 = _short_conv(x1_ref[...].astype(F32), sw1_ref[...], sb1_ref[...])
    x2 = _short_conv(x2_ref[...].astype(F32), sw2_ref[...], sb2_ref[...])
    z = _short_conv(v_ref[...].astype(F32), swv_ref[...], sbv_ref[...])
    rh = z.shape[2]
    for gate, kf_ref, o in ((x1, kf0_ref, 0), (x2, kf1_ref, 1)):
        s = jnp.concatenate([z[0], z[1]], axis=1)
        c = _fwd_steps(jnp.swapaxes(s, 1, 2), w1_ref[...], twr_ref[...], twi_ref[...], w2_ref[...])
        kf = kf_ref[...]
        n2 = c.shape[2] // 2
        cr, ci, kr, ki = c[:, :, :n2], c[:, :, n2:], kf[:, :, :n2], kf[:, :, n2:]
        d = jnp.concatenate([cr * kr - ci * ki, cr * ki + ci * kr], axis=-1)
        y = _inv_steps(d, w3_ref[...], twrt_ref[...], twit_ref[...], w4_ref[...])
        y = jnp.swapaxes(y, 1, 2)
        conv = jnp.stack([y[:, :rh], y[:, rh:]], axis=0)
        z = gate * (conv + skip_ref[o][None] * z)
    z_ref[...] = z


def _hyena(hy4, sw, sb, kf, skip, tabs, *, g):
    bsz, c3, rh, lanes = hy4.shape
    c = c3 // 3
    r1 = 2 * rh
    ncb = c // g
    w1, twr, twi, w2, w3, twrt, twit, w4 = tabs
    const2 = lambda i, p: (0, 0)
    data = lambda off: pl.BlockSpec((2, g, rh, lanes), lambda i, p, off=off: (p, off + i, 0, 0))
    wspec = lambda off: pl.BlockSpec((3, g, 1, lanes), lambda i, p, off=off: (0, off + i, 0, 0))
    bspec = lambda off: pl.BlockSpec((g, 1, lanes), lambda i, p, off=off: (off + i, 0, 0))
    kspec = lambda off: pl.BlockSpec((g, r1, 2 * lanes), lambda i, p, off=off: (off + i, 0, 0))
    return pl.pallas_call(
        _hyena_kernel,
        grid=(ncb, bsz // 2),
        in_specs=[data(0), data(ncb), data(2 * ncb),
                  wspec(0), bspec(0), wspec(ncb), bspec(ncb), wspec(2 * ncb), bspec(2 * ncb),
                  kspec(0), kspec(ncb),
                  pl.BlockSpec((2, g, 1, lanes), lambda i, p: (0, i, 0, 0)),
                  pl.BlockSpec(w1.shape, const2), pl.BlockSpec(twr.shape, const2),
                  pl.BlockSpec(twi.shape, const2), pl.BlockSpec(w2.shape, const2),
                  pl.BlockSpec(w3.shape, const2), pl.BlockSpec(twrt.shape, const2),
                  pl.BlockSpec(twit.shape, const2), pl.BlockSpec(w4.shape, const2)],
        out_specs=pl.BlockSpec((2, g, rh, lanes), lambda i, p: (p, i, 0, 0)),
        out_shape=jax.ShapeDtypeStruct((bsz, c, rh, lanes), F32),
        compiler_params=_cparams("parallel", "parallel"),
        name="hyena",
    )(hy4, hy4, hy4, sw, sb, sw, sb, sw, sb, kf, kf, skip, w1, twr, twi, w2, w3, twrt, twit, w4)


def _mix_kernel(ua_ref, z_ref, gate_ref, h_ref, wa_ref, wb_ref, wm_ref, g_ref, b_ref, o_ref, *, alpha, sub):
    for r in range(0, h_ref.shape[0], sub):
        rs = slice(r, r + sub)
        ya = _dot(ua_ref[rs, :], wa_ref[...])
        zc = jnp.concatenate([z_ref[:, k, :] for k in range(r // LANES, (r + sub) // LANES)], axis=-1)
        yb = _dot(jnp.transpose(zc).astype(BF16), wb_ref[...])
        gt = gate_ref[rs, :].astype(F32)
        d = ya.shape[1]
        m = gt[:, :d] * ya + gt[:, d:] * yb
        mixed = _dot(m.astype(BF16), wm_ref[...])
        o_ref[rs, :] = _ln(alpha * h_ref[rs, :] + mixed, g_ref[...], b_ref[...])


def _mix(ua, z4, gates, h, wa, wb, wm, g, b, *, tm, sub, alpha):
    bsz, length, d = h.shape
    c = ua.shape[2]
    sub = min(sub, tm)
    assert sub % LANES == 0 and ((tm // LANES) % SUBLANES == 0 or tm == length)
    const = lambda i, j: (0, 0)
    return pl.pallas_call(
        functools.partial(_mix_kernel, alpha=alpha, sub=sub),
        grid=(bsz, length // tm),
        in_specs=[pl.BlockSpec((None, tm, c), lambda i, j: (i, j, 0)),
                  pl.BlockSpec((None, c, tm // LANES, LANES), lambda i, j: (i, 0, j, 0)),
                  pl.BlockSpec((None, tm, 2 * d), lambda i, j: (i, j, 0)),
                  pl.BlockSpec((None, tm, d), lambda i, j: (i, j, 0)),
                  pl.BlockSpec((c, d), const), pl.BlockSpec((c, d), const), pl.BlockSpec((d, d), const),
                  pl.BlockSpec((1, d), const), pl.BlockSpec((1, d), const)],
        out_specs=pl.BlockSpec((None, tm, d), lambda i, j: (i, j, 0)),
        out_shape=jax.ShapeDtypeStruct((bsz, length, d), F32),
        compiler_params=_cparams("parallel", "parallel"),
        name="mix",
    )(ua, z4, gates, h, wa, wb, wm, g, b)


def _kv_kernel(mem_ref, wk_ref, wv_ref, k_ref, v_ref):
    mb = mem_ref[...].astype(BF16)
    k_ref[...] = _dot(mb, wk_ref[...]).astype(BF16)
    v_ref[...] = _dot(mb, wv_ref[...]).astype(BF16)


def _kv(mem, wk, wv):
    bsz, m, d = mem.shape
    const = lambda i: (0, 0)
    blk = pl.BlockSpec((None, m, d), lambda i: (i, 0, 0))
    return pl.pallas_call(
        _kv_kernel,
        grid=(bsz,),
        in_specs=[blk, pl.BlockSpec((d, d), const), pl.BlockSpec((d, d), const)],
        out_specs=[blk, blk],
        out_shape=[jax.ShapeDtypeStruct((bsz, m, d), BF16)] * 2,
        compiler_params=_cparams("parallel"),
        name="kv",
    )(mem, wk, wv)


def _xattn_kernel(x_ref, k_ref, v_ref, wq_ref, wo_ref, g_ref, b_ref, wrh_ref, wrl_ref,
                  x2_ref, x2b_ref, aff_ref, *, alpha, heads, sub):
    d = x_ref.shape[1]
    dh = d // heads
    wrh, wrl = wrh_ref[...], wrl_ref[...]
    for r in range(0, x_ref.shape[0], sub):
        rs = slice(r, r + sub)
        x = x_ref[rs, :]
        q = (_dot(x.astype(BF16), wq_ref[...]) * (dh ** -0.5)).astype(BF16)
        outs = []
        for hd in range(heads):
            sl = slice(hd * dh, (hd + 1) * dh)
            s = _dot_nt(q[:, sl], k_ref[:, sl])
            s = s - jnp.max(s, axis=-1, keepdims=True)
            p = jnp.exp(s)
            p = p / jnp.sum(p, axis=-1, keepdims=True)
            outs.append(_dot(p.astype(BF16), v_ref[:, sl]))
        o = jnp.concatenate(outs, axis=-1)
        xa = _dot(o.astype(BF16), wo_ref[...])
        x2 = _ln(alpha * x + xa, g_ref[...], b_ref[...])
        x2_ref[rs, :] = x2
        x2h, x2l = _split_bf16(x2)
        x2b_ref[rs, :] = x2h
        n_e = wrh.shape[0]
        both = _dot_nt(jnp.concatenate([wrh, wrl], axis=0), x2h)
        logits = both[:n_e] + (_dot_nt(wrh, x2l) + both[n_e:])
        logits = logits - jnp.max(logits, axis=0, keepdims=True)
        ex = jnp.exp(logits)
        aff_ref[:, rs] = ex / jnp.sum(ex, axis=0, keepdims=True)


def _xattn(x, k, v, wq, wo, g, b, wrh, wrl, *, tm, sub, alpha, heads):
    bsz, length, d = x.shape
    m = k.shape[1]
    e = wrh.shape[0]
    const = lambda i, j: (0, 0)
    tok = pl.BlockSpec((None, tm, d), lambda i, j: (i, j, 0))
    mem = pl.BlockSpec((None, m, d), lambda i, j: (i, 0, 0))
    return pl.pallas_call(
        functools.partial(_xattn_kernel, alpha=alpha, heads=heads, sub=min(sub, tm)),
        grid=(bsz, length // tm),
        in_specs=[tok, mem, mem, pl.BlockSpec((d, d), const), pl.BlockSpec((d, d), const),
                  pl.BlockSpec((1, d), const), pl.BlockSpec((1, d), const),
                  pl.BlockSpec((e, d), const), pl.BlockSpec((e, d), const)],
        out_specs=[tok, tok, pl.BlockSpec((None, e, tm), lambda i, j: (i, 0, j))],
        out_shape=[jax.ShapeDtypeStruct((bsz, length, d), F32),
                   jax.ShapeDtypeStruct((bsz, length, d), BF16),
                   jax.ShapeDtypeStruct((bsz, e, length), F32)],
        compiler_params=_cparams("parallel", "parallel"),
        name="xattn",
    )(x, k, v, wq, wo, g, b, wrh, wrl)


def _excl_cumsum_lanes(mask_f32, tri, write):
    e, t = mask_f32.shape
    carry = jnp.zeros((e, 1), F32)
    for c in range(t // LANES):
        m = mask_f32[:, c * LANES:(c + 1) * LANES]
        inc = _dot(m.astype(BF16), tri)
        write(c, inc - m + carry, carry)
        carry = carry + inc[:, LANES - 1:LANES]


def _select_kernel(aff_ref, tri_ref, slot_ref, slott_ref, ghl_ref, r0_ref, jlo_ref, nch_ref, cnt_ref, cum_ref,
                   *, cap, st, kct):
    a = aff_ref[...]
    e = a.shape[0]
    v = jnp.zeros((e, 1), I32)
    for bit in range(30, -1, -1):
        cand = v | (1 << bit)
        cnt = jnp.sum(jnp.where(a >= pltpu.bitcast(cand, F32), 1.0, 0.0), axis=1, keepdims=True)
        v = jnp.where(cnt >= cap, cand, v)
    thr = pltpu.bitcast(v, F32)
    gt = a > thr
    eq = a == thr
    need = cap - jnp.sum(jnp.where(gt, 1.0, 0.0), axis=1, keepdims=True)
    tri = tri_ref[...]

    def write_eq(c, val, before):
        cnt_ref[:, c * LANES:(c + 1) * LANES] = val
    _excl_cumsum_lanes(jnp.where(eq, 1.0, 0.0), tri, write_eq)
    sel = gt | (eq & (cnt_ref[...] < need))
    gate = jnp.where(sel, a, 0.0)

    n_tiles = a.shape[1] // LANES

    def write_rank(c, val, before):
        cnt_ref[:, c * LANES:(c + 1) * LANES] = val
        cum_ref[:, c:c + 1] = before
    _excl_cumsum_lanes(jnp.where(sel, 1.0, 0.0), tri, write_rank)
    slot = jnp.where(sel, cnt_ref[...], -1.0)
    slot_ref[...] = slot.astype(I32)
    slott_ref[...] = jnp.transpose(slot).astype(I32)
    g_hi = gate.astype(BF16).astype(F32)
    g_lo = (gate - g_hi).astype(BF16).astype(F32)
    ghl_ref[...] = jnp.transpose(jnp.concatenate([g_hi, g_lo], axis=0)).astype(BF16)

    before = cum_ref[:, 0:n_tiles]
    r0_ref[...] = before.astype(I32)
    after = jnp.concatenate([before[:, 1:], jnp.full((e, 1), cap, F32)], axis=1)
    for s in range(cap // st):
        lo = jnp.sum(jnp.where(after <= s * st, 1.0, 0.0), axis=1, keepdims=True)
        hi = jnp.sum(jnp.where(before < (s + 1) * st, 1.0, 0.0), axis=1, keepdims=True) - 1.0
        n = jnp.ones_like(lo)
        for m in range(1, -(-n_tiles // kct)):
            n = n + jnp.where(hi - lo >= m * kct, 1.0, 0.0)
        jlo_ref[:, s:s + 1] = lo.astype(I32)
        nch_ref[:, s:s + 1] = n.astype(I32)


def _select(aff, tri, *, cap, st, kct):
    bsz, e, t = aff.shape
    n_tiles, n_st = t // LANES, cap // st
    blk = pl.BlockSpec((None, e, t), lambda i: (i, 0, 0))
    small = lambda n: pl.BlockSpec((None, e, n), lambda i: (i, 0, 0))
    return pl.pallas_call(
        functools.partial(_select_kernel, cap=cap, st=st, kct=kct),
        grid=(bsz,),
        in_specs=[blk, pl.BlockSpec((LANES, LANES), lambda i: (0, 0))],
        out_specs=[blk, pl.BlockSpec((None, t, e), lambda i: (i, 0, 0)),
                   pl.BlockSpec((None, t, 2 * e), lambda i: (i, 0, 0)),
                   small(n_tiles), small(n_st), small(n_st)],
        out_shape=[jax.ShapeDtypeStruct((bsz, e, t), I32), jax.ShapeDtypeStruct((bsz, t, e), I32),
                   jax.ShapeDtypeStruct((bsz, t, 2 * e), BF16),
                   jax.ShapeDtypeStruct((bsz, e, n_tiles), I32),
                   jax.ShapeDtypeStruct((bsz, e, n_st), I32), jax.ShapeDtypeStruct((bsz, e, n_st), I32)],
        scratch_shapes=[pltpu.VMEM((e, t), F32), pltpu.VMEM((e, LANES), F32)],
        compiler_params=_cparams("parallel"),
        name="select",
    )(aff, tri)


def _window_start(r0, cap, win, align):
    return pl.multiple_of(jnp.minimum((r0 // align) * align, cap - win), align)


def _dispatch_kernel(jlo_ref, nch_ref, x_ref, slot_ref, ghl_ref, o_ref, gs_ref, acc_ref, gacc_ref,
                     *, st, kc, tt):
    b, e = pl.program_id(0), pl.program_id(1)
    t_total = x_ref.shape[0]
    n_e = ghl_ref.shape[1] // 2
    n_st = o_ref.shape[0] // st

    def chunk(s, c):
        want = (jlo_ref[b, e, s] + c * (kc // tt)) * tt
        t0 = pl.multiple_of(jnp.minimum(want, t_total - kc), tt)
        tok = t0 + lax.broadcasted_iota(I32, (1, kc), 1)
        rel = jnp.where(tok >= want, slot_ref[:, pl.ds(t0, kc)] - s * st, -1)
        onehot = jnp.where(lax.broadcasted_iota(I32, (st, kc), 0) == rel, 1.0, 0.0).astype(BF16)
        return _dot(onehot, x_ref[pl.ds(t0, kc), :]), _dot(onehot, ghl_ref[pl.ds(t0, kc), :])

    for s in range(n_st):
        acc_ref[s], gacc_ref[s] = chunk(s, 0)
    for s in range(n_st):
        def body(c, carry, s=s):
            xs, gs = chunk(s, c)
            acc_ref[s] += xs
            gacc_ref[s] += gs
            return carry
        lax.fori_loop(1, nch_ref[b, e, s], body, 0)
    lane = lax.broadcasted_iota(I32, (st, 2 * n_e), 1)
    mine = (lane == e) | (lane == e + n_e)
    for s in range(n_st):
        o_ref[s * st:(s + 1) * st, :] = acc_ref[s].astype(o_ref.dtype)
        gs_ref[s * st:(s + 1) * st, :] = jnp.sum(jnp.where(mine, gacc_ref[s], 0.0), axis=1, keepdims=True)


def _dispatch(jlo, nch, xb, slot4, gate_hl, *, cap, st, kc, tt):
    bsz, t, d = xb.shape
    e = slot4.shape[1]
    assert cap % st == 0 and kc % tt == 0 and t % tt == 0 and kc <= t
    return pl.pallas_call(
        functools.partial(_dispatch_kernel, st=st, kc=kc, tt=tt),
        grid_spec=pltpu.PrefetchScalarGridSpec(
            num_scalar_prefetch=2,
            grid=(bsz, e),
            in_specs=[pl.BlockSpec((None, t, d), lambda i, j, a, c: (i, 0, 0)),
                      pl.BlockSpec((None, None, 1, t), lambda i, j, a, c: (i, j, 0, 0)),
                      pl.BlockSpec((None, t, 2 * e), lambda i, j, a, c: (i, 0, 0))],
            out_specs=[pl.BlockSpec((None, None, cap, d), lambda i, j, a, c: (i, j, 0, 0)),
                       pl.BlockSpec((None, None, cap, 1), lambda i, j, a, c: (i, j, 0, 0))],
            scratch_shapes=[pltpu.VMEM((cap // st, st, d), F32), pltpu.VMEM((cap // st, st, 2 * e), F32)]),
        out_shape=[jax.ShapeDtypeStruct((bsz, e, cap, d), BF16),
                   jax.ShapeDtypeStruct((bsz, e, cap, 1), F32)],
        compiler_params=_cparams("parallel", "parallel"),
        name="dispatch",
    )(jlo, nch, xb, slot4, gate_hl)


def _expert_kernel(x_ref, gs_ref, wg_ref, wu_ref, wd_ref, o_ref, acc_ref):
    f = pl.program_id(2)
    wg = wg_ref[...].astype(BF16)
    wu = wu_ref[...].astype(BF16)
    wd = wd_ref[...].astype(BF16)
    @pl.when(f == 0)
    def _():
        acc_ref[...] = jnp.zeros_like(acc_ref)

    for bi in range(x_ref.shape[0]):
        x = x_ref[bi]
        hg = _dot(x, wg)
        hu = _dot(x, wu)
        act = (hg * _sigmoid(hg) * hu).astype(BF16)
        acc_ref[bi] += _dot(act, wd)

    @pl.when(f == pl.num_programs(2) - 1)
    def _():
        o_ref[...] = (acc_ref[...] * gs_ref[...]).astype(o_ref.dtype)


def _expert(xg, gs, wg, wu, wd, *, fc, nb):
    bsz, e, cap, d = xg.shape
    f = wg.shape[2]
    assert bsz % nb == 0 and f % fc == 0
    tok = lambda last: pl.BlockSpec((nb, None, cap, last), lambda i, j, k: (j, i, 0, 0))
    return pl.pallas_call(
        _expert_kernel,
        grid=(e, bsz // nb, f // fc),
        in_specs=[tok(d), tok(1),
                  pl.BlockSpec((None, d, fc), lambda i, j, k: (i, 0, k)),
                  pl.BlockSpec((None, d, fc), lambda i, j, k: (i, 0, k)),
                  pl.BlockSpec((None, fc, d), lambda i, j, k: (i, k, 0))],
        out_specs=tok(d),
        out_shape=jax.ShapeDtypeStruct((bsz, e, cap, d), BF16),
        scratch_shapes=[pltpu.VMEM((nb, cap, d), F32)],
        compiler_params=_cparams("parallel", "parallel", "arbitrary"),
        name="expert",
    )(xg, gs, wg, wu, wd)


def _combine_kernel(r0_ref, x_ref, slot_ref, g_ref, b_ref, y_ref, o_ref, *, tt, win, align, alpha):
    b, j = pl.program_id(0), pl.program_id(1)
    n_e, cap = y_ref.shape[0], y_ref.shape[1]
    n_sub = x_ref.shape[0] // tt
    for u in range(n_sub):
        rs = slice(u * tt, (u + 1) * tt)
        acc = alpha * x_ref[rs, :]
        slots = slot_ref[rs, :]
        for e in range(n_e):
            start = _window_start(r0_ref[b, e, j * n_sub + u], cap, win, align)
            rel = slots[:, e:e + 1] - start
            onehot = jnp.where(lax.broadcasted_iota(I32, (tt, win), 1) == rel, 1.0, 0.0).astype(BF16)
            acc = acc + _dot(onehot, y_ref[e, pl.ds(start, win), :])
        o_ref[rs, :] = _ln(acc, g_ref[...], b_ref[...])


def _combine(r0, x, slot_t, g, b, y, *, tt, n_sub, win, align, alpha):
    bsz, t, d = x.shape
    e, cap = y.shape[1], y.shape[2]
    assert win >= tt + align and win <= cap and (cap - win) % align == 0
    ts = tt * n_sub
    const = lambda i, j, r: (0, 0)
    return pl.pallas_call(
        functools.partial(_combine_kernel, tt=tt, win=win, align=align, alpha=alpha),
        grid_spec=pltpu.PrefetchScalarGridSpec(
            num_scalar_prefetch=1,
            grid=(bsz, t // ts),
            in_specs=[pl.BlockSpec((None, ts, d), lambda i, j, r: (i, j, 0)),
                      pl.BlockSpec((None, ts, e), lambda i, j, r: (i, j, 0)),
                      pl.BlockSpec((1, d), const), pl.BlockSpec((1, d), const),
                      pl.BlockSpec((None, e, cap, d), lambda i, j, r: (i, 0, 0, 0),
                                   pipeline_mode=pl.Buffered(1))],
            out_specs=pl.BlockSpec((None, ts, d), lambda i, j, r: (i, j, 0))),
        out_shape=jax.ShapeDtypeStruct((bsz, t, d), F32),
        compiler_params=_cparams("parallel", "parallel"),
        name="combine",
    )(r0, x, slot_t, g, b, y)


def _hyena_constants(length):
    n = 2 * length
    m = np.arange(n)
    pos = np.where(m < length, m, n - m).astype(np.float64)
    pos[length] = 0.0
    t = pos / max(length - 1, 1)
    f = np.linspace(1e-4, HY_BANDS - 1, HY_BANDS)
    ang = (2.0 * np.pi * pos / length)[:, None] * f[None, :]
    feat = np.concatenate([t[:, None], np.cos(ang), -np.sin(ang)], axis=-1)
    feat_t = np.zeros((LANES, n), np.float32)
    feat_t[:feat.shape[1]] = feat.T
    valid = np.ones((1, n), np.float32)
    valid[0, length] = 0.0
    return feat_t, t[None, :].astype(np.float32), valid


def _decay_rates(width):
    max_decay = math.log(HY_DECAY_TARGET) / HY_FAST_DECAY
    min_decay = math.log(HY_DECAY_TARGET) / HY_SLOW_DECAY
    return np.abs(np.linspace(min_decay, max_decay, width, dtype=np.float32)).astype(np.float32)


def _dft_constants(r1):
    n, a1h, a2, at = _dft_tables(r1, r1 // 2)
    _, a1f, _, _ = _dft_tables(r1, r1)
    bf = lambda x: jnp.asarray(x, F32).astype(BF16)
    w1_data = bf(_cplx_rhs(a1h, -1.0))
    w1_filt = bf(np.concatenate([np.cos(a1f), -np.sin(a1f)], axis=1))
    w2 = bf(_cplx_rhs(a2, -1.0))
    w3 = bf(_cplx_rhs(a2.T, 1.0))
    w4 = bf(_cplx_rhs(a1h.T, 1.0) / n)
    twr, twi = np.cos(at), -np.sin(at)
    f32 = lambda x: jnp.asarray(x, F32)
    return dict(w1_data=w1_data, w1_filt=w1_filt, w2=w2, w3=w3, w4=w4,
                twr=f32(twr), twi=f32(twi), twr_t=f32(twr.T), twi_t=f32(twi.T))


def _tile_plan(length, hy_width, d_ff):
    tt = LANES
    return dict(
        filt_ffn_tn=min(2048, 2 * length),
        filt_rows=min(128, hy_width),
        kfft_rows=min(32, hy_width),
        inproj_tm=min(512, length),
        conf_tl=min(1024, length),
        hyena_g=min(32, hy_width),
        mix_tm=min(1024, length), mix_sub=512,
        xattn_tm=min(1024, length), xattn_sub=1024,
        tt=tt,
        st=LANES,
        kc=min(10 * tt, length),
        align=2 * SUBLANES,
        expert_fc=min(512, d_ff), expert_nb=2,
        combine_sub=min(4, length // tt),
    )


def kernel(x, mem, ln_in_g, ln_in_b, w_in, b_gate, conf_dw_w, conf_dw_b, conf_ln_g, conf_ln_b, conf_w_out, hy_short_w, hy_short_b, hy_ffn_w1, hy_ffn_b1, hy_freq1, hy_ffn_w2, hy_ffn_b2, hy_freq2, hy_ffn_w3, hy_skip, hy_w_out, w_mix_out, ln_mix_g, ln_mix_b, xa_wq, xa_wk, xa_wv, xa_wo, ln_xa_g, ln_xa_b, moe_w_router, moe_w_gate, moe_w_up, moe_w_down, ln_moe_g, ln_moe_b):
    bsz, length, d = x.shape
    depth = w_in.shape[0]
    alpha = (2.0 * depth) ** 0.25
    cw = conf_dw_w.shape[2]
    hw = hy_skip.shape[2]
    n_glu, n_hy = 2 * cw, 3 * hw
    n_exp = moe_w_router.shape[2]
    cap = max(1, N_EXPERT_CAPACITY * length // n_exp)
    rh = length // LANES
    r1 = 2 * rh
    assert bsz % 2 == 0 and length % LANES == 0

    row = lambda v: v.reshape(1, -1).astype(F32)
    col = lambda v: v.reshape(-1, 1).astype(F32)

    feat_t, t_ext, valid = _hyena_constants(length)
    dft = _dft_constants(r1)
    rates = _decay_rates(hw)
    tri = jnp.asarray(np.triu(np.ones((LANES, LANES), np.float32))).astype(BF16)

    n_ord = hy_skip.shape[1]
    tp = _tile_plan(length, hw, moe_w_gate.shape[3])
    lane_b = lambda v: jnp.broadcast_to(v[..., None, None], v.shape + (1, LANES)).astype(F32)
    dft_data = (dft["w1_data"], dft["twr"], dft["twi"], dft["w2"], dft["w3"], dft["twr_t"], dft["twi_t"],
                dft["w4"])

    h = x
    for i in range(depth):
        w1h, w1l = _split_bf16(jnp.pad(hy_ffn_w1[i].T, ((0, 0), (0, LANES - hy_ffn_w1.shape[1]))))
        w2h, w2l = _split_bf16(hy_ffn_w2[i].T)
        h2t = _filt_ffn(jnp.asarray(feat_t), w1h, w1l, col(hy_ffn_b1[i]), col(hy_freq1[i]),
                        w2h, w2l, col(hy_ffn_b2[i]), col(hy_freq2[i]), tn=tp["filt_ffn_tn"])
        w3 = hy_ffn_w3[i].reshape(-1, n_ord, 2, hw)
        w3f = jnp.transpose(w3[:, :, 0], (1, 2, 0)).reshape(n_ord * hw, -1)
        w3b = jnp.transpose(w3[:, :, 1], (1, 2, 0)).reshape(n_ord * hw, -1)
        w3fh, w3fl = _split_bf16(w3f)
        w3bh, w3bl = _split_bf16(w3b)
        delta = jnp.asarray(np.tile(rates, n_ord).reshape(-1, 1))
        kern = _filt(h2t, w3fh, w3fl, w3bh, w3bl, delta, jnp.asarray(t_ext), jnp.asarray(valid),
                     g=tp["filt_rows"])
        kf = _kfft(kern.reshape(n_ord * hw, r1, LANES), dft["w1_filt"], dft["twr"], dft["twi"], dft["w2"],
                   g=tp["kfft_rows"])

        wi = w_in[i]
        wglu = wi[:, :n_glu].astype(BF16)
        whyt = wi[:, n_glu:n_glu + n_hy].T.astype(BF16)
        wgate = wi[:, n_glu + n_hy:].astype(BF16)
        hn, u, hyt, gates = _inproj(h, row(ln_in_g), row(ln_in_b), wglu, whyt, wgate, row(b_gate[i]),
                                    tm=tp["inproj_tm"], apply_ln=(i == 0))

        ua = _conf(u, conf_dw_w[i], row(conf_dw_b[i]), row(conf_ln_g[i]), row(conf_ln_b[i]),
                   tl=tp["conf_tl"])

        z = _hyena(hyt.reshape(bsz, n_hy, rh, LANES), lane_b(hy_short_w[i]), lane_b(hy_short_b[i]), kf,
                   lane_b(hy_skip[i]), dft_data, g=tp["hyena_g"])

        x1 = _mix(ua, z, gates, hn, conf_w_out[i].astype(BF16),
                  hy_w_out[i].astype(BF16), w_mix_out[i].astype(BF16), row(ln_mix_g[i]), row(ln_mix_b[i]),
                  tm=tp["mix_tm"], sub=tp["mix_sub"], alpha=alpha)

        k, v = _kv(mem, xa_wk[i].astype(BF16), xa_wv[i].astype(BF16))
        wrh, wrl = _split_bf16(moe_w_router[i].T)
        x2, x2b, aff = _xattn(x1, k, v, xa_wq[i].astype(BF16), xa_wo[i].astype(BF16), row(ln_xa_g[i]),
                              row(ln_xa_b[i]), wrh, wrl, tm=tp["xattn_tm"], sub=tp["xattn_sub"], alpha=alpha,
                              heads=XA_HEADS)

        tt, st, kc, align = tp["tt"], tp["st"], tp["kc"], tp["align"]
        slot, slot_t, gate_hl, r0, jlo, nch = _select(aff, tri, cap=cap, st=st, kct=kc // tt)
        xg, gslot = _dispatch(jlo, nch, x2b, slot.reshape(bsz, n_exp, 1, length), gate_hl,
                              cap=cap, st=st, kc=kc, tt=tt)
        y = _expert(xg, gslot, moe_w_gate[i], moe_w_up[i], moe_w_down[i],
                    fc=tp["expert_fc"], nb=tp["expert_nb"])
        h = _combine(r0, x2, slot_t, row(ln_moe_g[i]), row(ln_moe_b[i]), y,
                     tt=tt, n_sub=tp["combine_sub"], win=tt + align, align=align, alpha=alpha)
    return h
```

```python
import functools
import math

import numpy as np
import jax
import jax.numpy as jnp
from jax import lax
from jax.experimental import pallas as pl
from jax.experimental.pallas import tpu as pltpu

F32 = jnp.float32
BF16 = jnp.bfloat16
I32 = jnp.int32

LANES = 128
SUBLANES = 8
VMEM_LIMIT_BYTES = 56 * 1024 * 1024
SUB_ROWS = 256

LN_EPS = 1e-5
XA_HEADS = 4
N_EXPERT_CAPACITY = 2
HY_DECAY_TARGET = 1e-2
HY_FAST_DECAY = 0.3
HY_SLOW_DECAY = 1.5
HY_BANDS = 16


def _cparams(*sem):
    return pltpu.CompilerParams(dimension_semantics=sem, vmem_limit_bytes=VMEM_LIMIT_BYTES)


def _ln(x, g, b):
    mu = jnp.mean(x, axis=-1, keepdims=True)
    xc = x - mu
    var = jnp.mean(xc * xc, axis=-1, keepdims=True)
    return xc * lax.rsqrt(var + LN_EPS) * g + b


def _sigmoid(x):
    return 1.0 / (1.0 + jnp.exp(-x))


def _dot(a, b):
    return jnp.dot(a, b, preferred_element_type=F32)


def _dot_nt(a, b):
    return lax.dot_general(a, b, (((1,), (1,)), ((), ())), preferred_element_type=F32)


def _split_bf16(x):
    hi = x.astype(BF16)
    lo = (x - hi.astype(F32)).astype(BF16)
    return hi, lo


def _dot3(a, b_hi, b_lo):
    a_hi, a_lo = _split_bf16(a)
    return _dot(a_hi, b_hi) + (_dot(a_hi, b_lo) + _dot(a_lo, b_hi))


def _dot3_lhs(a_hi, a_lo, b):
    b_hi, b_lo = _split_bf16(b)
    return _dot(a_hi, b_hi) + (_dot(a_hi, b_lo) + _dot(a_lo, b_hi))


def _inproj_kernel(x_ref, g_ref, b_ref, wglu_ref, whyt_ref, wgate_ref, bgate_ref,
                   h_ref, u_ref, hyt_ref, gate_ref, *, apply_ln, sub):
    for r in range(0, x_ref.shape[0], sub):
        rs = slice(r, r + sub)
        h = _ln(x_ref[rs, :], g_ref[...], b_ref[...]) if apply_ln else x_ref[rs, :]
        h_ref[rs, :] = h
        hb = h.astype(BF16)
        glu = _dot(hb, wglu_ref[...])
        cw = glu.shape[1] // 2
        u_ref[rs, :] = glu[:, :cw] * _sigmoid(glu[:, cw:])
        hyt_ref[:, rs] = _dot_nt(whyt_ref[...], hb).astype(hyt_ref.dtype)
        gl = _dot(hb, wgate_ref[...]) + bgate_ref[...]
        gate_ref[rs, :] = _sigmoid(gl).astype(BF16)


def _inproj(x, g, b, wglu, whyt, wgate, bgate, *, tm, apply_ln):
    bsz, length, d = x.shape
    n_glu, n_hy, n_gate = wglu.shape[1], whyt.shape[0], wgate.shape[1]
    const = lambda i, j: (0, 0)
    return pl.pallas_call(
        functools.partial(_inproj_kernel, apply_ln=apply_ln, sub=min(SUB_ROWS, tm)),
        grid=(bsz, length // tm),
        in_specs=[
            pl.BlockSpec((None, tm, d), lambda i, j: (i, j, 0)),
            pl.BlockSpec((1, d), const), pl.BlockSpec((1, d), const),
            pl.BlockSpec((d, n_glu), const),
            pl.BlockSpec((n_hy, d), const),
            pl.BlockSpec((d, n_gate), const),
            pl.BlockSpec((1, n_gate), const),
        ],
        out_specs=[
            pl.BlockSpec((None, tm, d), lambda i, j: (i, j, 0)),
            pl.BlockSpec((None, tm, n_glu // 2), lambda i, j: (i, j, 0)),
            pl.BlockSpec((None, n_hy, tm), lambda i, j: (i, 0, j)),
            pl.BlockSpec((None, tm, n_gate), lambda i, j: (i, j, 0)),
        ],
        out_shape=[
            jax.ShapeDtypeStruct((bsz, length, d), F32),
            jax.ShapeDtypeStruct((bsz, length, n_glu // 2), F32),
            jax.ShapeDtypeStruct((bsz, n_hy, length), BF16),
            jax.ShapeDtypeStruct((bsz, length, n_gate), BF16),
        ],
        compiler_params=_cparams("parallel", "parallel"),
        name="inproj",
    )(x, g, b, wglu, whyt, wgate, bgate)


def _conf_kernel(prev_ref, cur_ref, next_ref, w_ref, cb_ref, g_ref, b_ref, o_ref, ext_ref, sh_ref,
                 *, tl, halo, ksize, rows):
    j = pl.program_id(1)
    nj = pl.num_programs(1)
    pad = (ksize - 1) // 2
    ext_ref[0:halo, :] = jnp.where(j > 0, prev_ref[...], 0.0)
    ext_ref[halo:halo + tl, :] = cur_ref[...]
    ext_ref[halo + tl:halo + tl + halo, :] = jnp.where(j < nj - 1, next_ref[...], 0.0)
    n_sh = sh_ref.shape[1]
    for r in range(1, SUBLANES):
        sh_ref[r - 1, :, :] = ext_ref[r:r + n_sh, :]
    w = w_ref[...]
    cb, g, b = cb_ref[...], g_ref[...], b_ref[...]
    for r0 in range(0, tl, rows):
        acc = jnp.zeros((rows, w.shape[1]), F32)
        for k in range(ksize):
            q, r = divmod(halo - pad + k + r0, SUBLANES)
            q *= SUBLANES
            src = ext_ref[q:q + rows, :] if r == 0 else sh_ref[r - 1, q:q + rows, :]
            acc = acc + w[k:k + 1, :] * src
        y = _ln(acc + cb, g, b)
        o_ref[r0:r0 + rows, :] = (y * _sigmoid(y)).astype(o_ref.dtype)


def _conf(u, w, cb, g, b, *, tl, halo=16, rows=64):
    bsz, length, c = u.shape
    ksize = w.shape[0]
    assert (ksize - 1) // 2 <= halo and tl % halo == 0 and tl % rows == 0
    nh = tl // halo
    last = length // halo - 1
    const = lambda i, j: (0, 0)
    return pl.pallas_call(
        functools.partial(_conf_kernel, tl=tl, halo=halo, ksize=ksize, rows=rows),
        grid=(bsz, length // tl),
        in_specs=[
            pl.BlockSpec((None, halo, c), lambda i, j: (i, jnp.maximum(j * nh - 1, 0), 0)),
            pl.BlockSpec((None, tl, c), lambda i, j: (i, j, 0)),
            pl.BlockSpec((None, halo, c), lambda i, j: (i, jnp.minimum((j + 1) * nh, last), 0)),
            pl.BlockSpec((ksize, c), const),
            pl.BlockSpec((1, c), const), pl.BlockSpec((1, c), const), pl.BlockSpec((1, c), const),
        ],
        out_specs=pl.BlockSpec((None, tl, c), lambda i, j: (i, j, 0)),
        out_shape=jax.ShapeDtypeStruct((bsz, length, c), BF16),
        scratch_shapes=[pltpu.VMEM((tl + 2 * halo, c), F32),
                        pltpu.VMEM((SUBLANES - 1, tl + 2 * halo - SUBLANES, c), F32)],
        compiler_params=_cparams("parallel", "parallel"),
        name="conf",
    )(u, u, u, w, cb, g, b)


def _filt_ffn_kernel(feat_ref, w1h_ref, w1l_ref, b1_ref, f1_ref, w2h_ref, w2l_ref, b2_ref, f2_ref, o_ref):
    a = _dot3_lhs(w1h_ref[...], w1l_ref[...], feat_ref[...]) + b1_ref[...]
    h = jnp.sin(f1_ref[...] * a)
    a2 = _dot3_lhs(w2h_ref[...], w2l_ref[...], h) + b2_ref[...]
    o_ref[...] = jnp.sin(f2_ref[...] * a2)


def _filt_ffn(feat_t, w1h, w1l, b1, f1, w2h, w2l, b2, f2, *, tn):
    kp, n = feat_t.shape
    m = w1h.shape[0]
    const = lambda i: (0, 0)
    return pl.pallas_call(
        _filt_ffn_kernel,
        grid=(n // tn,),
        in_specs=[pl.BlockSpec((kp, tn), lambda i: (0, i)),
                  pl.BlockSpec((m, kp), const), pl.BlockSpec((m, kp), const),
                  pl.BlockSpec((m, 1), const), pl.BlockSpec((m, 1), const),
                  pl.BlockSpec((m, m), const), pl.BlockSpec((m, m), const),
                  pl.BlockSpec((m, 1), const), pl.BlockSpec((m, 1), const)],
        out_specs=pl.BlockSpec((m, tn), lambda i: (0, i)),
        out_shape=jax.ShapeDtypeStruct((m, n), F32),
        compiler_params=_cparams("parallel"),
        name="filt_ffn",
    )(feat_t, w1h, w1l, b1, f1, w2h, w2l, b2, f2)


def _filt_kernel(h2_ref, w3fh_ref, w3fl_ref, w3bh_ref, w3bl_ref, delta_ref, text_ref, valid_ref, o_ref):
    n = h2_ref.shape[1]
    half = n // 2
    decay = jnp.exp(-text_ref[...] * delta_ref[...]) * valid_ref[...]
    kf = _dot3_lhs(w3fh_ref[...], w3fl_ref[...], h2_ref[:, 0:half]) * decay[:, 0:half]
    kb = _dot3_lhs(w3bh_ref[...], w3bl_ref[...], h2_ref[:, half:n]) * decay[:, half:n]
    ss = jnp.sum(kf * kf, axis=1, keepdims=True) + jnp.sum(kb * kb, axis=1, keepdims=True)
    scale = lax.rsqrt(ss + 1e-6)
    o_ref[:, 0:half] = kf * scale
    o_ref[:, half:n] = kb * scale


def _filt(h2t, w3fh, w3fl, w3bh, w3bl, delta, text, valid, *, g):
    m, n = h2t.shape
    rows = w3fh.shape[0]
    const = lambda i: (0, 0)
    return pl.pallas_call(
        _filt_kernel,
        grid=(rows // g,),
        in_specs=[pl.BlockSpec((m, n), const),
                  pl.BlockSpec((g, m), lambda i: (i, 0)), pl.BlockSpec((g, m), lambda i: (i, 0)),
                  pl.BlockSpec((g, m), lambda i: (i, 0)), pl.BlockSpec((g, m), lambda i: (i, 0)),
                  pl.BlockSpec((g, 1), lambda i: (i, 0)),
                  pl.BlockSpec((1, n), const), pl.BlockSpec((1, n), const)],
        out_specs=pl.BlockSpec((g, n), lambda i: (i, 0)),
        out_shape=jax.ShapeDtypeStruct((rows, n), F32),
        compiler_params=_cparams("parallel"),
        name="filt",
    )(h2t, w3fh, w3fl, w3bh, w3bl, delta, text, valid)


def _dft_tables(r1, n1_used):
    n = r1 * LANES
    a1 = 2.0 * np.pi * np.outer(np.arange(n1_used), np.arange(r1)) / r1
    a2 = 2.0 * np.pi * np.outer(np.arange(LANES), np.arange(LANES)) / LANES
    at = 2.0 * np.pi * np.outer(np.arange(LANES), np.arange(r1)) / n
    return n, a1, a2, at


def _cplx_rhs(ang, sign):
    c, s = np.cos(ang), sign * np.sin(ang)
    return np.block([[c, s], [-s, c]])


def _fwd_steps(x_t, w1, twr, twi, w2):
    g, n2, k1w = x_t.shape
    r1 = twr.shape[1]
    a = _dot(x_t.reshape(g * n2, k1w).astype(BF16), w1).reshape(g, n2, 2 * r1)
    ar, ai = a[:, :, :r1], a[:, :, r1:]
    br = ar * twr - ai * twi
    bi = ar * twi + ai * twr
    bt = jnp.concatenate([jnp.swapaxes(br.astype(BF16), 1, 2), jnp.swapaxes(bi.astype(BF16), 1, 2)],
                         axis=-1)
    c = _dot(bt.reshape(g * r1, 2 * n2), w2)
    return c.reshape(g, r1, 2 * n2)


def _inv_steps(d, w3, twr_t, twi_t, w4):
    g, r1, w = d.shape
    n2 = w // 2
    e = _dot(d.reshape(g * r1, w).astype(BF16), w3).reshape(g, r1, w)
    er, ei = e[:, :, :n2], e[:, :, n2:]
    fr = er * twr_t + ei * twi_t
    fi = ei * twr_t - er * twi_t
    ft = jnp.concatenate([jnp.swapaxes(fr.astype(BF16), 1, 2), jnp.swapaxes(fi.astype(BF16), 1, 2)],
                         axis=-1)
    y = _dot(ft.reshape(g * n2, 2 * r1), w4)
    return y.reshape(g, n2, w4.shape[1])


def _kfft_kernel(k_ref, w1_ref, twr_ref, twi_ref, w2_ref, o_ref):
    x_t = jnp.swapaxes(k_ref[...].astype(BF16), 1, 2)
    o_ref[...] = _fwd_steps(x_t, w1_ref[...], twr_ref[...], twi_ref[...], w2_ref[...])


def _kfft(k3, w1, twr, twi, w2, *, g):
    rows, r1, lanes = k3.shape
    const = lambda i: (0, 0)
    return pl.pallas_call(
        _kfft_kernel,
        grid=(rows // g,),
        in_specs=[pl.BlockSpec((g, r1, lanes), lambda i: (i, 0, 0)),
                  pl.BlockSpec(w1.shape, const), pl.BlockSpec(twr.shape, const),
                  pl.BlockSpec(twi.shape, const), pl.BlockSpec(w2.shape, const)],
        out_specs=pl.BlockSpec((g, r1, 2 * lanes), lambda i: (i, 0, 0)),
        out_shape=jax.ShapeDtypeStruct((rows, r1, 2 * lanes), F32),
        compiler_params=_cparams("parallel"),
        name="kfft",
    )(k3, w1, twr, twi, w2)


def _short_conv(a, w, b):
    p, g, r, l = a.shape
    a3 = a.reshape(p * g, r, l)
    lane = lax.broadcasted_iota(I32, a3.shape, 2)
    row = lax.broadcasted_iota(I32, a3.shape, 1)
    pl_ = pltpu.roll(a3, 1, 2)
    ql = pltpu.roll(pl_, 1, 1)
    prev = jnp.where(lane == 0, jnp.where(row == 0, 0.0, ql), pl_)
    pr = pltpu.roll(a3, l - 1, 2)
    qr = pltpu.roll(pr, r - 1, 1)
    nxt = jnp.where(lane == l - 1, jnp.where(row == r - 1, 0.0, qr), pr)
    prev, nxt = prev.reshape(a.shape), nxt.reshape(a.shape)
    return w[0][None] * prev + w[1][None] * a + w[2][None] * nxt + b[None]


def _hyena_kernel(x1_ref, x2_ref, v_ref, sw1_ref, sb1_ref, sw2_ref, sb2_ref, swv_ref, sbv_ref,
                  kf0_ref, kf1_ref, skip_ref, w1_ref, twr_ref, twi_ref, w2_ref, w3_ref,
                  twrt_ref, twit_ref, w4_ref, z_ref):
    x1 = _short_conv(x1_ref[...].astype(F32), sw1_ref[...], sb1_ref[...])
    x2 = _short_conv(x2_ref[...].astype(F32), sw2_ref[...], sb2_ref[...])
    z = _short_conv(v_ref[...].astype(F32), swv_ref[...], sbv_ref[...])
    rh = z.shape[2]
    for gate, kf_ref, o in ((x1, kf0_ref, 0), (x2, kf1_ref, 1)):
        s = jnp.concatenate([z[0], z[1]], axis=1)
        c = _fwd_steps(jnp.swapaxes(s, 1, 2), w1_ref[...], twr_ref[...], twi_ref[...], w2_ref[...])
        kf = kf_ref[...]
        n2 = c.shape[2] // 2
        cr, ci, kr, ki = c[:, :, :n2], c[:, :, n2:], kf[:, :, :n2], kf[:, :, n2:]
        d = jnp.concatenate([cr * kr - ci * ki, cr * ki + ci * kr], axis=-1)
        y = _inv_steps(d, w3_ref[...], twrt_ref[...], twit_ref[...], w4_ref[...])
        y = jnp.swapaxes(y, 1, 2)
        conv = jnp.stack([y[:, :rh], y[:, rh:]], axis=0)
        z = gate * (conv + skip_ref[o][None] * z)
    z_ref[...] = z


def _hyena(hy4, sw, sb, kf, skip, tabs, *, g):
    bsz, c3, rh, lanes = hy4.shape
    c = c3 // 3
    r1 = 2 * rh
    ncb = c // g
    w1, twr, twi, w2, w3, twrt, twit, w4 = tabs
    const2 = lambda i, p: (0, 0)
    data = lambda off: pl.BlockSpec((2, g, rh, lanes), lambda i, p, off=off: (p, off + i, 0, 0))
    wspec = lambda off: pl.BlockSpec((3, g, 1, lanes), lambda i, p, off=off: (0, off + i, 0, 0))
    bspec = lambda off: pl.BlockSpec((g, 1, lanes), lambda i, p, off=off: (off + i, 0, 0))
    kspec = lambda off: pl.BlockSpec((g, r1, 2 * lanes), lambda i, p, off=off: (off + i, 0, 0))
    return pl.pallas_call(
        _hyena_kernel,
        grid=(ncb, bsz // 2),
        in_specs=[data(0), data(ncb), data(2 * ncb),
                  wspec(0), bspec(0), wspec(ncb), bspec(ncb), wspec(2 * ncb), bspec(2 * ncb),
                  kspec(0), kspec(ncb),
                  pl.BlockSpec((2, g, 1, lanes), lambda i, p: (0, i, 0, 0)),
                  pl.BlockSpec(w1.shape, const2), pl.BlockSpec(twr.shape, const2),
                  pl.BlockSpec(twi.shape, const2), pl.BlockSpec(w2.shape, const2),
                  pl.BlockSpec(w3.shape, const2), pl.BlockSpec(twrt.shape, const2),
                  pl.BlockSpec(twit.shape, const2), pl.BlockSpec(w4.shape, const2)],
        out_specs=pl.BlockSpec((2, g, rh, lanes), lambda i, p: (p, i, 0, 0)),
        out_shape=jax.ShapeDtypeStruct((bsz, c, rh, lanes), F32),
        compiler_params=_cparams("parallel", "parallel"),
        name="hyena",
    )(hy4, hy4, hy4, sw, sb, sw, sb, sw, sb, kf, kf, skip, w1, twr, twi, w2, w3, twrt, twit, w4)


def _mix_kernel(ua_ref, z_ref, gate_ref, h_ref, wa_ref, wb_ref, wm_ref, g_ref, b_ref, o_ref, *, alpha, sub):
    for r in range(0, h_ref.shape[0], sub):
        rs = slice(r, r + sub)
        ya = _dot(ua_ref[rs, :], wa_ref[...])
        zc = jnp.concatenate([z_ref[:, k, :] for k in range(r // LANES, (r + sub) // LANES)], axis=-1)
        yb = _dot(jnp.transpose(zc).astype(BF16), wb_ref[...])
        gt = gate_ref[rs, :].astype(F32)
        d = ya.shape[1]
        m = gt[:, :d] * ya + gt[:, d:] * yb
        mixed = _dot(m.astype(BF16), wm_ref[...])
        o_ref[rs, :] = _ln(alpha * h_ref[rs, :] + mixed, g_ref[...], b_ref[...])


def _mix(ua, z4, gates, h, wa, wb, wm, g, b, *, tm, sub, alpha):
    bsz, length, d = h.shape
    c = ua.shape[2]
    sub = min(sub, tm)
    assert sub % LANES == 0 and ((tm // LANES) % SUBLANES == 0 or tm == length)
    const = lambda i, j: (0, 0)
    return pl.pallas_call(
        functools.partial(_mix_kernel, alpha=alpha, sub=sub),
        grid=(bsz, length // tm),
        in_specs=[pl.BlockSpec((None, tm, c), lambda i, j: (i, j, 0)),
                  pl.BlockSpec((None, c, tm // LANES, LANES), lambda i, j: (i, 0, j, 0)),
                  pl.BlockSpec((None, tm, 2 * d), lambda i, j: (i, j, 0)),
                  pl.BlockSpec((None, tm, d), lambda i, j: (i, j, 0)),
                  pl.BlockSpec((c, d), const), pl.BlockSpec((c, d), const), pl.BlockSpec((d, d), const),
                  pl.BlockSpec((1, d), const), pl.BlockSpec((1, d), const)],
        out_specs=pl.BlockSpec((None, tm, d), lambda i, j: (i, j, 0)),
        out_shape=jax.ShapeDtypeStruct((bsz, length, d), F32),
        compiler_params=_cparams("parallel", "parallel"),
        name="mix",
    )(ua, z4, gates, h, wa, wb, wm, g, b)


def _kv_kernel(mem_ref, wk_ref, wv_ref, k_ref, v_ref):
    mb = mem_ref[...].astype(BF16)
    k_ref[...] = _dot(mb, wk_ref[...]).astype(BF16)
    v_ref[...] = _dot(mb, wv_ref[...]).astype(BF16)


def _kv(mem, wk, wv):
    bsz, m, d = mem.shape
    const = lambda i: (0, 0)
    blk = pl.BlockSpec((None, m, d), lambda i: (i, 0, 0))
    return pl.pallas_call(
        _kv_kernel,
        grid=(bsz,),
        in_specs=[blk, pl.BlockSpec((d, d), const), pl.BlockSpec((d, d), const)],
        out_specs=[blk, blk],
        out_shape=[jax.ShapeDtypeStruct((bsz, m, d), BF16)] * 2,
        compiler_params=_cparams("parallel"),
        name="kv",
    )(mem, wk, wv)


def _xattn_kernel(x_ref, k_ref, v_ref, wq_ref, wo_ref, g_ref, b_ref, wrh_ref, wrl_ref,
                  x2_ref, x2b_ref, aff_ref, *, alpha, heads, sub):
    d = x_ref.shape[1]
    dh = d // heads
    wrh, wrl = wrh_ref[...], wrl_ref[...]
    for r in range(0, x_ref.shape[0], sub):
        rs = slice(r, r + sub)
        x = x_ref[rs, :]
        q = (_dot(x.astype(BF16), wq_ref[...]) * (dh ** -0.5)).astype(BF16)
        outs = []
        for hd in range(heads):
            sl = slice(hd * dh, (hd + 1) * dh)
            s = _dot_nt(q[:, sl], k_ref[:, sl])
            s = s - jnp.max(s, axis=-1, keepdims=True)
            p = jnp.exp(s)
            p = p / jnp.sum(p, axis=-1, keepdims=True)
            outs.append(_dot(p.astype(BF16), v_ref[:, sl]))
        o = jnp.concatenate(outs, axis=-1)
        xa = _dot(o.astype(BF16), wo_ref[...])
        x2 = _ln(alpha * x + xa, g_ref[...], b_ref[...])
        x2_ref[rs, :] = x2
        x2h, x2l = _split_bf16(x2)
        x2b_ref[rs, :] = x2h
        n_e = wrh.shape[0]
        both = _dot_nt(jnp.concatenate([wrh, wrl], axis=0), x2h)
        logits = both[:n_e] + (_dot_nt(wrh, x2l) + both[n_e:])
        logits = logits - jnp.max(logits, axis=0, keepdims=True)
        ex = jnp.exp(logits)
        aff_ref[:, rs] = ex / jnp.sum(ex, axis=0, keepdims=True)


def _xattn(x, k, v, wq, wo, g, b, wrh, wrl, *, tm, sub, alpha, heads):
    bsz, length, d = x.shape
    m = k.shape[1]
    e = wrh.shape[0]
    const = lambda i, j: (0, 0)
    tok = pl.BlockSpec((None, tm, d), lambda i, j: (i, j, 0))
    mem = pl.BlockSpec((None, m, d), lambda i, j: (i, 0, 0))
    return pl.pallas_call(
        functools.partial(_xattn_kernel, alpha=alpha, heads=heads, sub=min(sub, tm)),
        grid=(bsz, length // tm),
        in_specs=[tok, mem, mem, pl.BlockSpec((d, d), const), pl.BlockSpec((d, d), const),
                  pl.BlockSpec((1, d), const), pl.BlockSpec((1, d), const),
                  pl.BlockSpec((e, d), const), pl.BlockSpec((e, d), const)],
        out_specs=[tok, tok, pl.BlockSpec((None, e, tm), lambda i, j: (i, 0, j))],
        out_shape=[jax.ShapeDtypeStruct((bsz, length, d), F32),
                   jax.ShapeDtypeStruct((bsz, length, d), BF16),
                   jax.ShapeDtypeStruct((bsz, e, length), F32)],
        compiler_params=_cparams("parallel", "parallel"),
        name="xattn",
    )(x, k, v, wq, wo, g, b, wrh, wrl)


def _excl_cumsum_lanes(mask_f32, tri, write):
    e, t = mask_f32.shape
    carry = jnp.zeros((e, 1), F32)
    for c in range(t // LANES):
        m = mask_f32[:, c * LANES:(c + 1) * LANES]
        inc = _dot(m.astype(BF16), tri)
        write(c, inc - m + carry, carry)
        carry = carry + inc[:, LANES - 1:LANES]


def _select_kernel(aff_ref, tri_ref, slot_ref, slott_ref, gatet_ref, r0_ref, jlo_ref, nch_ref, cnt_ref, cum_ref,
                   *, cap, st, kct):
    a = aff_ref[...]
    e = a.shape[0]
    v = jnp.zeros((e, 1), I32)
    for bit in range(30, -1, -1):
        cand = v | (1 << bit)
        cnt = jnp.sum(jnp.where(a >= pltpu.bitcast(cand, F32), 1.0, 0.0), axis=1, keepdims=True)
        v = jnp.where(cnt >= cap, cand, v)
    thr = pltpu.bitcast(v, F32)
    gt = a > thr
    eq = a == thr
    need = cap - jnp.sum(jnp.where(gt, 1.0, 0.0), axis=1, keepdims=True)
    tri = tri_ref[...]

    def write_eq(c, val, before):
        cnt_ref[:, c * LANES:(c + 1) * LANES] = val
    _excl_cumsum_lanes(jnp.where(eq, 1.0, 0.0), tri, write_eq)
    sel = gt | (eq & (cnt_ref[...] < need))
    gate = jnp.where(sel, a, 0.0)

    n_tiles = a.shape[1] // LANES

    def write_rank(c, val, before):
        cnt_ref[:, c * LANES:(c + 1) * LANES] = val
        cum_ref[:, c:c + 1] = before
    _excl_cumsum_lanes(jnp.where(sel, 1.0, 0.0), tri, write_rank)
    slot = jnp.where(sel, cnt_ref[...], -1.0)
    slot_ref[...] = slot.astype(I32)
    slott_ref[...] = jnp.transpose(slot).astype(I32)
    gatet_ref[...] = jnp.transpose(gate)

    before = cum_ref[:, 0:n_tiles]
    r0_ref[...] = before.astype(I32)
    after = jnp.concatenate([before[:, 1:], jnp.full((e, 1), cap, F32)], axis=1)
    for s in range(cap // st):
        lo = jnp.sum(jnp.where(after <= s * st, 1.0, 0.0), axis=1, keepdims=True)
        hi = jnp.sum(jnp.where(before < (s + 1) * st, 1.0, 0.0), axis=1, keepdims=True) - 1.0
        n = jnp.ones_like(lo)
        for m in range(1, -(-n_tiles // kct)):
            n = n + jnp.where(hi - lo >= m * kct, 1.0, 0.0)
        jlo_ref[:, s:s + 1] = lo.astype(I32)
        nch_ref[:, s:s + 1] = n.astype(I32)


def _select(aff, tri, *, cap, st, kct):
    bsz, e, t = aff.shape
    n_tiles, n_st = t // LANES, cap // st
    blk = pl.BlockSpec((None, e, t), lambda i: (i, 0, 0))
    small = lambda n: pl.BlockSpec((None, e, n), lambda i: (i, 0, 0))
    return pl.pallas_call(
        functools.partial(_select_kernel, cap=cap, st=st, kct=kct),
        grid=(bsz,),
        in_specs=[blk, pl.BlockSpec((LANES, LANES), lambda i: (0, 0))],
        out_specs=[blk, pl.BlockSpec((None, t, e), lambda i: (i, 0, 0)),
                   pl.BlockSpec((None, t, e), lambda i: (i, 0, 0)),
                   small(n_tiles), small(n_st), small(n_st)],
        out_shape=[jax.ShapeDtypeStruct((bsz, e, t), I32), jax.ShapeDtypeStruct((bsz, t, e), I32),
                   jax.ShapeDtypeStruct((bsz, t, e), F32),
                   jax.ShapeDtypeStruct((bsz, e, n_tiles), I32),
                   jax.ShapeDtypeStruct((bsz, e, n_st), I32), jax.ShapeDtypeStruct((bsz, e, n_st), I32)],
        scratch_shapes=[pltpu.VMEM((e, t), F32), pltpu.VMEM((e, LANES), F32)],
        compiler_params=_cparams("parallel"),
        name="select",
    )(aff, tri)


def _window_start(r0, cap, win, align):
    return pl.multiple_of(jnp.minimum((r0 // align) * align, cap - win), align)


def _dispatch_kernel(jlo_ref, nch_ref, x_ref, slot_ref, o_ref, acc_ref, *, st, kc, tt):
    b, e = pl.program_id(0), pl.program_id(1)
    t_total = x_ref.shape[0]
    n_st = o_ref.shape[0] // st

    def chunk(s, c):
        want = (jlo_ref[b, e, s] + c * (kc // tt)) * tt
        t0 = pl.multiple_of(jnp.minimum(want, t_total - kc), tt)
        tok = t0 + lax.broadcasted_iota(I32, (1, kc), 1)
        rel = jnp.where(tok >= want, slot_ref[:, pl.ds(t0, kc)] - s * st, -1)
        onehot = jnp.where(lax.broadcasted_iota(I32, (st, kc), 0) == rel, 1.0, 0.0).astype(BF16)
        return _dot(onehot, x_ref[pl.ds(t0, kc), :])

    for s in range(n_st):
        acc_ref[s] = chunk(s, 0)
    for s in range(n_st):
        def body(c, carry, s=s):
            acc_ref[s] += chunk(s, c)
            return carry
        lax.fori_loop(1, nch_ref[b, e, s], body, 0)
    for s in range(n_st):
        o_ref[s * st:(s + 1) * st, :] = acc_ref[s].astype(o_ref.dtype)


def _dispatch(jlo, nch, xb, slot4, *, cap, st, kc, tt):
    bsz, t, d = xb.shape
    e = slot4.shape[1]
    assert cap % st == 0 and kc % tt == 0 and t % tt == 0 and kc <= t
    return pl.pallas_call(
        functools.partial(_dispatch_kernel, st=st, kc=kc, tt=tt),
        grid_spec=pltpu.PrefetchScalarGridSpec(
            num_scalar_prefetch=2,
            grid=(bsz, e),
            in_specs=[pl.BlockSpec((None, t, d), lambda i, j, a, c: (i, 0, 0)),
                      pl.BlockSpec((None, None, 1, t), lambda i, j, a, c: (i, j, 0, 0))],
            out_specs=pl.BlockSpec((None, None, cap, d), lambda i, j, a, c: (i, j, 0, 0)),
            scratch_shapes=[pltpu.VMEM((cap // st, st, d), F32)]),
        out_shape=jax.ShapeDtypeStruct((bsz, e, cap, d), BF16),
        compiler_params=_cparams("parallel", "parallel"),
        name="dispatch",
    )(jlo, nch, xb, slot4)


def _expert_kernel(x_ref, wg_ref, wu_ref, wd_ref, o_ref, acc_ref):
    f = pl.program_id(2)
    wg = wg_ref[...].astype(BF16)
    wu = wu_ref[...].astype(BF16)
    wd = wd_ref[...].astype(BF16)
    @pl.when(f == 0)
    def _():
        acc_ref[...] = jnp.zeros_like(acc_ref)

    for bi in range(x_ref.shape[0]):
        x = x_ref[bi]
        hg = _dot(x, wg)
        hu = _dot(x, wu)
        act = (hg * _sigmoid(hg) * hu).astype(BF16)
        acc_ref[bi] += _dot(act, wd)

    @pl.when(f == pl.num_programs(2) - 1)
    def _():
        o_ref[...] = acc_ref[...].astype(o_ref.dtype)


def _expert(xg, wg, wu, wd, *, fc, nb):
    bsz, e, cap, d = xg.shape
    f = wg.shape[2]
    assert bsz % nb == 0 and f % fc == 0
    tok = pl.BlockSpec((nb, None, cap, d), lambda i, j, k: (j, i, 0, 0))
    return pl.pallas_call(
        _expert_kernel,
        grid=(e, bsz // nb, f // fc),
        in_specs=[tok,
                  pl.BlockSpec((None, d, fc), lambda i, j, k: (i, 0, k)),
                  pl.BlockSpec((None, d, fc), lambda i, j, k: (i, 0, k)),
                  pl.BlockSpec((None, fc, d), lambda i, j, k: (i, k, 0))],
        out_specs=tok,
        out_shape=jax.ShapeDtypeStruct((bsz, e, cap, d), BF16),
        scratch_shapes=[pltpu.VMEM((nb, cap, d), F32)],
        compiler_params=_cparams("parallel", "parallel", "arbitrary"),
        name="expert",
    )(xg, wg, wu, wd)


def _combine_kernel(r0_ref, x_ref, slot_ref, gate_ref, g_ref, b_ref, y_ref, o_ref, *, tt, win, align, alpha):
    b, j = pl.program_id(0), pl.program_id(1)
    n_e, cap = y_ref.shape[0], y_ref.shape[1]
    n_sub = x_ref.shape[0] // tt
    for u in range(n_sub):
        rs = slice(u * tt, (u + 1) * tt)
        acc = alpha * x_ref[rs, :]
        slots = slot_ref[rs, :]
        gates = gate_ref[rs, :]
        for e in range(n_e):
            start = _window_start(r0_ref[b, e, j * n_sub + u], cap, win, align)
            rel = slots[:, e:e + 1] - start
            pick = jnp.where(lax.broadcasted_iota(I32, (tt, win), 1) == rel, gates[:, e:e + 1], 0.0)
            acc = acc + _dot(pick.astype(BF16), y_ref[e, pl.ds(start, win), :])
        o_ref[rs, :] = _ln(acc, g_ref[...], b_ref[...])


def _combine(r0, x, slot_t, gate_t, g, b, y, *, tt, n_sub, win, align, alpha):
    bsz, t, d = x.shape
    e, cap = y.shape[1], y.shape[2]
    assert win >= tt + align and win <= cap and (cap - win) % align == 0
    ts = tt * n_sub
    const = lambda i, j, r: (0, 0)
    return pl.pallas_call(
        functools.partial(_combine_kernel, tt=tt, win=win, align=align, alpha=alpha),
        grid_spec=pltpu.PrefetchScalarGridSpec(
            num_scalar_prefetch=1,
            grid=(bsz, t // ts),
            in_specs=[pl.BlockSpec((None, ts, d), lambda i, j, r: (i, j, 0)),
                      pl.BlockSpec((None, ts, e), lambda i, j, r: (i, j, 0)),
                      pl.BlockSpec((None, ts, e), lambda i, j, r: (i, j, 0)),
                      pl.BlockSpec((1, d), const), pl.BlockSpec((1, d), const),
                      pl.BlockSpec((None, e, cap, d), lambda i, j, r: (i, 0, 0, 0),
                                   pipeline_mode=pl.Buffered(1))],
            out_specs=pl.BlockSpec((None, ts, d), lambda i, j, r: (i, j, 0))),
        out_shape=jax.ShapeDtypeStruct((bsz, t, d), F32),
        compiler_params=_cparams("parallel", "parallel"),
        name="combine",
    )(r0, x, slot_t, gate_t, g, b, y)


def _hyena_constants(length):
    n = 2 * length
    m = np.arange(n)
    pos = np.where(m < length, m, n - m).astype(np.float64)
    pos[length] = 0.0
    t = pos / max(length - 1, 1)
    f = np.linspace(1e-4, HY_BANDS - 1, HY_BANDS)
    ang = (2.0 * np.pi * pos / length)[:, None] * f[None, :]
    feat = np.concatenate([t[:, None], np.cos(ang), -np.sin(ang)], axis=-1)
    feat_t = np.zeros((LANES, n), np.float32)
    feat_t[:feat.shape[1]] = feat.T
    valid = np.ones((1, n), np.float32)
    valid[0, length] = 0.0
    return feat_t, t[None, :].astype(np.float32), valid


def _decay_rates(width):
    max_decay = math.log(HY_DECAY_TARGET) / HY_FAST_DECAY
    min_decay = math.log(HY_DECAY_TARGET) / HY_SLOW_DECAY
    return np.abs(np.linspace(min_decay, max_decay, width, dtype=np.float32)).astype(np.float32)


def _dft_constants(r1):
    n, a1h, a2, at = _dft_tables(r1, r1 // 2)
    _, a1f, _, _ = _dft_tables(r1, r1)
    bf = lambda x: jnp.asarray(x, F32).astype(BF16)
    w1_data = bf(_cplx_rhs(a1h, -1.0))
    w1_filt = bf(np.concatenate([np.cos(a1f), -np.sin(a1f)], axis=1))
    w2 = bf(_cplx_rhs(a2, -1.0))
    w3 = bf(_cplx_rhs(a2.T, 1.0))
    w4 = bf(_cplx_rhs(a1h.T, 1.0) / n)
    twr, twi = np.cos(at), -np.sin(at)
    f32 = lambda x: jnp.asarray(x, F32)
    return dict(w1_data=w1_data, w1_filt=w1_filt, w2=w2, w3=w3, w4=w4,
                twr=f32(twr), twi=f32(twi), twr_t=f32(twr.T), twi_t=f32(twi.T))


def _tile_plan(length, hy_width, d_ff):
    tt = LANES
    return dict(
        filt_ffn_tn=min(2048, 2 * length),
        filt_rows=min(128, hy_width),
        kfft_rows=min(32, hy_width),
        inproj_tm=min(512, length),
        conf_tl=min(1024, length),
        hyena_g=min(32, hy_width),
        mix_tm=min(1024, length), mix_sub=512,
        xattn_tm=min(1024, length), xattn_sub=1024,
        tt=tt,
        st=LANES,
        kc=min(10 * tt, length),
        align=2 * SUBLANES,
        expert_fc=min(512, d_ff), expert_nb=2,
        combine_sub=min(4, length // tt),
    )


def kernel(x, mem, ln_in_g, ln_in_b, w_in, b_gate, conf_dw_w, conf_dw_b, conf_ln_g, conf_ln_b, conf_w_out, hy_short_w, hy_short_b, hy_ffn_w1, hy_ffn_b1, hy_freq1, hy_ffn_w2, hy_ffn_b2, hy_freq2, hy_ffn_w3, hy_skip, hy_w_out, w_mix_out, ln_mix_g, ln_mix_b, xa_wq, xa_wk, xa_wv, xa_wo, ln_xa_g, ln_xa_b, moe_w_router, moe_w_gate, moe_w_up, moe_w_down, ln_moe_g, ln_moe_b):
    bsz, length, d = x.shape
    depth = w_in.shape[0]
    alpha = (2.0 * depth) ** 0.25
    cw = conf_dw_w.shape[2]
    hw = hy_skip.shape[2]
    n_glu, n_hy = 2 * cw, 3 * hw
    n_exp = moe_w_router.shape[2]
    cap = max(1, N_EXPERT_CAPACITY * length // n_exp)
    rh = length // LANES
    r1 = 2 * rh
    assert bsz % 2 == 0 and length % LANES == 0

    row = lambda v: v.reshape(1, -1).astype(F32)
    col = lambda v: v.reshape(-1, 1).astype(F32)

    feat_t, t_ext, valid = _hyena_constants(length)
    dft = _dft_constants(r1)
    rates = _decay_rates(hw)
    tri = jnp.asarray(np.triu(np.ones((LANES, LANES), np.float32))).astype(BF16)

    n_ord = hy_skip.shape[1]
    tp = _tile_plan(length, hw, moe_w_gate.shape[3])
    lane_b = lambda v: jnp.broadcast_to(v[..., None, None], v.shape + (1, LANES)).astype(F32)
    dft_data = (dft["w1_data"], dft["twr"], dft["twi"], dft["w2"], dft["w3"], dft["twr_t"], dft["twi_t"],
                dft["w4"])

    h = x
    for i in range(depth):
        w1h, w1l = _split_bf16(jnp.pad(hy_ffn_w1[i].T, ((0, 0), (0, LANES - hy_ffn_w1.shape[1]))))
        w2h, w2l = _split_bf16(hy_ffn_w2[i].T)
        h2t = _filt_ffn(jnp.asarray(feat_t), w1h, w1l, col(hy_ffn_b1[i]), col(hy_freq1[i]),
                        w2h, w2l, col(hy_ffn_b2[i]), col(hy_freq2[i]), tn=tp["filt_ffn_tn"])
        w3 = hy_ffn_w3[i].reshape(-1, n_ord, 2, hw)
        w3f = jnp.transpose(w3[:, :, 0], (1, 2, 0)).reshape(n_ord * hw, -1)
        w3b = jnp.transpose(w3[:, :, 1], (1, 2, 0)).reshape(n_ord * hw, -1)
        w3fh, w3fl = _split_bf16(w3f)
        w3bh, w3bl = _split_bf16(w3b)
        delta = jnp.asarray(np.tile(rates, n_ord).reshape(-1, 1))
        kern = _filt(h2t, w3fh, w3fl, w3bh, w3bl, delta, jnp.asarray(t_ext), jnp.asarray(valid),
                     g=tp["filt_rows"])
        kf = _kfft(kern.reshape(n_ord * hw, r1, LANES), dft["w1_filt"], dft["twr"], dft["twi"], dft["w2"],
                   g=tp["kfft_rows"])

        wi = w_in[i]
        wglu = wi[:, :n_glu].astype(BF16)
        whyt = wi[:, n_glu:n_glu + n_hy].T.astype(BF16)
        wgate = wi[:, n_glu + n_hy:].astype(BF16)
        hn, u, hyt, gates = _inproj(h, row(ln_in_g), row(ln_in_b), wglu, whyt, wgate, row(b_gate[i]),
                                    tm=tp["inproj_tm"], apply_ln=(i == 0))

        ua = _conf(u, conf_dw_w[i], row(conf_dw_b[i]), row(conf_ln_g[i]), row(conf_ln_b[i]),
                   tl=tp["conf_tl"])

        z = _hyena(hyt.reshape(bsz, n_hy, rh, LANES), lane_b(hy_short_w[i]), lane_b(hy_short_b[i]), kf,
                   lane_b(hy_skip[i]), dft_data, g=tp["hyena_g"])

        x1 = _mix(ua, z, gates, hn, conf_w_out[i].astype(BF16),
                  hy_w_out[i].astype(BF16), w_mix_out[i].astype(BF16), row(ln_mix_g[i]), row(ln_mix_b[i]),
                  tm=tp["mix_tm"], sub=tp["mix_sub"], alpha=alpha)

        k, v = _kv(mem, xa_wk[i].astype(BF16), xa_wv[i].astype(BF16))
        wrh, wrl = _split_bf16(moe_w_router[i].T)
        x2, x2b, aff = _xattn(x1, k, v, xa_wq[i].astype(BF16), xa_wo[i].astype(BF16), row(ln_xa_g[i]),
                              row(ln_xa_b[i]), wrh, wrl, tm=tp["xattn_tm"], sub=tp["xattn_sub"], alpha=alpha,
                              heads=XA_HEADS)

        tt, st, kc, align = tp["tt"], tp["st"], tp["kc"], tp["align"]
        slot, slot_t, gate_t, r0, jlo, nch = _select(aff, tri, cap=cap, st=st, kct=kc // tt)
        xg = _dispatch(jlo, nch, x2b, slot.reshape(bsz, n_exp, 1, length), cap=cap, st=st, kc=kc, tt=tt)
        y = _expert(xg, moe_w_gate[i], moe_w_up[i], moe_w_down[i], fc=tp["expert_fc"], nb=tp["expert_nb"])
        h = _combine(r0, x2, slot_t, gate_t, row(ln_moe_g[i]), row(ln_moe_b[i]), y,
                     tt=tt, n_sub=tp["combine_sub"], win=tt + align, align=align, alpha=alpha)
    return h
```

```python
import functools
import math

import numpy as np
import jax
import jax.numpy as jnp
from jax import lax
from jax.experimental import pallas as pl
from jax.experimental.pallas import tpu as pltpu

F32 = jnp.float32
BF16 = jnp.bfloat16
I32 = jnp.int32

LANES = 128
SUBLANES = 8
VMEM_LIMIT_BYTES = 56 * 1024 * 1024
SUB_ROWS = 256

LN_EPS = 1e-5
XA_HEADS = 4
N_EXPERT_CAPACITY = 2
HY_DECAY_TARGET = 1e-2
HY_FAST_DECAY = 0.3
HY_SLOW_DECAY = 1.5
HY_BANDS = 16


def _cparams(*sem):
    return pltpu.CompilerParams(dimension_semantics=sem, vmem_limit_bytes=VMEM_LIMIT_BYTES)


def _ln(x, g, b):
    mu = jnp.mean(x, axis=-1, keepdims=True)
    xc = x - mu
    var = jnp.mean(xc * xc, axis=-1, keepdims=True)
    return xc * lax.rsqrt(var + LN_EPS) * g + b


def _sigmoid(x):
    return 1.0 / (1.0 + jnp.exp(-x))


def _dot(a, b):
    return jnp.dot(a, b, preferred_element_type=F32)


def _dot_nt(a, b):
    return lax.dot_general(a, b, (((1,), (1,)), ((), ())), preferred_element_type=F32)


def _split_bf16(x):
    hi = x.astype(BF16)
    lo = (x - hi.astype(F32)).astype(BF16)
    return hi, lo


def _dot3_lhs(a_hi, a_lo, b):
    b_hi, b_lo = _split_bf16(b)
    return _dot(a_hi, b_hi) + (_dot(a_hi, b_lo) + _dot(a_lo, b_hi))


def _inproj_kernel(x_ref, g_ref, b_ref, wglu_ref, whyt_ref, wgate_ref, bgate_ref,
                   h_ref, u_ref, hyt_ref, gate_ref, *, apply_ln, sub):
    for r in range(0, x_ref.shape[0], sub):
        rs = slice(r, r + sub)
        h = _ln(x_ref[rs, :], g_ref[...], b_ref[...]) if apply_ln else x_ref[rs, :]
        h_ref[rs, :] = h
        hb = h.astype(BF16)
        glu = _dot(hb, wglu_ref[...])
        cw = glu.shape[1] // 2
        u_ref[rs, :] = glu[:, :cw] * _sigmoid(glu[:, cw:])
        hyt_ref[:, rs] = _dot_nt(whyt_ref[...], hb).astype(hyt_ref.dtype)
        gl = _dot(hb, wgate_ref[...]) + bgate_ref[...]
        gate_ref[rs, :] = _sigmoid(gl).astype(BF16)


def _inproj(x, g, b, wglu, whyt, wgate, bgate, *, tm, apply_ln):
    bsz, length, d = x.shape
    n_glu, n_hy, n_gate = wglu.shape[1], whyt.shape[0], wgate.shape[1]
    const = lambda i, j: (0, 0)
    return pl.pallas_call(
        functools.partial(_inproj_kernel, apply_ln=apply_ln, sub=min(SUB_ROWS, tm)),
        grid=(bsz, length // tm),
        in_specs=[
            pl.BlockSpec((None, tm, d), lambda i, j: (i, j, 0)),
            pl.BlockSpec((1, d), const), pl.BlockSpec((1, d), const),
            pl.BlockSpec((d, n_glu), const),
            pl.BlockSpec((n_hy, d), const),
            pl.BlockSpec((d, n_gate), const),
            pl.BlockSpec((1, n_gate), const),
        ],
        out_specs=[
            pl.BlockSpec((None, tm, d), lambda i, j: (i, j, 0)),
            pl.BlockSpec((None, tm, n_glu // 2), lambda i, j: (i, j, 0)),
            pl.BlockSpec((None, n_hy, tm), lambda i, j: (i, 0, j)),
            pl.BlockSpec((None, tm, n_gate), lambda i, j: (i, j, 0)),
        ],
        out_shape=[
            jax.ShapeDtypeStruct((bsz, length, d), F32),
            jax.ShapeDtypeStruct((bsz, length, n_glu // 2), F32),
            jax.ShapeDtypeStruct((bsz, n_hy, length), BF16),
            jax.ShapeDtypeStruct((bsz, length, n_gate), BF16),
        ],
        compiler_params=_cparams("parallel", "parallel"),
        name="inproj",
    )(x, g, b, wglu, whyt, wgate, bgate)


def _conf_kernel(prev_ref, cur_ref, next_ref, w_ref, cb_ref, g_ref, b_ref, o_ref, ext_ref, sh_ref,
                 *, tl, halo, ksize, rows):
    j = pl.program_id(1)
    nj = pl.num_programs(1)
    pad = (ksize - 1) // 2
    ext_ref[0:halo, :] = jnp.where(j > 0, prev_ref[...], 0.0)
    ext_ref[halo:halo + tl, :] = cur_ref[...]
    ext_ref[halo + tl:halo + tl + halo, :] = jnp.where(j < nj - 1, next_ref[...], 0.0)
    n_sh = sh_ref.shape[1]
    for r in range(1, SUBLANES):
        sh_ref[r - 1, :, :] = ext_ref[r:r + n_sh, :]
    w = w_ref[...]
    cb, g, b = cb_ref[...], g_ref[...], b_ref[...]
    for r0 in range(0, tl, rows):
        acc = jnp.zeros((rows, w.shape[1]), F32)
        for k in range(ksize):
            q, r = divmod(halo - pad + k + r0, SUBLANES)
            q *= SUBLANES
            src = ext_ref[q:q + rows, :] if r == 0 else sh_ref[r - 1, q:q + rows, :]
            acc = acc + w[k:k + 1, :] * src
        y = _ln(acc + cb, g, b)
        o_ref[r0:r0 + rows, :] = (y * _sigmoid(y)).astype(o_ref.dtype)


def _conf(u, w, cb, g, b, *, tl, halo=16, rows=64):
    bsz, length, c = u.shape
    ksize = w.shape[0]
    assert (ksize - 1) // 2 <= halo and tl % halo == 0 and tl % rows == 0
    nh = tl // halo
    last = length // halo - 1
    const = lambda i, j: (0, 0)
    return pl.pallas_call(
        functools.partial(_conf_kernel, tl=tl, halo=halo, ksize=ksize, rows=rows),
        grid=(bsz, length // tl),
        in_specs=[
            pl.BlockSpec((None, halo, c), lambda i, j: (i, jnp.maximum(j * nh - 1, 0), 0)),
            pl.BlockSpec((None, tl, c), lambda i, j: (i, j, 0)),
            pl.BlockSpec((None, halo, c), lambda i, j: (i, jnp.minimum((j + 1) * nh, last), 0)),
            pl.BlockSpec((ksize, c), const),
            pl.BlockSpec((1, c), const), pl.BlockSpec((1, c), const), pl.BlockSpec((1, c), const),
        ],
        out_specs=pl.BlockSpec((None, tl, c), lambda i, j: (i, j, 0)),
        out_shape=jax.ShapeDtypeStruct((bsz, length, c), BF16),
        scratch_shapes=[pltpu.VMEM((tl + 2 * halo, c), F32),
                        pltpu.VMEM((SUBLANES - 1, tl + 2 * halo - SUBLANES, c), F32)],
        compiler_params=_cparams("parallel", "parallel"),
        name="conf",
    )(u, u, u, w, cb, g, b)


def _filt_ffn_kernel(feat_ref, w1h_ref, w1l_ref, b1_ref, f1_ref, w2h_ref, w2l_ref, b2_ref, f2_ref, o_ref):
    a = _dot3_lhs(w1h_ref[...], w1l_ref[...], feat_ref[...]) + b1_ref[...]
    h = jnp.sin(f1_ref[...] * a)
    a2 = _dot3_lhs(w2h_ref[...], w2l_ref[...], h) + b2_ref[...]
    o_ref[...] = jnp.sin(f2_ref[...] * a2)


def _filt_ffn(feat_t, w1h, w1l, b1, f1, w2h, w2l, b2, f2, *, tn):
    kp, n = feat_t.shape
    m = w1h.shape[0]
    const = lambda i: (0, 0)
    return pl.pallas_call(
        _filt_ffn_kernel,
        grid=(n // tn,),
        in_specs=[pl.BlockSpec((kp, tn), lambda i: (0, i)),
                  pl.BlockSpec((m, kp), const), pl.BlockSpec((m, kp), const),
                  pl.BlockSpec((m, 1), const), pl.BlockSpec((m, 1), const),
                  pl.BlockSpec((m, m), const), pl.BlockSpec((m, m), const),
                  pl.BlockSpec((m, 1), const), pl.BlockSpec((m, 1), const)],
        out_specs=pl.BlockSpec((m, tn), lambda i: (0, i)),
        out_shape=jax.ShapeDtypeStruct((m, n), F32),
        compiler_params=_cparams("parallel"),
        name="filt_ffn",
    )(feat_t, w1h, w1l, b1, f1, w2h, w2l, b2, f2)


def _filt_kernel(h2_ref, w3fh_ref, w3fl_ref, w3bh_ref, w3bl_ref, delta_ref, text_ref, valid_ref, o_ref):
    n = h2_ref.shape[1]
    half = n // 2
    decay = jnp.exp(-text_ref[...] * delta_ref[...]) * valid_ref[...]
    kf = _dot3_lhs(w3fh_ref[...], w3fl_ref[...], h2_ref[:, 0:half]) * decay[:, 0:half]
    kb = _dot3_lhs(w3bh_ref[...], w3bl_ref[...], h2_ref[:, half:n]) * decay[:, half:n]
    ss = jnp.sum(kf * kf, axis=1, keepdims=True) + jnp.sum(kb * kb, axis=1, keepdims=True)
    scale = lax.rsqrt(ss + 1e-6)
    o_ref[:, 0:half] = kf * scale
    o_ref[:, half:n] = kb * scale


def _filt(h2t, w3fh, w3fl, w3bh, w3bl, delta, text, valid, *, g):
    m, n = h2t.shape
    rows = w3fh.shape[0]
    const = lambda i: (0, 0)
    return pl.pallas_call(
        _filt_kernel,
        grid=(rows // g,),
        in_specs=[pl.BlockSpec((m, n), const),
                  pl.BlockSpec((g, m), lambda i: (i, 0)), pl.BlockSpec((g, m), lambda i: (i, 0)),
                  pl.BlockSpec((g, m), lambda i: (i, 0)), pl.BlockSpec((g, m), lambda i: (i, 0)),
                  pl.BlockSpec((g, 1), lambda i: (i, 0)),
                  pl.BlockSpec((1, n), const), pl.BlockSpec((1, n), const)],
        out_specs=pl.BlockSpec((g, n), lambda i: (i, 0)),
        out_shape=jax.ShapeDtypeStruct((rows, n), F32),
        compiler_params=_cparams("parallel"),
        name="filt",
    )(h2t, w3fh, w3fl, w3bh, w3bl, delta, text, valid)


def _dft_tables(r1, n1_used):
    n = r1 * LANES
    a1 = 2.0 * np.pi * np.outer(np.arange(n1_used), np.arange(r1)) / r1
    a2 = 2.0 * np.pi * np.outer(np.arange(LANES), np.arange(LANES)) / LANES
    at = 2.0 * np.pi * np.outer(np.arange(LANES), np.arange(r1)) / n
    return n, a1, a2, at


def _cplx_rhs(ang, sign):
    c, s = np.cos(ang), sign * np.sin(ang)
    return np.block([[c, s], [-s, c]])


def _fwd_steps(x_t, w1, twr, twi, w2):
    g, n2, k1w = x_t.shape
    r1 = twr.shape[1]
    a = _dot(x_t.reshape(g * n2, k1w).astype(BF16), w1).reshape(g, n2, 2 * r1)
    ar, ai = a[:, :, :r1], a[:, :, r1:]
    br = ar * twr - ai * twi
    bi = ar * twi + ai * twr
    bt = jnp.concatenate([jnp.swapaxes(br.astype(BF16), 1, 2), jnp.swapaxes(bi.astype(BF16), 1, 2)],
                         axis=-1)
    c = _dot(bt.reshape(g * r1, 2 * n2), w2)
    return c.reshape(g, r1, 2 * n2)


def _inv_steps(d, w3, twr_t, twi_t, w4):
    g, r1, w = d.shape
    n2 = w // 2
    e = _dot(d.reshape(g * r1, w).astype(BF16), w3).reshape(g, r1, w)
    er, ei = e[:, :, :n2], e[:, :, n2:]
    fr = er * twr_t + ei * twi_t
    fi = ei * twr_t - er * twi_t
    ft = jnp.concatenate([jnp.swapaxes(fr.astype(BF16), 1, 2), jnp.swapaxes(fi.astype(BF16), 1, 2)],
                         axis=-1)
    y = _dot(ft.reshape(g * n2, 2 * r1), w4)
    return y.reshape(g, n2, w4.shape[1])


def _kfft_kernel(k_ref, w1_ref, twr_ref, twi_ref, w2_ref, o_ref):
    x_t = jnp.swapaxes(k_ref[...].astype(BF16), 1, 2)
    o_ref[...] = _fwd_steps(x_t, w1_ref[...], twr_ref[...], twi_ref[...], w2_ref[...])


def _kfft(k3, w1, twr, twi, w2, *, g):
    rows, r1, lanes = k3.shape
    const = lambda i: (0, 0)
    return pl.pallas_call(
        _kfft_kernel,
        grid=(rows // g,),
        in_specs=[pl.BlockSpec((g, r1, lanes), lambda i: (i, 0, 0)),
                  pl.BlockSpec(w1.shape, const), pl.BlockSpec(twr.shape, const),
                  pl.BlockSpec(twi.shape, const), pl.BlockSpec(w2.shape, const)],
        out_specs=pl.BlockSpec((g, r1, 2 * lanes), lambda i: (i, 0, 0)),
        out_shape=jax.ShapeDtypeStruct((rows, r1, 2 * lanes), F32),
        compiler_params=_cparams("parallel"),
        name="kfft",
    )(k3, w1, twr, twi, w2)


def _short_conv(a, w, b):
    p, g, r, l = a.shape
    a3 = a.reshape(p * g, r, l)
    lane = lax.broadcasted_iota(I32, a3.shape, 2)
    row = lax.broadcasted_iota(I32, a3.shape, 1)
    pl_ = pltpu.roll(a3, 1, 2)
    ql = pltpu.roll(pl_, 1, 1)
    prev = jnp.where(lane == 0, jnp.where(row == 0, 0.0, ql), pl_)
    pr = pltpu.roll(a3, l - 1, 2)
    qr = pltpu.roll(pr, r - 1, 1)
    nxt = jnp.where(lane == l - 1, jnp.where(row == r - 1, 0.0, qr), pr)
    prev, nxt = prev.reshape(a.shape), nxt.reshape(a.shape)
    return w[0][None] * prev + w[1][None] * a + w[2][None] * nxt + b[None]


def _hyena_kernel(x1_ref, x2_ref, v_ref, sw1_ref, sb1_ref, sw2_ref, sb2_ref, swv_ref, sbv_ref,
                  kf0_ref, kf1_ref, skip_ref, w1_ref, twr_ref, twi_ref, w2_ref, w3_ref,
                  twrt_ref, twit_ref, w4_ref, z_ref):
    x1 = _short_conv(x1_ref[...].astype(F32), sw1_ref[...], sb1_ref[...])
    x2 = _short_conv(x2_ref[...].astype(F32), sw2_ref[...], sb2_ref[...])
    z = _short_conv(v_ref[...].astype(F32), swv_ref[...], sbv_ref[...])
    rh = z.shape[2]
    for gate, kf_ref, o in ((x1, kf0_ref, 0), (x2, kf1_ref, 1)):
        s = jnp.concatenate([z[0], z[1]], axis=1)
        c = _fwd_steps(jnp.swapaxes(s, 1, 2), w1_ref[...], twr_ref[...], twi_ref[...], w2_ref[...])
        kf = kf_ref[...]
        n2 = c.shape[2] // 2
        cr, ci, kr, ki = c[:, :, :n2], c[:, :, n2:], kf[:, :, :n2], kf[:, :, n2:]
        d = jnp.concatenate([cr * kr - ci * ki, cr * ki + ci * kr], axis=-1)
        y = _inv_steps(d, w3_ref[...], twrt_ref[...], twit_ref[...], w4_ref[...])
        y = jnp.swapaxes(y, 1, 2)
        conv = jnp.stack([y[:, :rh], y[:, rh:]], axis=0)
        z = gate * (conv + skip_ref[o][None] * z)
    z_ref[...] = z


def _hyena(hy4, sw, sb, kf, skip, tabs, *, g):
    bsz, c3, rh, lanes = hy4.shape
    c = c3 // 3
    r1 = 2 * rh
    ncb = c // g
    w1, twr, twi, w2, w3, twrt, twit, w4 = tabs
    const2 = lambda i, p: (0, 0)
    data = lambda off: pl.BlockSpec((2, g, rh, lanes), lambda i, p, off=off: (p, off + i, 0, 0))
    wspec = lambda off: pl.BlockSpec((3, g, 1, lanes), lambda i, p, off=off: (0, off + i, 0, 0))
    bspec = lambda off: pl.BlockSpec((g, 1, lanes), lambda i, p, off=off: (off + i, 0, 0))
    kspec = lambda off: pl.BlockSpec((g, r1, 2 * lanes), lambda i, p, off=off: (off + i, 0, 0))
    return pl.pallas_call(
        _hyena_kernel,
        grid=(ncb, bsz // 2),
        in_specs=[data(0), data(ncb), data(2 * ncb),
                  wspec(0), bspec(0), wspec(ncb), bspec(ncb), wspec(2 * ncb), bspec(2 * ncb),
                  kspec(0), kspec(ncb),
                  pl.BlockSpec((2, g, 1, lanes), lambda i, p: (0, i, 0, 0)),
                  pl.BlockSpec(w1.shape, const2), pl.BlockSpec(twr.shape, const2),
                  pl.BlockSpec(twi.shape, const2), pl.BlockSpec(w2.shape, const2),
                  pl.BlockSpec(w3.shape, const2), pl.BlockSpec(twrt.shape, const2),
                  pl.BlockSpec(twit.shape, const2), pl.BlockSpec(w4.shape, const2)],
        out_specs=pl.BlockSpec((2, g, rh, lanes), lambda i, p: (p, i, 0, 0)),
        out_shape=jax.ShapeDtypeStruct((bsz, c, rh, lanes), F32),
        compiler_params=_cparams("parallel", "parallel"),
        name="hyena",
    )(hy4, hy4, hy4, sw, sb, sw, sb, sw, sb, kf, kf, skip, w1, twr, twi, w2, w3, twrt, twit, w4)


def _mix_kernel(ua_ref, z_ref, gate_ref, h_ref, wa_ref, wb_ref, wm_ref, g_ref, b_ref, o_ref, *, alpha, sub):
    for r in range(0, h_ref.shape[0], sub):
        rs = slice(r, r + sub)
        ya = _dot(ua_ref[rs, :], wa_ref[...])
        zc = jnp.concatenate([z_ref[:, k, :] for k in range(r // LANES, (r + sub) // LANES)], axis=-1)
        yb = _dot(jnp.transpose(zc).astype(BF16), wb_ref[...])
        gt = gate_ref[rs, :].astype(F32)
        d = ya.shape[1]
        m = gt[:, :d] * ya + gt[:, d:] * yb
        mixed = _dot(m.astype(BF16), wm_ref[...])
        o_ref[rs, :] = _ln(alpha * h_ref[rs, :] + mixed, g_ref[...], b_ref[...])


def _mix(ua, z4, gates, h, wa, wb, wm, g, b, *, tm, sub, alpha):
    bsz, length, d = h.shape
    c = ua.shape[2]
    sub = min(sub, tm)
    assert sub % LANES == 0 and ((tm // LANES) % SUBLANES == 0 or tm == length)
    const = lambda i, j: (0, 0)
    return pl.pallas_call(
        functools.partial(_mix_kernel, alpha=alpha, sub=sub),
        grid=(bsz, length // tm),
        in_specs=[pl.BlockSpec((None, tm, c), lambda i, j: (i, j, 0)),
                  pl.BlockSpec((None, c, tm // LANES, LANES), lambda i, j: (i, 0, j, 0)),
                  pl.BlockSpec((None, tm, 2 * d), lambda i, j: (i, j, 0)),
                  pl.BlockSpec((None, tm, d), lambda i, j: (i, j, 0)),
                  pl.BlockSpec((c, d), const), pl.BlockSpec((c, d), const), pl.BlockSpec((d, d), const),
                  pl.BlockSpec((1, d), const), pl.BlockSpec((1, d), const)],
        out_specs=pl.BlockSpec((None, tm, d), lambda i, j: (i, j, 0)),
        out_shape=jax.ShapeDtypeStruct((bsz, length, d), F32),
        compiler_params=_cparams("parallel", "parallel"),
        name="mix",
    )(ua, z4, gates, h, wa, wb, wm, g, b)


def _kv_kernel(mem_ref, wk_ref, wv_ref, k_ref, v_ref):
    mb = mem_ref[...].astype(BF16)
    k_ref[...] = _dot(mb, wk_ref[...]).astype(BF16)
    v_ref[...] = _dot(mb, wv_ref[...]).astype(BF16)


def _kv(mem, wk, wv):
    bsz, m, d = mem.shape
    const = lambda i: (0, 0)
    blk = pl.BlockSpec((None, m, d), lambda i: (i, 0, 0))
    return pl.pallas_call(
        _kv_kernel,
        grid=(bsz,),
        in_specs=[blk, pl.BlockSpec((d, d), const), pl.BlockSpec((d, d), const)],
        out_specs=[blk, blk],
        out_shape=[jax.ShapeDtypeStruct((bsz, m, d), BF16)] * 2,
        compiler_params=_cparams("parallel"),
        name="kv",
    )(mem, wk, wv)


def _xattn_kernel(x_ref, k_ref, v_ref, wq_ref, wo_ref, g_ref, b_ref, wrh_ref, wrl_ref,
                  x2_ref, x2b_ref, aff_ref, *, alpha, heads, sub):
    d = x_ref.shape[1]
    dh = d // heads
    wrh, wrl = wrh_ref[...], wrl_ref[...]
    for r in range(0, x_ref.shape[0], sub):
        rs = slice(r, r + sub)
        x = x_ref[rs, :]
        q = (_dot(x.astype(BF16), wq_ref[...]) * (dh ** -0.5)).astype(BF16)
        outs = []
        for hd in range(heads):
            sl = slice(hd * dh, (hd + 1) * dh)
            s = _dot_nt(q[:, sl], k_ref[:, sl])
            s = s - jnp.max(s, axis=-1, keepdims=True)
            p = jnp.exp(s)
            p = p / jnp.sum(p, axis=-1, keepdims=True)
            outs.append(_dot(p.astype(BF16), v_ref[:, sl]))
        o = jnp.concatenate(outs, axis=-1)
        xa = _dot(o.astype(BF16), wo_ref[...])
        x2 = _ln(alpha * x + xa, g_ref[...], b_ref[...])
        x2_ref[rs, :] = x2
        x2h, x2l = _split_bf16(x2)
        x2b_ref[rs, :] = x2h
        n_e = wrh.shape[0]
        both = _dot_nt(jnp.concatenate([wrh, wrl], axis=0), x2h)
        logits = both[:n_e] + (_dot_nt(wrh, x2l) + both[n_e:])
        logits = logits - jnp.max(logits, axis=0, keepdims=True)
        ex = jnp.exp(logits)
        aff_ref[:, rs] = ex / jnp.sum(ex, axis=0, keepdims=True)


def _xattn(x, k, v, wq, wo, g, b, wrh, wrl, *, tm, sub, alpha, heads):
    bsz, length, d = x.shape
    m = k.shape[1]
    e = wrh.shape[0]
    const = lambda i, j: (0, 0)
    tok = pl.BlockSpec((None, tm, d), lambda i, j: (i, j, 0))
    mem = pl.BlockSpec((None, m, d), lambda i, j: (i, 0, 0))
    return pl.pallas_call(
        functools.partial(_xattn_kernel, alpha=alpha, heads=heads, sub=min(sub, tm)),
        grid=(bsz, length // tm),
        in_specs=[tok, mem, mem, pl.BlockSpec((d, d), const), pl.BlockSpec((d, d), const),
                  pl.BlockSpec((1, d), const), pl.BlockSpec((1, d), const),
                  pl.BlockSpec((e, d), const), pl.BlockSpec((e, d), const)],
        out_specs=[tok, tok, pl.BlockSpec((None, e, tm), lambda i, j: (i, 0, j))],
        out_shape=[jax.ShapeDtypeStruct((bsz, length, d), F32),
                   jax.ShapeDtypeStruct((bsz, length, d), BF16),
                   jax.ShapeDtypeStruct((bsz, e, length), F32)],
        compiler_params=_cparams("parallel", "parallel"),
        name="xattn",
    )(x, k, v, wq, wo, g, b, wrh, wrl)


def _excl_cumsum_lanes(mask_f32, tri, write):
    e, t = mask_f32.shape
    carry = jnp.zeros((e, 1), F32)
    for c in range(t // LANES):
        m = mask_f32[:, c * LANES:(c + 1) * LANES]
        inc = _dot(m.astype(BF16), tri)
        write(c, inc - m + carry, carry)
        carry = carry + inc[:, LANES - 1:LANES]


def _select_kernel(aff_ref, tri_ref, slot_ref, slott_ref, gatet_ref, r0_ref, jlo_ref, nch_ref, cnt_ref, cum_ref,
                   *, cap, st, kct):
    a = aff_ref[...]
    e = a.shape[0]
    v = jnp.zeros((e, 1), I32)
    for bit in range(30, -1, -1):
        cand = v | (1 << bit)
        cnt = jnp.sum(jnp.where(a >= pltpu.bitcast(cand, F32), 1.0, 0.0), axis=1, keepdims=True)
        v = jnp.where(cnt >= cap, cand, v)
    thr = pltpu.bitcast(v, F32)
    gt = a > thr
    eq = a == thr
    need = cap - jnp.sum(jnp.where(gt, 1.0, 0.0), axis=1, keepdims=True)
    tri = tri_ref[...]

    def write_eq(c, val, before):
        cnt_ref[:, c * LANES:(c + 1) * LANES] = val
    _excl_cumsum_lanes(jnp.where(eq, 1.0, 0.0), tri, write_eq)
    sel = gt | (eq & (cnt_ref[...] < need))
    gate = jnp.where(sel, a, 0.0)

    n_tiles = a.shape[1] // LANES

    def write_rank(c, val, before):
        cnt_ref[:, c * LANES:(c + 1) * LANES] = val
        cum_ref[:, c:c + 1] = before
    _excl_cumsum_lanes(jnp.where(sel, 1.0, 0.0), tri, write_rank)
    slot = jnp.where(sel, cnt_ref[...], -1.0)
    slot_ref[...] = slot.astype(I32)
    slott_ref[...] = jnp.transpose(slot).astype(I32)
    gatet_ref[...] = jnp.transpose(gate)

    before = cum_ref[:, 0:n_tiles]
    r0_ref[...] = before.astype(I32)
    after = jnp.concatenate([before[:, 1:], jnp.full((e, 1), cap, F32)], axis=1)
    for s in range(cap // st):
        lo = jnp.sum(jnp.where(after <= s * st, 1.0, 0.0), axis=1, keepdims=True)
        hi = jnp.sum(jnp.where(before < (s + 1) * st, 1.0, 0.0), axis=1, keepdims=True) - 1.0
        n = jnp.ones_like(lo)
        for m in range(1, -(-n_tiles // kct)):
            n = n + jnp.where(hi - lo >= m * kct, 1.0, 0.0)
        jlo_ref[:, s:s + 1] = lo.astype(I32)
        nch_ref[:, s:s + 1] = n.astype(I32)


def _select(aff, tri, *, cap, st, kct):
    bsz, e, t = aff.shape
    n_tiles, n_st = t // LANES, cap // st
    blk = pl.BlockSpec((None, e, t), lambda i: (i, 0, 0))
    small = lambda n: pl.BlockSpec((None, e, n), lambda i: (i, 0, 0))
    return pl.pallas_call(
        functools.partial(_select_kernel, cap=cap, st=st, kct=kct),
        grid=(bsz,),
        in_specs=[blk, pl.BlockSpec((LANES, LANES), lambda i: (0, 0))],
        out_specs=[blk, pl.BlockSpec((None, t, e), lambda i: (i, 0, 0)),
                   pl.BlockSpec((None, t, e), lambda i: (i, 0, 0)),
                   small(n_tiles), small(n_st), small(n_st)],
        out_shape=[jax.ShapeDtypeStruct((bsz, e, t), I32), jax.ShapeDtypeStruct((bsz, t, e), I32),
                   jax.ShapeDtypeStruct((bsz, t, e), F32),
                   jax.ShapeDtypeStruct((bsz, e, n_tiles), I32),
                   jax.ShapeDtypeStruct((bsz, e, n_st), I32), jax.ShapeDtypeStruct((bsz, e, n_st), I32)],
        scratch_shapes=[pltpu.VMEM((e, t), F32), pltpu.VMEM((e, LANES), F32)],
        compiler_params=_cparams("parallel"),
        name="select",
    )(aff, tri)


def _window_start(r0, cap, win, align):
    return pl.multiple_of(jnp.minimum((r0 // align) * align, cap - win), align)


def _dispatch_kernel(jlo_ref, nch_ref, x_ref, slot_ref, o_ref, acc_ref, *, st, kc, tt):
    b, e = pl.program_id(0), pl.program_id(1)
    t_total = x_ref.shape[0]
    n_st = o_ref.shape[0] // st

    def chunk(s, c):
        want = (jlo_ref[b, e, s] + c * (kc // tt)) * tt
        t0 = pl.multiple_of(jnp.minimum(want, t_total - kc), tt)
        tok = t0 + lax.broadcasted_iota(I32, (1, kc), 1)
        rel = jnp.where(tok >= want, slot_ref[:, pl.ds(t0, kc)] - s * st, -1)
        onehot = jnp.where(lax.broadcasted_iota(I32, (st, kc), 0) == rel, 1.0, 0.0).astype(BF16)
        return _dot(onehot, x_ref[pl.ds(t0, kc), :])

    for s in range(n_st):
        acc_ref[s] = chunk(s, 0)
    for s in range(n_st):
        def body(c, carry, s=s):
            acc_ref[s] += chunk(s, c)
            return carry
        lax.fori_loop(1, nch_ref[b, e, s], body, 0)
    for s in range(n_st):
        o_ref[s * st:(s + 1) * st, :] = acc_ref[s].astype(o_ref.dtype)


def _dispatch(jlo, nch, xb, slot4, *, cap, st, kc, tt):
    bsz, t, d = xb.shape
    e = slot4.shape[1]
    assert cap % st == 0 and kc % tt == 0 and t % tt == 0 and kc <= t
    return pl.pallas_call(
        functools.partial(_dispatch_kernel, st=st, kc=kc, tt=tt),
        grid_spec=pltpu.PrefetchScalarGridSpec(
            num_scalar_prefetch=2,
            grid=(bsz, e),
            in_specs=[pl.BlockSpec((None, t, d), lambda i, j, a, c: (i, 0, 0)),
                      pl.BlockSpec((None, None, 1, t), lambda i, j, a, c: (i, j, 0, 0))],
            out_specs=pl.BlockSpec((None, None, cap, d), lambda i, j, a, c: (i, j, 0, 0)),
            scratch_shapes=[pltpu.VMEM((cap // st, st, d), F32)]),
        out_shape=jax.ShapeDtypeStruct((bsz, e, cap, d), BF16),
        compiler_params=_cparams("parallel", "parallel"),
        name="dispatch",
    )(jlo, nch, xb, slot4)


def _expert_kernel(x_ref, wg_ref, wu_ref, wd_ref, o_ref, acc_ref):
    f = pl.program_id(2)
    wg = wg_ref[...].astype(BF16)
    wu = wu_ref[...].astype(BF16)
    wd = wd_ref[...].astype(BF16)
    @pl.when(f == 0)
    def _():
        acc_ref[...] = jnp.zeros_like(acc_ref)

    for bi in range(x_ref.shape[0]):
        x = x_ref[bi]
        hg = _dot(x, wg)
        hu = _dot(x, wu)
        act = (hg * _sigmoid(hg) * hu).astype(BF16)
        acc_ref[bi] += _dot(act, wd)

    @pl.when(f == pl.num_programs(2) - 1)
    def _():
        o_ref[...] = acc_ref[...].astype(o_ref.dtype)


def _expert(xg, wg, wu, wd, *, fc, nb):
    bsz, e, cap, d = xg.shape
    f = wg.shape[2]
    assert bsz % nb == 0 and f % fc == 0
    tok = pl.BlockSpec((nb, None, cap, d), lambda i, j, k: (j, i, 0, 0))
    return pl.pallas_call(
        _expert_kernel,
        grid=(e, bsz // nb, f // fc),
        in_specs=[tok,
                  pl.BlockSpec((None, d, fc), lambda i, j, k: (i, 0, k)),
                  pl.BlockSpec((None, d, fc), lambda i, j, k: (i, 0, k)),
                  pl.BlockSpec((None, fc, d), lambda i, j, k: (i, k, 0))],
        out_specs=tok,
        out_shape=jax.ShapeDtypeStruct((bsz, e, cap, d), BF16),
        scratch_shapes=[pltpu.VMEM((nb, cap, d), F32)],
        compiler_params=_cparams("parallel", "parallel", "arbitrary"),
        name="expert",
    )(xg, wg, wu, wd)


def _combine_kernel(r0_ref, x_ref, slot_ref, gate_ref, g_ref, b_ref, y_ref, o_ref, *, tt, win, align, alpha):
    b, j = pl.program_id(0), pl.program_id(1)
    n_e, cap = y_ref.shape[0], y_ref.shape[1]
    n_sub = x_ref.shape[0] // tt
    for u in range(n_sub):
        rs = slice(u * tt, (u + 1) * tt)
        acc = alpha * x_ref[rs, :]
        slots = slot_ref[rs, :]
        gates = gate_ref[rs, :]
        for e in range(n_e):
            start = _window_start(r0_ref[b, e, j * n_sub + u], cap, win, align)
            rel = slots[:, e:e + 1] - start
            pick = jnp.where(lax.broadcasted_iota(I32, (tt, win), 1) == rel, gates[:, e:e + 1], 0.0)
            acc = acc + _dot(pick.astype(BF16), y_ref[e, pl.ds(start, win), :])
        o_ref[rs, :] = _ln(acc, g_ref[...], b_ref[...])


def _combine(r0, x, slot_t, gate_t, g, b, y, *, tt, n_sub, win, align, alpha):
    bsz, t, d = x.shape
    e, cap = y.shape[1], y.shape[2]
    assert win >= tt + align and win <= cap and (cap - win) % align == 0
    ts = tt * n_sub
    const = lambda i, j, r: (0, 0)
    return pl.pallas_call(
        functools.partial(_combine_kernel, tt=tt, win=win, align=align, alpha=alpha),
        grid_spec=pltpu.PrefetchScalarGridSpec(
            num_scalar_prefetch=1,
            grid=(bsz, t // ts),
            in_specs=[pl.BlockSpec((None, ts, d), lambda i, j, r: (i, j, 0)),
                      pl.BlockSpec((None, ts, e), lambda i, j, r: (i, j, 0)),
                      pl.BlockSpec((None, ts, e), lambda i, j, r: (i, j, 0)),
                      pl.BlockSpec((1, d), const), pl.BlockSpec((1, d), const),
                      pl.BlockSpec((None, e, cap, d), lambda i, j, r: (i, 0, 0, 0),
                                   pipeline_mode=pl.Buffered(1))],
            out_specs=pl.BlockSpec((None, ts, d), lambda i, j, r: (i, j, 0))),
        out_shape=jax.ShapeDtypeStruct((bsz, t, d), F32),
        compiler_params=_cparams("parallel", "parallel"),
        name="combine",
    )(r0, x, slot_t, gate_t, g, b, y)


def _hyena_constants(length):
    n = 2 * length
    m = np.arange(n)
    pos = np.where(m < length, m, n - m).astype(np.float64)
    pos[length] = 0.0
    t = pos / max(length - 1, 1)
    f = np.linspace(1e-4, HY_BANDS - 1, HY_BANDS)
    ang = (2.0 * np.pi * pos / length)[:, None] * f[None, :]
    feat = np.concatenate([t[:, None], np.cos(ang), -np.sin(ang)], axis=-1)
    feat_t = np.zeros((LANES, n), np.float32)
    feat_t[:feat.shape[1]] = feat.T
    valid = np.ones((1, n), np.float32)
    valid[0, length] = 0.0
    return feat_t, t[None, :].astype(np.float32), valid


def _decay_rates(width):
    max_decay = math.log(HY_DECAY_TARGET) / HY_FAST_DECAY
    min_decay = math.log(HY_DECAY_TARGET) / HY_SLOW_DECAY
    return np.abs(np.linspace(min_decay, max_decay, width, dtype=np.float32)).astype(np.float32)


def _dft_constants(r1):
    n, a1h, a2, at = _dft_tables(r1, r1 // 2)
    _, a1f, _, _ = _dft_tables(r1, r1)
    bf = lambda x: jnp.asarray(x, F32).astype(BF16)
    w1_data = bf(_cplx_rhs(a1h, -1.0))
    w1_filt = bf(np.concatenate([np.cos(a1f), -np.sin(a1f)], axis=1))
    w2 = bf(_cplx_rhs(a2, -1.0))
    w3 = bf(_cplx_rhs(a2.T, 1.0))
    w4 = bf(_cplx_rhs(a1h.T, 1.0) / n)
    twr, twi = np.cos(at), -np.sin(at)
    f32 = lambda x: jnp.asarray(x, F32)
    return dict(w1_data=w1_data, w1_filt=w1_filt, w2=w2, w3=w3, w4=w4,
                twr=f32(twr), twi=f32(twi), twr_t=f32(twr.T), twi_t=f32(twi.T))


def _tile_plan(length, hy_width, d_ff):
    tt = LANES
    return dict(
        filt_ffn_tn=min(2048, 2 * length),
        filt_rows=min(128, hy_width),
        kfft_rows=min(32, hy_width),
        inproj_tm=min(512, length),
        conf_tl=min(1024, length),
        hyena_g=min(32, hy_width),
        mix_tm=min(1024, length), mix_sub=512,
        xattn_tm=min(1024, length), xattn_sub=1024,
        tt=tt,
        st=LANES,
        kc=min(10 * tt, length),
        align=2 * SUBLANES,
        expert_fc=min(512, d_ff), expert_nb=2,
        combine_sub=min(4, length // tt),
    )


def kernel(x, mem, ln_in_g, ln_in_b, w_in, b_gate, conf_dw_w, conf_dw_b, conf_ln_g, conf_ln_b, conf_w_out, hy_short_w, hy_short_b, hy_ffn_w1, hy_ffn_b1, hy_freq1, hy_ffn_w2, hy_ffn_b2, hy_freq2, hy_ffn_w3, hy_skip, hy_w_out, w_mix_out, ln_mix_g, ln_mix_b, xa_wq, xa_wk, xa_wv, xa_wo, ln_xa_g, ln_xa_b, moe_w_router, moe_w_gate, moe_w_up, moe_w_down, ln_moe_g, ln_moe_b):
    bsz, length, d = x.shape
    depth = w_in.shape[0]
    alpha = (2.0 * depth) ** 0.25
    cw = conf_dw_w.shape[2]
    hw = hy_skip.shape[2]
    n_glu, n_hy = 2 * cw, 3 * hw
    n_exp = moe_w_router.shape[2]
    cap = max(1, N_EXPERT_CAPACITY * length // n_exp)
    rh = length // LANES
    r1 = 2 * rh
    assert bsz % 2 == 0 and length % LANES == 0

    row = lambda v: v.reshape(1, -1).astype(F32)
    col = lambda v: v.reshape(-1, 1).astype(F32)

    feat_t, t_ext, valid = _hyena_constants(length)
    dft = _dft_constants(r1)
    rates = _decay_rates(hw)
    tri = jnp.asarray(np.triu(np.ones((LANES, LANES), np.float32))).astype(BF16)

    n_ord = hy_skip.shape[1]
    tp = _tile_plan(length, hw, moe_w_gate.shape[3])
    lane_b = lambda v: jnp.broadcast_to(v[..., None, None], v.shape + (1, LANES)).astype(F32)
    dft_data = (dft["w1_data"], dft["twr"], dft["twi"], dft["w2"], dft["w3"], dft["twr_t"], dft["twi_t"],
                dft["w4"])

    h = x
    for i in range(depth):
        w1h, w1l = _split_bf16(jnp.pad(hy_ffn_w1[i].T, ((0, 0), (0, LANES - hy_ffn_w1.shape[1]))))
        w2h, w2l = _split_bf16(hy_ffn_w2[i].T)
        h2t = _filt_ffn(jnp.asarray(feat_t), w1h, w1l, col(hy_ffn_b1[i]), col(hy_freq1[i]),
                        w2h, w2l, col(hy_ffn_b2[i]), col(hy_freq2[i]), tn=tp["filt_ffn_tn"])
        w3 = hy_ffn_w3[i].reshape(-1, n_ord, 2, hw)
        w3f = jnp.transpose(w3[:, :, 0], (1, 2, 0)).reshape(n_ord * hw, -1)
        w3b = jnp.transpose(w3[:, :, 1], (1, 2, 0)).reshape(n_ord * hw, -1)
        w3fh, w3fl = _split_bf16(w3f)
        w3bh, w3bl = _split_bf16(w3b)
        delta = jnp.asarray(np.tile(rates, n_ord).reshape(-1, 1))
        kern = _filt(h2t, w3fh, w3fl, w3bh, w3bl, delta, jnp.asarray(t_ext), jnp.asarray(valid),
                     g=tp["filt_rows"])
        kf = _kfft(kern.reshape(n_ord * hw, r1, LANES), dft["w1_filt"], dft["twr"], dft["twi"], dft["w2"],
                   g=tp["kfft_rows"])

        wi = w_in[i]
        wglu = wi[:, :n_glu].astype(BF16)
        whyt = wi[:, n_glu:n_glu + n_hy].T.astype(BF16)
        wgate = wi[:, n_glu + n_hy:].astype(BF16)
        hn, u, hyt, gates = _inproj(h, row(ln_in_g), row(ln_in_b), wglu, whyt, wgate, row(b_gate[i]),
                                    tm=tp["inproj_tm"], apply_ln=(i == 0))

        ua = _conf(u, conf_dw_w[i], row(conf_dw_b[i]), row(conf_ln_g[i]), row(conf_ln_b[i]),
                   tl=tp["conf_tl"])

        z = _hyena(hyt.reshape(bsz, n_hy, rh, LANES), lane_b(hy_short_w[i]), lane_b(hy_short_b[i]), kf,
                   lane_b(hy_skip[i]), dft_data, g=tp["hyena_g"])

        x1 = _mix(ua, z, gates, hn, conf_w_out[i].astype(BF16),
                  hy_w_out[i].astype(BF16), w_mix_out[i].astype(BF16), row(ln_mix_g[i]), row(ln_mix_b[i]),
                  tm=tp["mix_tm"], sub=tp["mix_sub"], alpha=alpha)

        k, v = _kv(mem, xa_wk[i].astype(BF16), xa_wv[i].astype(BF16))
        wrh, wrl = _split_bf16(moe_w_router[i].T)
        x2, x2b, aff = _xattn(x1, k, v, xa_wq[i].astype(BF16), xa_wo[i].astype(BF16), row(ln_xa_g[i]),
                              row(ln_xa_b[i]), wrh, wrl, tm=tp["xattn_tm"], sub=tp["xattn_sub"], alpha=alpha,
                              heads=XA_HEADS)

        tt, st, kc, align = tp["tt"], tp["st"], tp["kc"], tp["align"]
        slot, slot_t, gate_t, r0, jlo, nch = _select(aff, tri, cap=cap, st=st, kct=kc // tt)
        xg = _dispatch(jlo, nch, x2b, slot.reshape(bsz, n_exp, 1, length), cap=cap, st=st, kc=kc, tt=tt)
        y = _expert(xg, moe_w_gate[i], moe_w_up[i], moe_w_down[i], fc=tp["expert_fc"], nb=tp["expert_nb"])
        h = _combine(r0, x2, slot_t, gate_t, row(ln_moe_g[i]), row(ln_moe_b[i]), y,
                     tt=tt, n_sub=tp["combine_sub"], win=tt + align, align=align, alpha=alpha)
    return h
```

```python
import functools
import math

import numpy as np
import jax
import jax.numpy as jnp
from jax import lax
from jax.experimental import pallas as pl
from jax.experimental.pallas import tpu as pltpu

F32 = jnp.float32
BF16 = jnp.bfloat16
I32 = jnp.int32

LANES = 128
SUBLANES = 8
VMEM_LIMIT_BYTES = 56 * 1024 * 1024
SUB_ROWS = 256

LN_EPS = 1e-5
XA_HEADS = 4
N_EXPERT_CAPACITY = 2
HY_DECAY_TARGET = 1e-2
HY_FAST_DECAY = 0.3
HY_SLOW_DECAY = 1.5
HY_BANDS = 16


def _cparams(*sem):
    return pltpu.CompilerParams(dimension_semantics=sem, vmem_limit_bytes=VMEM_LIMIT_BYTES)


def _ln(x, g, b):
    mu = jnp.mean(x, axis=-1, keepdims=True)
    xc = x - mu
    var = jnp.mean(xc * xc, axis=-1, keepdims=True)
    return xc * lax.rsqrt(var + LN_EPS) * g + b


def _sigmoid(x):
    return 1.0 / (1.0 + jnp.exp(-x))


def _dot(a, b):
    return jnp.dot(a, b, preferred_element_type=F32)


def _dot_nt(a, b):
    return lax.dot_general(a, b, (((1,), (1,)), ((), ())), preferred_element_type=F32)


def _split_bf16(x):
    hi = x.astype(BF16)
    lo = (x - hi.astype(F32)).astype(BF16)
    return hi, lo


def _dot3_lhs(a_hi, a_lo, b):
    b_hi, b_lo = _split_bf16(b)
    return _dot(a_hi, b_hi) + (_dot(a_hi, b_lo) + _dot(a_lo, b_hi))


def _inproj_kernel(x_ref, g_ref, b_ref, wglu_ref, whyt_ref, wgate_ref, bgate_ref,
                   h_ref, u_ref, hyt_ref, gate_ref, *, apply_ln, sub):
    for r in range(0, x_ref.shape[0], sub):
        rs = slice(r, r + sub)
        h = _ln(x_ref[rs, :], g_ref[...], b_ref[...]) if apply_ln else x_ref[rs, :]
        h_ref[rs, :] = h
        hb = h.astype(BF16)
        glu = _dot(hb, wglu_ref[...])
        cw = glu.shape[1] // 2
        u_ref[rs, :] = glu[:, :cw] * _sigmoid(glu[:, cw:])
        hyt_ref[:, rs] = _dot_nt(whyt_ref[...], hb).astype(hyt_ref.dtype)
        gl = _dot(hb, wgate_ref[...]) + bgate_ref[...]
        gate_ref[rs, :] = _sigmoid(gl).astype(BF16)


def _inproj(x, g, b, wglu, whyt, wgate, bgate, *, tm, apply_ln):
    bsz, length, d = x.shape
    n_glu, n_hy, n_gate = wglu.shape[1], whyt.shape[0], wgate.shape[1]
    const = lambda i, j: (0, 0)
    return pl.pallas_call(
        functools.partial(_inproj_kernel, apply_ln=apply_ln, sub=min(SUB_ROWS, tm)),
        grid=(bsz, length // tm),
        in_specs=[
            pl.BlockSpec((None, tm, d), lambda i, j: (i, j, 0)),
            pl.BlockSpec((1, d), const), pl.BlockSpec((1, d), const),
            pl.BlockSpec((d, n_glu), const),
            pl.BlockSpec((n_hy, d), const),
            pl.BlockSpec((d, n_gate), const),
            pl.BlockSpec((1, n_gate), const),
        ],
        out_specs=[
            pl.BlockSpec((None, tm, d), lambda i, j: (i, j, 0)),
            pl.BlockSpec((None, tm, n_glu // 2), lambda i, j: (i, j, 0)),
            pl.BlockSpec((None, n_hy, tm), lambda i, j: (i, 0, j)),
            pl.BlockSpec((None, tm, n_gate), lambda i, j: (i, j, 0)),
        ],
        out_shape=[
            jax.ShapeDtypeStruct((bsz, length, d), F32),
            jax.ShapeDtypeStruct((bsz, length, n_glu // 2), F32),
            jax.ShapeDtypeStruct((bsz, n_hy, length), BF16),
            jax.ShapeDtypeStruct((bsz, length, n_gate), BF16),
        ],
        compiler_params=_cparams("parallel", "parallel"),
        name="inproj",
    )(x, g, b, wglu, whyt, wgate, bgate)


def _conf_kernel(prev_ref, cur_ref, next_ref, w_ref, cb_ref, g_ref, b_ref, o_ref, ext_ref, sh_ref,
                 *, tl, halo, ksize, rows):
    j = pl.program_id(1)
    nj = pl.num_programs(1)
    pad = (ksize - 1) // 2
    ext_ref[0:halo, :] = jnp.where(j > 0, prev_ref[...], 0.0)
    ext_ref[halo:halo + tl, :] = cur_ref[...]
    ext_ref[halo + tl:halo + tl + halo, :] = jnp.where(j < nj - 1, next_ref[...], 0.0)
    n_sh = sh_ref.shape[1]
    for r in range(1, SUBLANES):
        sh_ref[r - 1, :, :] = ext_ref[r:r + n_sh, :]
    w = w_ref[...]
    cb, g, b = cb_ref[...], g_ref[...], b_ref[...]
    for r0 in range(0, tl, rows):
        acc = jnp.zeros((rows, w.shape[1]), F32)
        for k in range(ksize):
            q, r = divmod(halo - pad + k + r0, SUBLANES)
            q *= SUBLANES
            src = ext_ref[q:q + rows, :] if r == 0 else sh_ref[r - 1, q:q + rows, :]
            acc = acc + w[k:k + 1, :] * src
        y = _ln(acc + cb, g, b)
        o_ref[r0:r0 + rows, :] = (y * _sigmoid(y)).astype(o_ref.dtype)


def _conf(u, w, cb, g, b, *, tl, halo=16, rows=64):
    bsz, length, c = u.shape
    ksize = w.shape[0]
    assert (ksize - 1) // 2 <= halo and tl % halo == 0 and tl % rows == 0
    nh = tl // halo
    last = length // halo - 1
    const = lambda i, j: (0, 0)
    return pl.pallas_call(
        functools.partial(_conf_kernel, tl=tl, halo=halo, ksize=ksize, rows=rows),
        grid=(bsz, length // tl),
        in_specs=[
            pl.BlockSpec((None, halo, c), lambda i, j: (i, jnp.maximum(j * nh - 1, 0), 0)),
            pl.BlockSpec((None, tl, c), lambda i, j: (i, j, 0)),
            pl.BlockSpec((None, halo, c), lambda i, j: (i, jnp.minimum((j + 1) * nh, last), 0)),
            pl.BlockSpec((ksize, c), const),
            pl.BlockSpec((1, c), const), pl.BlockSpec((1, c), const), pl.BlockSpec((1, c), const),
        ],
        out_specs=pl.BlockSpec((None, tl, c), lambda i, j: (i, j, 0)),
        out_shape=jax.ShapeDtypeStruct((bsz, length, c), BF16),
        scratch_shapes=[pltpu.VMEM((tl + 2 * halo, c), F32),
                        pltpu.VMEM((SUBLANES - 1, tl + 2 * halo - SUBLANES, c), F32)],
        compiler_params=_cparams("parallel", "parallel"),
        name="conf",
    )(u, u, u, w, cb, g, b)


def _filt_ffn_kernel(feat_ref, w1h_ref, w1l_ref, b1_ref, f1_ref, w2h_ref, w2l_ref, b2_ref, f2_ref, o_ref):
    a = _dot3_lhs(w1h_ref[...], w1l_ref[...], feat_ref[...]) + b1_ref[...]
    h = jnp.sin(f1_ref[...] * a)
    a2 = _dot3_lhs(w2h_ref[...], w2l_ref[...], h) + b2_ref[...]
    o_ref[...] = jnp.sin(f2_ref[...] * a2)


def _filt_ffn(feat_t, w1h, w1l, b1, f1, w2h, w2l, b2, f2, *, tn):
    kp, n = feat_t.shape
    m = w1h.shape[0]
    const = lambda i: (0, 0)
    return pl.pallas_call(
        _filt_ffn_kernel,
        grid=(n // tn,),
        in_specs=[pl.BlockSpec((kp, tn), lambda i: (0, i)),
                  pl.BlockSpec((m, kp), const), pl.BlockSpec((m, kp), const),
                  pl.BlockSpec((m, 1), const), pl.BlockSpec((m, 1), const),
                  pl.BlockSpec((m, m), const), pl.BlockSpec((m, m), const),
                  pl.BlockSpec((m, 1), const), pl.BlockSpec((m, 1), const)],
        out_specs=pl.BlockSpec((m, tn), lambda i: (0, i)),
        out_shape=jax.ShapeDtypeStruct((m, n), F32),
        compiler_params=_cparams("parallel"),
        name="filt_ffn",
    )(feat_t, w1h, w1l, b1, f1, w2h, w2l, b2, f2)


def _filt_kernel(h2_ref, w3fh_ref, w3fl_ref, w3bh_ref, w3bl_ref, delta_ref, text_ref, valid_ref, o_ref):
    n = h2_ref.shape[1]
    half = n // 2
    decay = jnp.exp(-text_ref[...] * delta_ref[...]) * valid_ref[...]
    kf = _dot3_lhs(w3fh_ref[...], w3fl_ref[...], h2_ref[:, 0:half]) * decay[:, 0:half]
    kb = _dot3_lhs(w3bh_ref[...], w3bl_ref[...], h2_ref[:, half:n]) * decay[:, half:n]
    ss = jnp.sum(kf * kf, axis=1, keepdims=True) + jnp.sum(kb * kb, axis=1, keepdims=True)
    scale = lax.rsqrt(ss + 1e-6)
    o_ref[:, 0:half] = kf * scale
    o_ref[:, half:n] = kb * scale


def _filt(h2t, w3fh, w3fl, w3bh, w3bl, delta, text, valid, *, g):
    m, n = h2t.shape
    rows = w3fh.shape[0]
    const = lambda i: (0, 0)
    return pl.pallas_call(
        _filt_kernel,
        grid=(rows // g,),
        in_specs=[pl.BlockSpec((m, n), const),
                  pl.BlockSpec((g, m), lambda i: (i, 0)), pl.BlockSpec((g, m), lambda i: (i, 0)),
                  pl.BlockSpec((g, m), lambda i: (i, 0)), pl.BlockSpec((g, m), lambda i: (i, 0)),
                  pl.BlockSpec((g, 1), lambda i: (i, 0)),
                  pl.BlockSpec((1, n), const), pl.BlockSpec((1, n), const)],
        out_specs=pl.BlockSpec((g, n), lambda i: (i, 0)),
        out_shape=jax.ShapeDtypeStruct((rows, n), F32),
        compiler_params=_cparams("parallel"),
        name="filt",
    )(h2t, w3fh, w3fl, w3bh, w3bl, delta, text, valid)


def _dft_tables(r1, n1_used):
    n = r1 * LANES
    a1 = 2.0 * np.pi * np.outer(np.arange(n1_used), np.arange(r1)) / r1
    a2 = 2.0 * np.pi * np.outer(np.arange(LANES), np.arange(LANES)) / LANES
    at = 2.0 * np.pi * np.outer(np.arange(LANES), np.arange(r1)) / n
    return n, a1, a2, at


def _cplx_rhs(ang, sign):
    c, s = np.cos(ang), sign * np.sin(ang)
    return np.block([[c, s], [-s, c]])


def _fwd_steps(x_t, w1, twr, twi, w2):
    g, n2, k1w = x_t.shape
    r1 = twr.shape[1]
    a = _dot(x_t.reshape(g * n2, k1w).astype(BF16), w1).reshape(g, n2, 2 * r1)
    ar, ai = a[:, :, :r1], a[:, :, r1:]
    br = ar * twr - ai * twi
    bi = ar * twi + ai * twr
    bt = jnp.concatenate([jnp.swapaxes(br.astype(BF16), 1, 2), jnp.swapaxes(bi.astype(BF16), 1, 2)],
                         axis=-1)
    c = _dot(bt.reshape(g * r1, 2 * n2), w2)
    return c.reshape(g, r1, 2 * n2)


def _inv_steps(d, w3, twr_t, twi_t, w4):
    g, r1, w = d.shape
    n2 = w // 2
    e = _dot(d.reshape(g * r1, w).astype(BF16), w3).reshape(g, r1, w)
    er, ei = e[:, :, :n2], e[:, :, n2:]
    fr = er * twr_t + ei * twi_t
    fi = ei * twr_t - er * twi_t
    ft = jnp.concatenate([jnp.swapaxes(fr.astype(BF16), 1, 2), jnp.swapaxes(fi.astype(BF16), 1, 2)],
                         axis=-1)
    y = _dot(ft.reshape(g * n2, 2 * r1), w4)
    return y.reshape(g, n2, w4.shape[1])


def _kfft_kernel(k_ref, w1_ref, twr_ref, twi_ref, w2_ref, o_ref):
    x_t = jnp.swapaxes(k_ref[...].astype(BF16), 1, 2)
    o_ref[...] = _fwd_steps(x_t, w1_ref[...], twr_ref[...], twi_ref[...], w2_ref[...])


def _kfft(k3, w1, twr, twi, w2, *, g):
    rows, r1, lanes = k3.shape
    const = lambda i: (0, 0)
    return pl.pallas_call(
        _kfft_kernel,
        grid=(rows // g,),
        in_specs=[pl.BlockSpec((g, r1, lanes), lambda i: (i, 0, 0)),
                  pl.BlockSpec(w1.shape, const), pl.BlockSpec(twr.shape, const),
                  pl.BlockSpec(twi.shape, const), pl.BlockSpec(w2.shape, const)],
        out_specs=pl.BlockSpec((g, r1, 2 * lanes), lambda i: (i, 0, 0)),
        out_shape=jax.ShapeDtypeStruct((rows, r1, 2 * lanes), F32),
        compiler_params=_cparams("parallel"),
        name="kfft",
    )(k3, w1, twr, twi, w2)


def _short_conv(a, w, b):
    p, g, r, l = a.shape
    a3 = a.reshape(p * g, r, l)
    lane = lax.broadcasted_iota(I32, a3.shape, 2)
    row = lax.broadcasted_iota(I32, a3.shape, 1)
    pl_ = pltpu.roll(a3, 1, 2)
    ql = pltpu.roll(pl_, 1, 1)
    prev = jnp.where(lane == 0, jnp.where(row == 0, 0.0, ql), pl_)
    pr = pltpu.roll(a3, l - 1, 2)
    qr = pltpu.roll(pr, r - 1, 1)
    nxt = jnp.where(lane == l - 1, jnp.where(row == r - 1, 0.0, qr), pr)
    prev, nxt = prev.reshape(a.shape), nxt.reshape(a.shape)
    return w[0][None] * prev + w[1][None] * a + w[2][None] * nxt + b[None]


def _hyena_kernel(x1_ref, x2_ref, v_ref, sw1_ref, sb1_ref, sw2_ref, sb2_ref, swv_ref, sbv_ref,
                  kf0_ref, kf1_ref, skip_ref, w1_ref, twr_ref, twi_ref, w2_ref, w3_ref,
                  twrt_ref, twit_ref, w4_ref, z_ref):
    x1 = _short_conv(x1_ref[...].astype(F32), sw1_ref[...], sb1_ref[...])
    x2 = _short_conv(x2_ref[...].astype(F32), sw2_ref[...], sb2_ref[...])
    z = _short_conv(v_ref[...].astype(F32), swv_ref[...], sbv_ref[...])
    rh = z.shape[2]
    for gate, kf_ref, o in ((x1, kf0_ref, 0), (x2, kf1_ref, 1)):
        s = jnp.concatenate([z[0], z[1]], axis=1)
        c = _fwd_steps(jnp.swapaxes(s, 1, 2), w1_ref[...], twr_ref[...], twi_ref[...], w2_ref[...])
        kf = kf_ref[...]
        n2 = c.shape[2] // 2
        cr, ci, kr, ki = c[:, :, :n2], c[:, :, n2:], kf[:, :, :n2], kf[:, :, n2:]
        d = jnp.concatenate([cr * kr - ci * ki, cr * ki + ci * kr], axis=-1)
        y = _inv_steps(d, w3_ref[...], twrt_ref[...], twit_ref[...], w4_ref[...])
        y = jnp.swapaxes(y, 1, 2)
        conv = jnp.stack([y[:, :rh], y[:, rh:]], axis=0)
        z = gate * (conv + skip_ref[o][None] * z)
    z_ref[...] = z


def _hyena(hy4, sw, sb, kf, skip, tabs, *, g):
    bsz, c3, rh, lanes = hy4.shape
    c = c3 // 3
    r1 = 2 * rh
    ncb = c // g
    w1, twr, twi, w2, w3, twrt, twit, w4 = tabs
    const2 = lambda i, p: (0, 0)
    data = lambda off: pl.BlockSpec((2, g, rh, lanes), lambda i, p, off=off: (p, off + i, 0, 0))
    wspec = lambda off: pl.BlockSpec((3, g, 1, lanes), lambda i, p, off=off: (0, off + i, 0, 0))
    bspec = lambda off: pl.BlockSpec((g, 1, lanes), lambda i, p, off=off: (off + i, 0, 0))
    kspec = lambda off: pl.BlockSpec((g, r1, 2 * lanes), lambda i, p, off=off: (off + i, 0, 0))
    return pl.pallas_call(
        _hyena_kernel,
        grid=(ncb, bsz // 2),
        in_specs=[data(0), data(ncb), data(2 * ncb),
                  wspec(0), bspec(0), wspec(ncb), bspec(ncb), wspec(2 * ncb), bspec(2 * ncb),
                  kspec(0), kspec(ncb),
                  pl.BlockSpec((2, g, 1, lanes), lambda i, p: (0, i, 0, 0)),
                  pl.BlockSpec(w1.shape, const2), pl.BlockSpec(twr.shape, const2),
                  pl.BlockSpec(twi.shape, const2), pl.BlockSpec(w2.shape, const2),
                  pl.BlockSpec(w3.shape, const2), pl.BlockSpec(twrt.shape, const2),
                  pl.BlockSpec(twit.shape, const2), pl.BlockSpec(w4.shape, const2)],
        out_specs=pl.BlockSpec((2, g, rh, lanes), lambda i, p: (p, i, 0, 0)),
        out_shape=jax.ShapeDtypeStruct((bsz, c, rh, lanes), F32),
        compiler_params=_cparams("parallel", "parallel"),
        name="hyena",
    )(hy4, hy4, hy4, sw, sb, sw, sb, sw, sb, kf, kf, skip, w1, twr, twi, w2, w3, twrt, twit, w4)


def _mix_kernel(ua_ref, z_ref, gate_ref, h_ref, wa_ref, wb_ref, wm_ref, g_ref, b_ref, o_ref, *, alpha, sub):
    for r in range(0, h_ref.shape[0], sub):
        rs = slice(r, r + sub)
        ya = _dot(ua_ref[rs, :], wa_ref[...])
        zc = jnp.concatenate([z_ref[:, k, :] for k in range(r // LANES, (r + sub) // LANES)], axis=-1)
        yb = _dot(jnp.transpose(zc).astype(BF16), wb_ref[...])
        gt = gate_ref[rs, :].astype(F32)
        d = ya.shape[1]
        m = gt[:, :d] * ya + gt[:, d:] * yb
        mixed = _dot(m.astype(BF16), wm_ref[...])
        o_ref[rs, :] = _ln(alpha * h_ref[rs, :] + mixed, g_ref[...], b_ref[...])


def _mix(ua, z4, gates, h, wa, wb, wm, g, b, *, tm, sub, alpha):
    bsz, length, d = h.shape
    c = ua.shape[2]
    sub = min(sub, tm)
    assert sub % LANES == 0 and ((tm // LANES) % SUBLANES == 0 or tm == length)
    const = lambda i, j: (0, 0)
    return pl.pallas_call(
        functools.partial(_mix_kernel, alpha=alpha, sub=sub),
        grid=(bsz, length // tm),
        in_specs=[pl.BlockSpec((None, tm, c), lambda i, j: (i, j, 0)),
                  pl.BlockSpec((None, c, tm // LANES, LANES), lambda i, j: (i, 0, j, 0)),
                  pl.BlockSpec((None, tm, 2 * d), lambda i, j: (i, j, 0)),
                  pl.BlockSpec((None, tm, d), lambda i, j: (i, j, 0)),
                  pl.BlockSpec((c, d), const), pl.BlockSpec((c, d), const), pl.BlockSpec((d, d), const),
                  pl.BlockSpec((1, d), const), pl.BlockSpec((1, d), const)],
        out_specs=pl.BlockSpec((None, tm, d), lambda i, j: (i, j, 0)),
        out_shape=jax.ShapeDtypeStruct((bsz, length, d), F32),
        compiler_params=_cparams("parallel", "parallel"),
        name="mix",
    )(ua, z4, gates, h, wa, wb, wm, g, b)


def _xattn_kernel(x_ref, mem_ref, wk_ref, wv_ref, wq_ref, wo_ref, g_ref, b_ref, wrh_ref, wrl_ref,
                  x2_ref, x2b_ref, aff_ref, k_ref, v_ref, *, alpha, heads, sub):
    d = x_ref.shape[1]
    dh = d // heads
    wrh, wrl = wrh_ref[...], wrl_ref[...]

    @pl.when(pl.program_id(1) == 0)
    def _():
        mb = mem_ref[...].astype(BF16)
        k_ref[...] = _dot(mb, wk_ref[...]).astype(BF16)
        v_ref[...] = _dot(mb, wv_ref[...]).astype(BF16)

    for r in range(0, x_ref.shape[0], sub):
        rs = slice(r, r + sub)
        x = x_ref[rs, :]
        q = (_dot(x.astype(BF16), wq_ref[...]) * (dh ** -0.5)).astype(BF16)
        outs = []
        for hd in range(heads):
            sl = slice(hd * dh, (hd + 1) * dh)
            s = _dot_nt(q[:, sl], k_ref[:, sl])
            s = s - jnp.max(s, axis=-1, keepdims=True)
            p = jnp.exp(s)
            p = p / jnp.sum(p, axis=-1, keepdims=True)
            outs.append(_dot(p.astype(BF16), v_ref[:, sl]))
        o = jnp.concatenate(outs, axis=-1)
        xa = _dot(o.astype(BF16), wo_ref[...])
        x2 = _ln(alpha * x + xa, g_ref[...], b_ref[...])
        x2_ref[rs, :] = x2
        x2h, x2l = _split_bf16(x2)
        x2b_ref[rs, :] = x2h
        n_e = wrh.shape[0]
        both = _dot_nt(jnp.concatenate([wrh, wrl], axis=0), x2h)
        logits = both[:n_e] + (_dot_nt(wrh, x2l) + both[n_e:])
        logits = logits - jnp.max(logits, axis=0, keepdims=True)
        ex = jnp.exp(logits)
        aff_ref[:, rs] = ex / jnp.sum(ex, axis=0, keepdims=True)


def _xattn(x, mem, wk, wv, wq, wo, g, b, wrh, wrl, *, tm, sub, alpha, heads):
    bsz, length, d = x.shape
    m = mem.shape[1]
    e = wrh.shape[0]
    const = lambda i, j: (0, 0)
    tok = pl.BlockSpec((None, tm, d), lambda i, j: (i, j, 0))
    sq = pl.BlockSpec((d, d), const)
    return pl.pallas_call(
        functools.partial(_xattn_kernel, alpha=alpha, heads=heads, sub=min(sub, tm)),
        grid=(bsz, length // tm),
        in_specs=[tok, pl.BlockSpec((None, m, d), lambda i, j: (i, 0, 0)), sq, sq, sq, sq,
                  pl.BlockSpec((1, d), const), pl.BlockSpec((1, d), const),
                  pl.BlockSpec((e, d), const), pl.BlockSpec((e, d), const)],
        out_specs=[tok, tok, pl.BlockSpec((None, e, tm), lambda i, j: (i, 0, j))],
        out_shape=[jax.ShapeDtypeStruct((bsz, length, d), F32),
                   jax.ShapeDtypeStruct((bsz, length, d), BF16),
                   jax.ShapeDtypeStruct((bsz, e, length), F32)],
        scratch_shapes=[pltpu.VMEM((m, d), BF16), pltpu.VMEM((m, d), BF16)],
        compiler_params=_cparams("arbitrary", "arbitrary"),
        name="xattn",
    )(x, mem, wk, wv, wq, wo, g, b, wrh, wrl)


def _excl_cumsum_lanes(mask_f32, tri, write):
    e, t = mask_f32.shape
    carry = jnp.zeros((e, 1), F32)
    for c in range(t // LANES):
        m = mask_f32[:, c * LANES:(c + 1) * LANES]
        inc = _dot(m.astype(BF16), tri)
        write(c, inc - m + carry, carry)
        carry = carry + inc[:, LANES - 1:LANES]


def _select_kernel(aff_ref, tri_ref, slot_ref, slott_ref, gatet_ref, r0_ref, jlo_ref, nch_ref, cnt_ref, cum_ref,
                   *, cap, st, kct):
    a = aff_ref[...]
    e = a.shape[0]
    v = jnp.zeros((e, 1), I32)
    for bit in range(30, -1, -1):
        cand = v | (1 << bit)
        cnt = jnp.sum(jnp.where(a >= pltpu.bitcast(cand, F32), 1.0, 0.0), axis=1, keepdims=True)
        v = jnp.where(cnt >= cap, cand, v)
    thr = pltpu.bitcast(v, F32)
    gt = a > thr
    eq = a == thr
    need = cap - jnp.sum(jnp.where(gt, 1.0, 0.0), axis=1, keepdims=True)
    tri = tri_ref[...]

    def write_eq(c, val, before):
        cnt_ref[:, c * LANES:(c + 1) * LANES] = val
    _excl_cumsum_lanes(jnp.where(eq, 1.0, 0.0), tri, write_eq)
    sel = gt | (eq & (cnt_ref[...] < need))
    gate = jnp.where(sel, a, 0.0)

    n_tiles = a.shape[1] // LANES

    def write_rank(c, val, before):
        cnt_ref[:, c * LANES:(c + 1) * LANES] = val
        cum_ref[:, c:c + 1] = before
    _excl_cumsum_lanes(jnp.where(sel, 1.0, 0.0), tri, write_rank)
    slot = jnp.where(sel, cnt_ref[...], -1.0)
    slot_ref[...] = slot.astype(I32)
    slott_ref[...] = jnp.transpose(slot).astype(I32)
    gatet_ref[...] = jnp.transpose(gate)

    before = cum_ref[:, 0:n_tiles]
    r0_ref[...] = before.astype(I32)
    after = jnp.concatenate([before[:, 1:], jnp.full((e, 1), cap, F32)], axis=1)
    for s in range(cap // st):
        lo = jnp.sum(jnp.where(after <= s * st, 1.0, 0.0), axis=1, keepdims=True)
        hi = jnp.sum(jnp.where(before < (s + 1) * st, 1.0, 0.0), axis=1, keepdims=True) - 1.0
        n = jnp.ones_like(lo)
        for m in range(1, -(-n_tiles // kct)):
            n = n + jnp.where(hi - lo >= m * kct, 1.0, 0.0)
        jlo_ref[:, s:s + 1] = lo.astype(I32)
        nch_ref[:, s:s + 1] = n.astype(I32)


def _select(aff, tri, *, cap, st, kct):
    bsz, e, t = aff.shape
    n_tiles, n_st = t // LANES, cap // st
    blk = pl.BlockSpec((None, e, t), lambda i: (i, 0, 0))
    small = lambda n: pl.BlockSpec((None, e, n), lambda i: (i, 0, 0))
    return pl.pallas_call(
        functools.partial(_select_kernel, cap=cap, st=st, kct=kct),
        grid=(bsz,),
        in_specs=[blk, pl.BlockSpec((LANES, LANES), lambda i: (0, 0))],
        out_specs=[blk, pl.BlockSpec((None, t, e), lambda i: (i, 0, 0)),
                   pl.BlockSpec((None, t, e), lambda i: (i, 0, 0)),
                   small(n_tiles), small(n_st), small(n_st)],
        out_shape=[jax.ShapeDtypeStruct((bsz, e, t), I32), jax.ShapeDtypeStruct((bsz, t, e), I32),
                   jax.ShapeDtypeStruct((bsz, t, e), F32),
                   jax.ShapeDtypeStruct((bsz, e, n_tiles), I32),
                   jax.ShapeDtypeStruct((bsz, e, n_st), I32), jax.ShapeDtypeStruct((bsz, e, n_st), I32)],
        scratch_shapes=[pltpu.VMEM((e, t), F32), pltpu.VMEM((e, LANES), F32)],
        compiler_params=_cparams("parallel"),
        name="select",
    )(aff, tri)


def _window_start(r0, cap, win, align):
    return pl.multiple_of(jnp.minimum((r0 // align) * align, cap - win), align)


def _dispatch_kernel(jlo_ref, nch_ref, x_ref, slot_ref, o_ref, acc_ref, *, st, kc, tt):
    b, e = pl.program_id(0), pl.program_id(1)
    t_total = x_ref.shape[0]
    n_st = o_ref.shape[0] // st

    def chunk(s, c):
        want = (jlo_ref[b, e, s] + c * (kc // tt)) * tt
        t0 = pl.multiple_of(jnp.minimum(want, t_total - kc), tt)
        tok = t0 + lax.broadcasted_iota(I32, (1, kc), 1)
        rel = jnp.where(tok >= want, slot_ref[:, pl.ds(t0, kc)] - s * st, -1)
        onehot = jnp.where(lax.broadcasted_iota(I32, (st, kc), 0) == rel, 1.0, 0.0).astype(BF16)
        return _dot(onehot, x_ref[pl.ds(t0, kc), :])

    for s in range(n_st):
        acc_ref[s] = chunk(s, 0)
    for s in range(n_st):
        def body(c, carry, s=s):
            acc_ref[s] += chunk(s, c)
            return carry
        lax.fori_loop(1, nch_ref[b, e, s], body, 0)
    for s in range(n_st):
        o_ref[s * st:(s + 1) * st, :] = acc_ref[s].astype(o_ref.dtype)


def _dispatch(jlo, nch, xb, slot4, *, cap, st, kc, tt):
    bsz, t, d = xb.shape
    e = slot4.shape[1]
    assert cap % st == 0 and kc % tt == 0 and t % tt == 0 and kc <= t
    return pl.pallas_call(
        functools.partial(_dispatch_kernel, st=st, kc=kc, tt=tt),
        grid_spec=pltpu.PrefetchScalarGridSpec(
            num_scalar_prefetch=2,
            grid=(bsz, e),
            in_specs=[pl.BlockSpec((None, t, d), lambda i, j, a, c: (i, 0, 0)),
                      pl.BlockSpec((None, None, 1, t), lambda i, j, a, c: (i, j, 0, 0))],
            out_specs=pl.BlockSpec((None, None, cap, d), lambda i, j, a, c: (i, j, 0, 0)),
            scratch_shapes=[pltpu.VMEM((cap // st, st, d), F32)]),
        out_shape=jax.ShapeDtypeStruct((bsz, e, cap, d), BF16),
        compiler_params=_cparams("parallel", "parallel"),
        name="dispatch",
    )(jlo, nch, xb, slot4)


def _expert_kernel(x_ref, wg_ref, wu_ref, wd_ref, o_ref, acc_ref):
    f = pl.program_id(2)
    wg = wg_ref[...].astype(BF16)
    wu = wu_ref[...].astype(BF16)
    wd = wd_ref[...].astype(BF16)
    @pl.when(f == 0)
    def _():
        acc_ref[...] = jnp.zeros_like(acc_ref)

    for bi in range(x_ref.shape[0]):
        x = x_ref[bi]
        hg = _dot(x, wg)
        hu = _dot(x, wu)
        act = (hg * _sigmoid(hg) * hu).astype(BF16)
        acc_ref[bi] += _dot(act, wd)

    @pl.when(f == pl.num_programs(2) - 1)
    def _():
        o_ref[...] = acc_ref[...].astype(o_ref.dtype)


def _expert(xg, wg, wu, wd, *, fc, nb):
    bsz, e, cap, d = xg.shape
    f = wg.shape[2]
    assert bsz % nb == 0 and f % fc == 0
    tok = pl.BlockSpec((nb, None, cap, d), lambda i, j, k: (j, i, 0, 0))
    return pl.pallas_call(
        _expert_kernel,
        grid=(e, bsz // nb, f // fc),
        in_specs=[tok,
                  pl.BlockSpec((None, d, fc), lambda i, j, k: (i, 0, k)),
                  pl.BlockSpec((None, d, fc), lambda i, j, k: (i, 0, k)),
                  pl.BlockSpec((None, fc, d), lambda i, j, k: (i, k, 0))],
        out_specs=tok,
        out_shape=jax.ShapeDtypeStruct((bsz, e, cap, d), BF16),
        scratch_shapes=[pltpu.VMEM((nb, cap, d), F32)],
        compiler_params=_cparams("parallel", "parallel", "arbitrary"),
        name="expert",
    )(xg, wg, wu, wd)


def _combine_kernel(r0_ref, x_ref, slot_ref, gate_ref, g_ref, b_ref, y_ref, o_ref, *, tt, win, align, alpha):
    b, j = pl.program_id(0), pl.program_id(1)
    n_e, cap = y_ref.shape[0], y_ref.shape[1]
    n_sub = x_ref.shape[0] // tt
    for u in range(n_sub):
        rs = slice(u * tt, (u + 1) * tt)
        acc = alpha * x_ref[rs, :]
        slots = slot_ref[rs, :]
        gates = gate_ref[rs, :]
        for e in range(n_e):
            start = _window_start(r0_ref[b, e, j * n_sub + u], cap, win, align)
            rel = slots[:, e:e + 1] - start
            pick = jnp.where(lax.broadcasted_iota(I32, (tt, win), 1) == rel, gates[:, e:e + 1], 0.0)
            acc = acc + _dot(pick.astype(BF16), y_ref[e, pl.ds(start, win), :])
        o_ref[rs, :] = _ln(acc, g_ref[...], b_ref[...])


def _combine(r0, x, slot_t, gate_t, g, b, y, *, tt, n_sub, win, align, alpha):
    bsz, t, d = x.shape
    e, cap = y.shape[1], y.shape[2]
    assert win >= tt + align and win <= cap and (cap - win) % align == 0
    ts = tt * n_sub
    const = lambda i, j, r: (0, 0)
    return pl.pallas_call(
        functools.partial(_combine_kernel, tt=tt, win=win, align=align, alpha=alpha),
        grid_spec=pltpu.PrefetchScalarGridSpec(
            num_scalar_prefetch=1,
            grid=(bsz, t // ts),
            in_specs=[pl.BlockSpec((None, ts, d), lambda i, j, r: (i, j, 0)),
                      pl.BlockSpec((None, ts, e), lambda i, j, r: (i, j, 0)),
                      pl.BlockSpec((None, ts, e), lambda i, j, r: (i, j, 0)),
                      pl.BlockSpec((1, d), const), pl.BlockSpec((1, d), const),
                      pl.BlockSpec((None, e, cap, d), lambda i, j, r: (i, 0, 0, 0),
                                   pipeline_mode=pl.Buffered(1))],
            out_specs=pl.BlockSpec((None, ts, d), lambda i, j, r: (i, j, 0))),
        out_shape=jax.ShapeDtypeStruct((bsz, t, d), F32),
        compiler_params=_cparams("parallel", "parallel"),
        name="combine",
    )(r0, x, slot_t, gate_t, g, b, y)


def _hyena_constants(length):
    n = 2 * length
    m = np.arange(n)
    pos = np.where(m < length, m, n - m).astype(np.float64)
    pos[length] = 0.0
    t = pos / max(length - 1, 1)
    f = np.linspace(1e-4, HY_BANDS - 1, HY_BANDS)
    ang = (2.0 * np.pi * pos / length)[:, None] * f[None, :]
    feat = np.concatenate([t[:, None], np.cos(ang), -np.sin(ang)], axis=-1)
    feat_t = np.zeros((LANES, n), np.float32)
    feat_t[:feat.shape[1]] = feat.T
    valid = np.ones((1, n), np.float32)
    valid[0, length] = 0.0
    return feat_t, t[None, :].astype(np.float32), valid


def _decay_rates(width):
    max_decay = math.log(HY_DECAY_TARGET) / HY_FAST_DECAY
    min_decay = math.log(HY_DECAY_TARGET) / HY_SLOW_DECAY
    return np.abs(np.linspace(min_decay, max_decay, width, dtype=np.float32)).astype(np.float32)


def _dft_constants(r1):
    n, a1h, a2, at = _dft_tables(r1, r1 // 2)
    _, a1f, _, _ = _dft_tables(r1, r1)
    bf = lambda x: jnp.asarray(x, F32).astype(BF16)
    w1_data = bf(_cplx_rhs(a1h, -1.0))
    w1_filt = bf(np.concatenate([np.cos(a1f), -np.sin(a1f)], axis=1))
    w2 = bf(_cplx_rhs(a2, -1.0))
    w3 = bf(_cplx_rhs(a2.T, 1.0))
    w4 = bf(_cplx_rhs(a1h.T, 1.0) / n)
    twr, twi = np.cos(at), -np.sin(at)
    f32 = lambda x: jnp.asarray(x, F32)
    return dict(w1_data=w1_data, w1_filt=w1_filt, w2=w2, w3=w3, w4=w4,
                twr=f32(twr), twi=f32(twi), twr_t=f32(twr.T), twi_t=f32(twi.T))


def _tile_plan(length, hy_width, d_ff):
    tt = LANES
    return dict(
        filt_ffn_tn=min(2048, 2 * length),
        filt_rows=min(128, hy_width),
        kfft_rows=min(32, hy_width),
        inproj_tm=min(512, length),
        conf_tl=min(1024, length),
        hyena_g=min(32, hy_width),
        mix_tm=min(1024, length), mix_sub=512,
        xattn_tm=min(1024, length), xattn_sub=1024,
        tt=tt,
        st=LANES,
        kc=min(10 * tt, length),
        align=2 * SUBLANES,
        expert_fc=min(512, d_ff), expert_nb=2,
        combine_sub=min(4, length // tt),
    )


def kernel(x, mem, ln_in_g, ln_in_b, w_in, b_gate, conf_dw_w, conf_dw_b, conf_ln_g, conf_ln_b, conf_w_out, hy_short_w, hy_short_b, hy_ffn_w1, hy_ffn_b1, hy_freq1, hy_ffn_w2, hy_ffn_b2, hy_freq2, hy_ffn_w3, hy_skip, hy_w_out, w_mix_out, ln_mix_g, ln_mix_b, xa_wq, xa_wk, xa_wv, xa_wo, ln_xa_g, ln_xa_b, moe_w_router, moe_w_gate, moe_w_up, moe_w_down, ln_moe_g, ln_moe_b):
    bsz, length, d = x.shape
    depth = w_in.shape[0]
    alpha = (2.0 * depth) ** 0.25
    cw = conf_dw_w.shape[2]
    hw = hy_skip.shape[2]
    n_glu, n_hy = 2 * cw, 3 * hw
    n_exp = moe_w_router.shape[2]
    cap = max(1, N_EXPERT_CAPACITY * length // n_exp)
    rh = length // LANES
    r1 = 2 * rh
    assert bsz % 2 == 0 and length % LANES == 0

    row = lambda v: v.reshape(1, -1).astype(F32)
    col = lambda v: v.reshape(-1, 1).astype(F32)

    feat_t, t_ext, valid = _hyena_constants(length)
    dft = _dft_constants(r1)
    rates = _decay_rates(hw)
    tri = jnp.asarray(np.triu(np.ones((LANES, LANES), np.float32))).astype(BF16)

    n_ord = hy_skip.shape[1]
    tp = _tile_plan(length, hw, moe_w_gate.shape[3])
    lane_b = lambda v: jnp.broadcast_to(v[..., None, None], v.shape + (1, LANES)).astype(F32)
    dft_data = (dft["w1_data"], dft["twr"], dft["twi"], dft["w2"], dft["w3"], dft["twr_t"], dft["twi_t"],
                dft["w4"])

    h = x
    for i in range(depth):
        w1h, w1l = _split_bf16(jnp.pad(hy_ffn_w1[i].T, ((0, 0), (0, LANES - hy_ffn_w1.shape[1]))))
        w2h, w2l = _split_bf16(hy_ffn_w2[i].T)
        h2t = _filt_ffn(jnp.asarray(feat_t), w1h, w1l, col(hy_ffn_b1[i]), col(hy_freq1[i]),
                        w2h, w2l, col(hy_ffn_b2[i]), col(hy_freq2[i]), tn=tp["filt_ffn_tn"])
        w3 = hy_ffn_w3[i].reshape(-1, n_ord, 2, hw)
        w3f = jnp.transpose(w3[:, :, 0], (1, 2, 0)).reshape(n_ord * hw, -1)
        w3b = jnp.transpose(w3[:, :, 1], (1, 2, 0)).reshape(n_ord * hw, -1)
        w3fh, w3fl = _split_bf16(w3f)
        w3bh, w3bl = _split_bf16(w3b)
        delta = jnp.asarray(np.tile(rates, n_ord).reshape(-1, 1))
        kern = _filt(h2t, w3fh, w3fl, w3bh, w3bl, delta, jnp.asarray(t_ext), jnp.asarray(valid),
                     g=tp["filt_rows"])
        kf = _kfft(kern.reshape(n_ord * hw, r1, LANES), dft["w1_filt"], dft["twr"], dft["twi"], dft["w2"],
                   g=tp["kfft_rows"])

        wi = w_in[i]
        wglu = wi[:, :n_glu].astype(BF16)
        whyt = wi[:, n_glu:n_glu + n_hy].T.astype(BF16)
        wgate = wi[:, n_glu + n_hy:].astype(BF16)
        hn, u, hyt, gates = _inproj(h, row(ln_in_g), row(ln_in_b), wglu, whyt, wgate, row(b_gate[i]),
                                    tm=tp["inproj_tm"], apply_ln=(i == 0))

        ua = _conf(u, conf_dw_w[i], row(conf_dw_b[i]), row(conf_ln_g[i]), row(conf_ln_b[i]),
                   tl=tp["conf_tl"])

        z = _hyena(hyt.reshape(bsz, n_hy, rh, LANES), lane_b(hy_short_w[i]), lane_b(hy_short_b[i]), kf,
                   lane_b(hy_skip[i]), dft_data, g=tp["hyena_g"])

        x1 = _mix(ua, z, gates, hn, conf_w_out[i].astype(BF16),
                  hy_w_out[i].astype(BF16), w_mix_out[i].astype(BF16), row(ln_mix_g[i]), row(ln_mix_b[i]),
                  tm=tp["mix_tm"], sub=tp["mix_sub"], alpha=alpha)

        wrh, wrl = _split_bf16(moe_w_router[i].T)
        x2, x2b, aff = _xattn(x1, mem, xa_wk[i].astype(BF16), xa_wv[i].astype(BF16),
                              xa_wq[i].astype(BF16), xa_wo[i].astype(BF16), row(ln_xa_g[i]),
                              row(ln_xa_b[i]), wrh, wrl, tm=tp["xattn_tm"], sub=tp["xattn_sub"], alpha=alpha,
                              heads=XA_HEADS)

        tt, st, kc, align = tp["tt"], tp["st"], tp["kc"], tp["align"]
        slot, slot_t, gate_t, r0, jlo, nch = _select(aff, tri, cap=cap, st=st, kct=kc // tt)
        xg = _dispatch(jlo, nch, x2b, slot.reshape(bsz, n_exp, 1, length), cap=cap, st=st, kc=kc, tt=tt)
        y = _expert(xg, moe_w_gate[i], moe_w_up[i], moe_w_down[i], fc=tp["expert_fc"], nb=tp["expert_nb"])
        h = _combine(r0, x2, slot_t, gate_t, row(ln_moe_g[i]), row(ln_moe_b[i]), y,
                     tt=tt, n_sub=tp["combine_sub"], win=tt + align, align=align, alpha=alpha)
    return h
```
